```python
import jax, jax.numpy as jnp
from jax import lax
import numpy as np

D_MODEL = 1024
BATCH = 16
SEQ = 256
DEPTH = 2
DEC_BATCH = 8
DEC_SEQ = 2048
PAST_LEN = 512

GRID_W = 64
N_MIXERS = 2
N_RWKV = (DEPTH + 1) // 2
N_CONV = DEPTH // 2
HEAD_DIM = 64
N_HEADS = D_MODEL // HEAD_DIM
D_DECAY_LORA = 64
D_AAA_LORA = 64
D_GATE_LORA = 160
CONV_WIDTH = 31
N_EXPERTS = 16
N_EXPERT_GROUPS = 4
EXPERTS_PER_GROUP = N_EXPERTS // N_EXPERT_GROUPS
TOP_K = 2
D_EXPERT = 512
RMS_EPS = 1e-6
LN_EPS = 1e-5
GN_EPS = 64e-5

kernel_name = 'bidir_rwkv7_conformer_grouped_moe_diffusion_step'


def rms_norm(x, g):
    xf = x.astype(jnp.float32)
    y = xf * lax.rsqrt(jnp.mean(xf * xf, -1, keepdims=True) + RMS_EPS)
    return (y * g).astype(x.dtype)


def layer_norm(x, g, b):
    xf = x.astype(jnp.float32)
    mu = jnp.mean(xf, -1, keepdims=True)
    var = jnp.mean(jnp.square(xf - mu), -1, keepdims=True)
    return ((xf - mu) * lax.rsqrt(var + LN_EPS) * g + b).astype(x.dtype)


def modulate(x, shift, scale):
    return x * (1 + scale[:, None, :]) + shift[:, None, :]


def centred_shift(x):
    prev = jnp.pad(x, ((0, 0), (1, 0), (0, 0)))[:, :-1]
    nxt = jnp.pad(x, ((0, 0), (0, 1), (0, 0)))[:, 1:]
    return 0.5 * (prev + nxt)


def wkv_scan(r, decay, k, v, a_vec, b_vec, s0):
    def step(s, inp):
        r_t, d_t, k_t, v_t, a_t, b_t = inp
        sa = jnp.einsum('bhvk,bhk->bhv', s, a_t)
        s = s * d_t[:, :, None, :] + sa[..., None] * b_t[:, :, None, :] + v_t[..., None] * k_t[:, :, None, :]
        return s, jnp.einsum('bhvk,bhk->bhv', s, r_t)
    xs = tuple(jnp.swapaxes(t, 0, 1) for t in (r, decay, k, v, a_vec, b_vec))
    s_fin, ys = lax.scan(step, s0, xs)
    return jnp.swapaxes(ys, 0, 1), s_fin


def rwkv_time_mix(h, s0, p, j):
    B, T, D = h.shape
    f32 = jnp.float32
    xx = centred_shift(h) - h
    mix = h[:, :, None, :] + xx[:, :, None, :] * p['rwkv_mu'][j]
    xr, xw, xk, xv, xa, xg = (mix[:, :, i] for i in range(6))
    w_rkv = p['rwkv_w_rkv'][j]
    r = xr @ w_rkv[0]
    k = xk @ w_rkv[1]
    v = xv @ w_rkv[2]
    g = jax.nn.sigmoid(xg @ p['rwkv_g1'][j]) @ p['rwkv_g2'][j]
    heads = lambda t: t.astype(f32).reshape(B, T, N_HEADS, HEAD_DIM)
    r_h, v_h = heads(r), heads(v)
    kk = heads(k * p['rwkv_k_k'][j])
    kk = kk * lax.rsqrt(jnp.maximum(jnp.sum(kk * kk, -1, keepdims=True), 1e-24))
    y = jnp.zeros_like(r_h)
    bonus = jnp.zeros_like(r_h)
    finals = []
    for d in range(2):
        w_log = -jax.nn.softplus(-(p['rwkv_w0'][j, d] + jnp.tanh(xw @ p['rwkv_w1'][j, d]) @ p['rwkv_w2'][j, d])) - 0.5
        decay = jnp.exp(-jnp.exp(heads(w_log)))
        a = jax.nn.sigmoid(p['rwkv_a0'][j, d] + (xa @ p['rwkv_a1'][j, d]) @ p['rwkv_a2'][j, d])
        k_d = heads(k * (1 + (a - 1) * p['rwkv_k_a'][j]))
        b_vec = kk * heads(a)
        seqs = (r_h, decay, k_d, v_h, -kk, b_vec)
        if d == 1:
            seqs = tuple(jnp.flip(t, 1) for t in seqs)
        y_d, s_d = wkv_scan(*seqs, s0[:, d].astype(f32))
        if d == 1:
            y_d = jnp.flip(y_d, 1)
        y = y + y_d
        bonus = bonus + jnp.sum(r_h * k_d * p['rwkv_r_k'][j], -1, keepdims=True) * v_h
        finals.append(s_d)
    mean = jnp.mean(y, -1, keepdims=True)
    var = jnp.mean(jnp.square(y - mean), -1, keepdims=True)
    y = ((y - mean) * lax.rsqrt(var + GN_EPS)).reshape(B, T, D) * p['rwkv_gn_g'][j] + p['rwkv_gn_b'][j]
    y = y + bonus.reshape(B, T, D)
    out = (y.astype(h.dtype) * g) @ p['rwkv_w_o'][j]
    return out, jnp.stack(finals, 1)


def conv_module(h, is_grid, p, j):
    B, T, D = h.shape
    u = h @ p['conv_pw1'][j] + p['conv_pw1_b'][j]
    u = u[..., :D] * jax.nn.sigmoid(u[..., D:])
    if is_grid:
        rows = T // GRID_W
        u = u.reshape(B * rows, GRID_W, D)
    u = lax.conv_general_dilated(
        u, p['conv_dw'][j][:, None, :], window_strides=(1,),
        padding=[(CONV_WIDTH // 2, CONV_WIDTH // 2)],
        dimension_numbers=('NWC', 'WIO', 'NWC'), feature_group_count=D) + p['conv_dw_b'][j]
    u = u.reshape(B, T, D)
    u = jax.nn.silu(layer_norm(u, p['conv_ln_g'][j], p['conv_ln_b'][j]))
    return u @ p['conv_pw2'][j] + p['conv_pw2_b'][j]


def grouped_moe(h, p, i):
    B, T, D = h.shape
    x = h.reshape(-1, D)
    logits = (x @ p['router_w'] + p['router_b']).astype(jnp.float32)
    probs = jax.nn.softmax(logits, -1)
    grouped = probs.reshape(-1, N_EXPERT_GROUPS, EXPERTS_PER_GROUP)
    group_score = jnp.sum(lax.top_k(grouped, TOP_K)[0], -1)
    sel = jnp.argmax(group_score, -1)
    in_group = jnp.sum(grouped * jax.nn.one_hot(sel, N_EXPERT_GROUPS, dtype=jnp.float32)[..., None], 1)
    vals, idx = lax.top_k(in_group, TOP_K)
    experts = sel[:, None] * EXPERTS_PER_GROUP + idx
    weights = vals / jnp.sum(vals, -1, keepdims=True)
    gates = jnp.einsum('nk,nke->ne', weights, jax.nn.one_hot(experts, N_EXPERTS, dtype=jnp.float32)).astype(x.dtype)
    y = jnp.zeros_like(x)
    for e in range(N_EXPERTS):
        he = jax.nn.silu(x @ p['moe_w_gate'][i, e]) * (x @ p['moe_w_up'][i, e])
        y = y + gates[:, e:e + 1] * (he @ p['moe_w_down'][i, e])
    return y.reshape(B, T, D)


def run_trunk(x, cond, s0, is_grid, p):
    new_states = []
    for i in range(DEPTH):
        mod = jax.nn.silu(cond) @ p['ada_w'][i] + p['ada_b'][i]
        sh1, sc1, gt1, sh2, sc2, gt2 = jnp.split(mod, 6, -1)
        j = i // N_MIXERS
        h = modulate(rms_norm(x, p['norm_g'][i, 0]), sh1, sc1)
        if i % N_MIXERS == 0:
            out, s_fin = rwkv_time_mix(h, s0[:, j], p, j)
            new_states.append(s_fin.astype(x.dtype))
        else:
            out = conv_module(h, is_grid, p, j)
        x = x + gt1[:, None, :] * out
        h = modulate(rms_norm(x, p['norm_g'][i, 1]), sh2, sc2)
        x = x + gt2[:, None, :] * grouped_moe(h, p, i)
    return rms_norm(x, p['final_g']), jnp.stack(new_states, 1)


def setup_inputs(seed: int = 0) -> dict:
    key = jax.random.key(seed)
    ks = iter(jax.random.split(key, 48))
    nrm = lambda shape, s: jax.random.normal(next(ks), shape, jnp.float32) * s
    D = D_MODEL
    return {
        'x_prompt': nrm((BATCH, SEQ, D), 1.0),
        'x_sample': nrm((DEC_BATCH, DEC_SEQ, D), 1.0),
        'state_rwkv': nrm((DEC_BATCH, N_RWKV, 2, N_HEADS, HEAD_DIM, HEAD_DIM), 0.5),
        'c': nrm((DEC_BATCH, D), 1.0),
        'c_ctx': nrm((D,), 1.0),
        'norm_g': 1.0 + nrm((DEPTH, 2, D), 0.02),
        'ada_w': nrm((DEPTH, D, 6 * D), 0.5 * D ** -0.5),
        'ada_b': nrm((DEPTH, 6 * D), 0.02),
        'final_g': 1.0 + nrm((D,), 0.02),
        'rwkv_mu': jax.random.uniform(next(ks), (N_RWKV, 6, D), jnp.float32),
        'rwkv_w_rkv': nrm((N_RWKV, 3, D, D), D ** -0.5),
        'rwkv_w_o': nrm((N_RWKV, D, D), D ** -0.5),
        'rwkv_w0': -2.0 + nrm((N_RWKV, 2, D), 0.5),
        'rwkv_w1': nrm((N_RWKV, 2, D, D_DECAY_LORA), D ** -0.5),
        'rwkv_w2': nrm((N_RWKV, 2, D_DECAY_LORA, D), 0.3 * D_DECAY_LORA ** -0.5),
        'rwkv_a0': nrm((N_RWKV, 2, D), 0.1),
        'rwkv_a1': nrm((N_RWKV, 2, D, D_AAA_LORA), D ** -0.5),
        'rwkv_a2': nrm((N_RWKV, 2, D_AAA_LORA, D), 0.3 * D_AAA_LORA ** -0.5),
        'rwkv_g1': nrm((N_RWKV, D, D_GATE_LORA), D ** -0.5),
        'rwkv_g2': nrm((N_RWKV, D_GATE_LORA, D), D_GATE_LORA ** -0.5),
        'rwkv_k_k': 0.85 + nrm((N_RWKV, D), 0.02),
        'rwkv_k_a': 1.0 + nrm((N_RWKV, D), 0.02),
        'rwkv_r_k': nrm((N_RWKV, N_HEADS, HEAD_DIM), 0.1),
        'rwkv_gn_g': 1.0 + nrm((N_RWKV, D), 0.02),
        'rwkv_gn_b': nrm((N_RWKV, D), 0.02),
        'conv_pw1': nrm((N_CONV, D, 2 * D), D ** -0.5),
        'conv_pw1_b': nrm((N_CONV, 2 * D), 0.01),
        'conv_dw': nrm((N_CONV, CONV_WIDTH, D), CONV_WIDTH ** -0.5),
        'conv_dw_b': nrm((N_CONV, D), 0.01),
        'conv_ln_g': 1.0 + nrm((N_CONV, D), 0.02),
        'conv_ln_b': nrm((N_CONV, D), 0.02),
        'conv_pw2': nrm((N_CONV, D, D), D ** -0.5),
        'conv_pw2_b': nrm((N_CONV, D), 0.01),
        'router_w': nrm((D, N_EXPERTS), D ** -0.5),
        'router_b': nrm((N_EXPERTS,), 0.01),
        'moe_w_gate': nrm((DEPTH, N_EXPERTS, D, D_EXPERT), D ** -0.5),
        'moe_w_up': nrm((DEPTH, N_EXPERTS, D, D_EXPERT), D ** -0.5),
        'moe_w_down': nrm((DEPTH, N_EXPERTS, D_EXPERT, D), D_EXPERT ** -0.5),
    }


def reference(x_prompt, x_sample, state_rwkv, c, c_ctx,
              norm_g, ada_w, ada_b, final_g,
              rwkv_mu, rwkv_w_rkv, rwkv_w_o, rwkv_w0, rwkv_w1, rwkv_w2,
              rwkv_a0, rwkv_a1, rwkv_a2, rwkv_g1, rwkv_g2, rwkv_k_k, rwkv_k_a,
              rwkv_r_k, rwkv_gn_g, rwkv_gn_b,
              conv_pw1, conv_pw1_b, conv_dw, conv_dw_b, conv_ln_g, conv_ln_b,
              conv_pw2, conv_pw2_b,
              router_w, router_b, moe_w_gate, moe_w_up, moe_w_down):
    p = {
        'norm_g': norm_g, 'ada_w': ada_w, 'ada_b': ada_b, 'final_g': final_g,
        'rwkv_mu': rwkv_mu, 'rwkv_w_rkv': rwkv_w_rkv, 'rwkv_w_o': rwkv_w_o,
        'rwkv_w0': rwkv_w0, 'rwkv_w1': rwkv_w1, 'rwkv_w2': rwkv_w2,
        'rwkv_a0': rwkv_a0, 'rwkv_a1': rwkv_a1, 'rwkv_a2': rwkv_a2,
        'rwkv_g1': rwkv_g1, 'rwkv_g2': rwkv_g2, 'rwkv_k_k': rwkv_k_k, 'rwkv_k_a': rwkv_k_a,
        'rwkv_r_k': rwkv_r_k, 'rwkv_gn_g': rwkv_gn_g, 'rwkv_gn_b': rwkv_gn_b,
        'conv_pw1': conv_pw1, 'conv_pw1_b': conv_pw1_b, 'conv_dw': conv_dw, 'conv_dw_b': conv_dw_b,
        'conv_ln_g': conv_ln_g, 'conv_ln_b': conv_ln_b, 'conv_pw2': conv_pw2, 'conv_pw2_b': conv_pw2_b,
        'router_w': router_w, 'router_b': router_b,
        'moe_w_gate': moe_w_gate, 'moe_w_up': moe_w_up, 'moe_w_down': moe_w_down,
    }
    ctx_s0 = jnp.zeros((x_prompt.shape[0], N_RWKV, 2, N_HEADS, HEAD_DIM, HEAD_DIM), x_prompt.dtype)
    y_prompt, new_state_rwkv = run_trunk(x_prompt, c_ctx[None, :], ctx_s0, False, p)
    y_sample, _ = run_trunk(x_sample, c, state_rwkv, True, p)
    return (y_prompt, y_sample, new_state_rwkv)
```

```python
import functools

import jax
import jax.numpy as jnp
from jax import lax
from jax.experimental import pallas as pl
from jax.experimental.pallas import tpu as pltpu

F32 = jnp.float32
BF16 = jnp.bfloat16

HEAD_DIM = 64
HEADS_PER_BLOCK = 4
BLOCK_LANES = HEAD_DIM * HEADS_PER_BLOCK
CHUNK = 64
GRID_W = 64
CONV_WIDTH = 31
CONV_PAD = 16
N_EXPERT_GROUPS = 4
EXPERTS_PER_GROUP = 4
LANE = 128
COND_ROWS = 16
RMS_EPS = 1e-6
LN_EPS = 1e-5
GN_EPS = 64e-5
VMEM_LIMIT_BYTES = 56 * 1024 * 1024


def _bf(x):
    return x.astype(BF16)


def _dot(a, b):
    return jnp.dot(a, b, preferred_element_type=F32)


def _dot_nt(a, b):
    return lax.dot_general(a, b, (((1,), (1,)), ((), ())), preferred_element_type=F32)


def _split(x):
    hi = _bf(x)
    lo = _bf(x - hi.astype(F32))
    return hi, lo


def _dot_split(a, b_exact):
    hi, lo = _split(a)
    return _dot(hi, b_exact) + _dot(lo, b_exact)


def _rms_mod(x, g, shift, scale):
    y = x * lax.rsqrt(jnp.mean(x * x, axis=-1, keepdims=True) + RMS_EPS) * g
    return y * (1.0 + scale) + shift


def _head_sum(z, hsel, hselt):
    return _dot_split(_dot_split(z, hsel), hselt)


def _params(sem):
    return pltpu.CompilerParams(dimension_semantics=sem, vmem_limit_bytes=VMEM_LIMIT_BYTES)


def _const_spec(shape):
    nd = len(shape)
    return pl.BlockSpec(shape, lambda *_: (0,) * nd, pipeline_mode=pl.Buffered(1))


def _mod_kernel(c_ref, w_ref, b_ref, o_ref):
    c = c_ref[...]
    s = c * jax.nn.sigmoid(c)
    o_ref[0] = jnp.dot(s, w_ref[0], preferred_element_type=F32,
                       precision=lax.Precision.HIGHEST) + b_ref[0]


def _mod_call(cond, ada_w, ada_b):
    depth, d, n = ada_w.shape
    tn = 1536
    return pl.pallas_call(
        _mod_kernel,
        grid=(depth, n // tn),
        in_specs=[
            pl.BlockSpec((COND_ROWS, d), lambda l, j: (0, 0)),
            pl.BlockSpec((1, d, tn), lambda l, j: (l, 0, j)),
            pl.BlockSpec((1, 1, tn), lambda l, j: (l, 0, j)),
        ],
        out_specs=pl.BlockSpec((1, COND_ROWS, tn), lambda l, j: (l, 0, j)),
        out_shape=jax.ShapeDtypeStruct((depth, COND_ROWS, n), F32),
        compiler_params=_params(("arbitrary", "arbitrary")),
        name="adaln_mod",
    )(cond, ada_w, ada_b.reshape(depth, 1, n))


def _rwkv_pre_kernel(x_ref, xp_ref, xn_ref, mod_ref, vec_ref, mu_ref, wrkv_ref, g1_ref, g2_ref,
                     w1_ref, w2_ref, a1_ref, a2_ref, hsel_ref, hselt_ref,
                     r_ref, v_ref, kkn_ref, logd_ref, kd_ref, bd_ref, gate_ref, bonus_ref,
                     *, tm, nt):
    t = pl.program_id(1)
    mod = mod_ref[0]
    shift, scale = mod[0:1], mod[1:2]
    vec = vec_ref[...]
    g, k_k, k_a, r_k = vec[0:1], vec[1:2], vec[2:3], vec[3:4]
    hsel = hsel_ref[...]
    hselt = hselt_ref[...]

    h = _rms_mod(x_ref[0], g, shift, scale)
    h_prev = _rms_mod(xp_ref[0], g, shift, scale)[7:8]
    h_next = _rms_mod(xn_ref[0], g, shift, scale)[0:1]
    h_prev = jnp.where(t == 0, 0.0, h_prev)
    h_next = jnp.where(t == nt - 1, 0.0, h_next)
    row = lax.broadcasted_iota(jnp.int32, (tm, 1), 0)
    prev = jnp.where(row == 0, h_prev, pltpu.roll(h, 1, 0))
    nxt = jnp.where(row == tm - 1, h_next, pltpu.roll(h, tm - 1, 0))
    xx = 0.5 * (prev + nxt) - h
    mu = mu_ref[...]

    def mix(i):
        return _bf(h + xx * mu[i:i + 1])

    r = _dot(mix(0), wrkv_ref[0])
    k = _dot(mix(2), wrkv_ref[1])
    v = _dot(mix(3), wrkv_ref[2])
    gate = _dot(_bf(jax.nn.sigmoid(_dot(mix(5), g1_ref[...]))), g2_ref[...])
    tw = _bf(jnp.tanh(_dot(mix(1), w1_ref[...])))
    ta = _bf(_dot(mix(4), a1_ref[...]))

    kk = k * k_k
    kkn = kk * lax.rsqrt(jnp.maximum(_head_sum(kk * kk, hsel, hselt), 1e-24))
    r_ref[0] = r
    v_ref[0] = v
    kkn_ref[0] = kkn
    gate_ref[0] = gate

    ksum = None
    for d in range(2):
        u = vec[4 + d:5 + d] + _dot(tw, w2_ref[d])
        w_log = jnp.minimum(u, 0.0) - jnp.log(1.0 + jnp.exp(-jnp.abs(u))) - 0.5
        logd_ref[d, 0] = -jnp.exp(w_log)
        a = jax.nn.sigmoid(vec[6 + d:7 + d] + _dot(ta, a2_ref[d]))
        kd = k * (1.0 + (a - 1.0) * k_a)
        kd_ref[d, 0] = kd
        bd_ref[d, 0] = kkn * a
        ksum = kd if ksum is None else ksum + kd
    bonus_ref[0] = _head_sum(r * r_k * ksum, hsel, hselt) * v


def _rwkv_pre_call(x, mod, per_batch, vec, mu, wrkv, g1, g2, w1c, w2p, a1c, a2p, hsel, hselt):
    b, t, d = x.shape
    tm = min(256, t)
    nt = t // tm
    r8 = tm // 8
    nb8 = t // 8
    mod_map = (lambda i, j: (i, 0, 0)) if per_batch else (lambda i, j: (0, 0, 0))
    tile = pl.BlockSpec((1, tm, d), lambda i, j: (i, j, 0))
    tile2 = pl.BlockSpec((2, 1, tm, d), lambda i, j: (0, i, j, 0))
    one = jax.ShapeDtypeStruct((b, t, d), F32)
    two = jax.ShapeDtypeStruct((2, b, t, d), F32)
    return pl.pallas_call(
        functools.partial(_rwkv_pre_kernel, tm=tm, nt=nt),
        grid=(b, nt),
        in_specs=[
            tile,
            pl.BlockSpec((1, 8, d), lambda i, j: (i, jnp.maximum(j * r8 - 1, 0), 0)),
            pl.BlockSpec((1, 8, d), lambda i, j: (i, jnp.minimum((j + 1) * r8, nb8 - 1), 0)),
            pl.BlockSpec((1, 6, d), mod_map),
            _const_spec(vec.shape), _const_spec(mu.shape), _const_spec(wrkv.shape),
            _const_spec(g1.shape), _const_spec(g2.shape), _const_spec(w1c.shape),
            _const_spec(w2p.shape), _const_spec(a1c.shape), _const_spec(a2p.shape),
            _const_spec(hsel.shape), _const_spec(hselt.shape),
        ],
        out_specs=[tile, tile, tile, tile2, tile2, tile2, tile, tile],
        out_shape=[one, one, one, two, two, two, one, one],
        compiler_params=_params(("arbitrary", "arbitrary")),
        name="rwkv_pre",
    )(x, x, x, mod, vec, mu, wrkv, g1, g2, w1c, w2p, a1c, a2p, hsel, hselt)


def _wkv_direction(r, v, kkn, logd, kd, bd, vt_pad_rows, state, *, reverse):
    c = CHUNK
    n = BLOCK_LANES
    vt, v_first = vt_pad_rows
    ri = lax.broadcasted_iota(jnp.int32, (c, c), 0)
    ci = lax.broadcasted_iota(jnp.int32, (c, c), 1)
    lmat = _bf(jnp.where((ri <= ci) if reverse else (ri >= ci), 1.0, 0.0))
    ld_hi, ld_lo = _split(logd)
    cum = _dot(lmat, ld_hi) + _dot(lmat, ld_lo)
    tot = cum[0:1] if reverse else cum[c - 1:c]
    e_in = jnp.exp(cum)
    e_ex = jnp.exp(cum - logd)
    e_inv = jnp.exp(-cum)
    e_bar = jnp.exp(tot - cum)
    p_c = jnp.exp(tot)

    a_t = -kkn * e_ex
    r_t = r * e_in
    b_t = bd * e_inv
    k_t = kd * e_inv
    b_bar = bd * e_bar
    k_bar = kd * e_bar

    rowb = lax.broadcasted_iota(jnp.int32, (n, n), 0)
    colb = lax.broadcasted_iota(jnp.int32, (n, n), 1)
    same_head = (rowb // HEAD_DIM) == (colb // HEAD_DIM)
    rt_, ct_ = rowb % c, colb % c
    strict = (rt_ < ct_) if reverse else (rt_ > ct_)
    incl = (rt_ <= ct_) if reverse else (rt_ >= ct_)

    def stack(z):
        return jnp.where(same_head, jnp.concatenate([z] * HEADS_PER_BLOCK, axis=0), 0.0)

    xa = stack(a_t)
    xa_b = _bf(xa)
    xr_b = _bf(stack(r_t))
    ycat = jnp.concatenate([_bf(stack(b_t)), _bf(stack(k_t))], axis=0)
    vst = _bf(stack(v))
    g1 = _dot_nt(xa_b, ycat)
    g2 = _dot_nt(xr_b, ycat)
    a_m = jnp.where(strict, g1[:, :n], 0.0)
    ak_m = jnp.where(strict, g1[:, n:], 0.0)
    rb_m = _bf(jnp.where(incl, g2[:, :n], 0.0))
    rk_m = _bf(jnp.where(incl, g2[:, n:], 0.0))

    nm = a_m
    q = _dot(_bf(a_m), _bf(a_m))
    levels = CHUNK.bit_length() - 2
    for j in range(levels):
        qb = _bf(q)
        nm = nm + q + _dot(qb, _bf(nm))
        if j + 1 < levels:
            q = _dot(qb, qb)
    nm_b = _bf(nm)

    zst = _dot(_bf(ak_m), vst)
    xz = jnp.concatenate([xa, zst], axis=1)
    txz = xz + _dot(nm_b, _bf(xz))
    ast_b = _bf(txz[:, :n])
    tzst_b = _bf(txz[:, n:])

    def compact(z):
        return z[0:c] + z[c:2 * c] + z[2 * c:3 * c] + z[3 * c:4 * c]

    a_hat = compact(txz[:, :n])
    tz = compact(txz[:, n:])
    r_hat = r_t + compact(_dot(rb_m, ast_b))
    y0 = compact(_dot(rb_m, tzst_b) + _dot(rk_m, vst))

    atz_t = _bf(jnp.transpose(jnp.concatenate([a_hat, tz], axis=0)))
    zeros = jnp.zeros((c, n), F32)
    m_t = jnp.where(same_head, _dot(atz_t, _bf(jnp.concatenate([b_bar, zeros], axis=0))), 0.0)
    k_rows = jnp.concatenate([k_bar, zeros] if v_first else [zeros, k_bar], axis=0)
    n0_t = jnp.where(
        same_head,
        _dot(atz_t, _bf(jnp.concatenate([zeros, b_bar], axis=0))) + _dot(_bf(vt), _bf(k_rows)),
        0.0)

    s_b = _bf(state)
    y = _dot_nt(_bf(r_hat), s_b) + y0
    new_state = state * p_c + _dot(s_b, _bf(m_t)) + n0_t
    return y, new_state


def _wkv_kernel(*refs, nc, has_s0, want_state):
    (rf, vf, af, ldf, kf, bf_, rb, vb, ab, ldb, kb, bb) = refs[:12]
    pos = 12
    s0_ref = None
    if has_s0:
        s0_ref = refs[pos]
        pos += 1
    yf_ref, yb_ref = refs[pos], refs[pos + 1]
    pos += 2
    so_ref = None
    if want_state:
        so_ref = refs[pos]
        pos += 1
    st_ref = refs[pos]
    ci = pl.program_id(2)

    @pl.when(ci == 0)
    def _():
        if has_s0:
            st_ref[...] = s0_ref[0, :, 0]
        else:
            st_ref[...] = jnp.zeros(st_ref.shape, F32)

    v_f = vf[0]
    v_b = vb[0]
    vt = jnp.transpose(jnp.concatenate([v_f, v_b], axis=0))
    y_f, s_f = _wkv_direction(rf[0], v_f, af[0], ldf[0, 0], kf[0, 0], bf_[0, 0], (vt, True),
                              st_ref[0], reverse=False)
    y_b, s_b = _wkv_direction(rb[0], v_b, ab[0], ldb[0, 0], kb[0, 0], bb[0, 0], (vt, False),
                              st_ref[1], reverse=True)
    yf_ref[0] = y_f
    yb_ref[0] = y_b
    st_ref[0] = s_f
    st_ref[1] = s_b

    if want_state:
        @pl.when(ci == nc - 1)
        def _():
            for d, s in enumerate((s_f, s_b)):
                for i in range(HEADS_PER_BLOCK):
                    lo = i * HEAD_DIM
                    so_ref[0, d, i] = s[lo:lo + HEAD_DIM, lo:lo + HEAD_DIM]


def _wkv_call(r, v, kkn, logd, kd, bd, s0_bd, want_state):
    b, t, d = r.shape
    c = CHUNK
    nc = t // c
    nq = d // BLOCK_LANES
    n = BLOCK_LANES
    fwd = pl.BlockSpec((1, c, n), lambda i, q, j: (i, j, q))
    bwd = pl.BlockSpec((1, c, n), lambda i, q, j: (i, nc - 1 - j, q))
    fwd2 = pl.BlockSpec((1, 1, c, n), lambda i, q, j: (0, i, j, q))
    bwd2 = pl.BlockSpec((1, 1, c, n), lambda i, q, j: (1, i, nc - 1 - j, q))
    in_specs = [fwd, fwd, fwd, fwd2, fwd2, fwd2, bwd, bwd, bwd, bwd2, bwd2, bwd2]
    args = [r, v, kkn, logd, kd, bd, r, v, kkn, logd, kd, bd]
    has_s0 = s0_bd is not None
    if has_s0:
        in_specs.append(pl.BlockSpec((1, 2, 1, n, n), lambda i, q, j: (i, 0, q, 0, 0)))
        args.append(s0_bd)
    out_specs = [fwd, bwd]
    out_shape = [jax.ShapeDtypeStruct((b, t, d), F32), jax.ShapeDtypeStruct((b, t, d), F32)]
    if want_state:
        out_specs.append(pl.BlockSpec((1, 2, HEADS_PER_BLOCK, HEAD_DIM, HEAD_DIM),
                                      lambda i, q, j: (i, 0, q, 0, 0)))
        out_shape.append(jax.ShapeDtypeStruct((b, 2, d // HEAD_DIM, HEAD_DIM, HEAD_DIM), F32))
    return pl.pallas_call(
        functools.partial(_wkv_kernel, nc=nc, has_s0=has_s0, want_state=want_state),
        grid=(b, nq, nc),
        in_specs=in_specs,
        out_specs=out_specs,
        out_shape=out_shape,
        scratch_shapes=[pltpu.VMEM((2, n, n), F32)],
        compiler_params=_params(("arbitrary", "arbitrary", "arbitrary")),
        name="wkv_scan",
    )(*args)


def _rwkv_post_kernel(x_ref, yf_ref, yb_ref, bonus_ref, gate_ref, mod_ref, vec_ref, wo_ref,
                      hsel_ref, hselt_ref, o_ref):
    hsel = hsel_ref[...]
    hselt = hselt_ref[...]
    vec = vec_ref[...]
    gn_g, gn_b = vec[0:1], vec[1:2]
    gt = mod_ref[0][2:3]
    y = yf_ref[0] + yb_ref[0]
    inv = 1.0 / HEAD_DIM
    mean = _head_sum(y, hsel, hselt) * inv
    yc = y - mean
    var = _head_sum(yc * yc, hsel, hselt) * inv
    yn = yc * lax.rsqrt(var + GN_EPS) * gn_g + gn_b + bonus_ref[0]
    out = _dot(_bf(yn * gate_ref[0]), wo_ref[...])
    o_ref[0] = x_ref[0] + gt * out


def _rwkv_post_call(x, yf, yb, bonus, gate, mod, per_batch, vec, wo, hsel, hselt):
    b, t, d = x.shape
    tm = min(512, t)
    mod_map = (lambda i, j: (i, 0, 0)) if per_batch else (lambda i, j: (0, 0, 0))
    tile = pl.BlockSpec((1, tm, d), lambda i, j: (i, j, 0))
    return pl.pallas_call(
        _rwkv_post_kernel,
        grid=(b, t // tm),
        in_specs=[tile, tile, tile, tile, tile, pl.BlockSpec((1, 6, d), mod_map),
                  _const_spec(vec.shape), _const_spec(wo.shape),
                  _const_spec(hsel.shape), _const_spec(hselt.shape)],
        out_specs=tile,
        out_shape=jax.ShapeDtypeStruct((b, t, d), F32),
        compiler_params=_params(("arbitrary", "arbitrary")),
        name="rwkv_post",
    )(x, yf, yb, bonus, gate, mod, vec, wo, hsel, hselt)


def _conv_kernel(x_ref, mod_ref, vec_ref, pw1_ref, pw1b_ref, dw_ref, pw2_ref, o_ref,
                 pad_ref, cv_ref, *, tm, seg):
    d = x_ref.shape[-1]
    nlb = d // LANE
    nseg = tm // seg
    mod = mod_ref[0]
    shift, scale, gt = mod[0:1], mod[1:2], mod[2:3]
    vec = vec_ref[...]
    g, dw_b, ln_g, ln_b, pw2_b = vec[0:1], vec[1:2], vec[2:3], vec[3:4], vec[4:5]
    x = x_ref[0]
    h = _rms_mod(x, g, shift, scale)
    u = _dot(_bf(h), pw1_ref[...]) + pw1b_ref[...]
    u = u[:, :d] * jax.nn.sigmoid(u[:, d:])

    zpad = jnp.zeros((nseg, CONV_PAD, LANE), F32)
    for lb in range(nlb):
        pad_ref[:, lb, 0:CONV_PAD, :] = zpad
        pad_ref[:, lb, CONV_PAD + seg:2 * CONV_PAD + seg, :] = zpad
        pad_ref[:, lb, CONV_PAD:CONV_PAD + seg, :] = u[:, lb * LANE:(lb + 1) * LANE].reshape(nseg, seg, LANE)

    base = CONV_PAD - CONV_WIDTH // 2

    def body(idx, carry):
        s = idx // nlb
        lb = idx % nlb
        acc = jnp.zeros((seg, LANE), F32)
        for j in range(CONV_WIDTH):
            acc = acc + pad_ref[s, lb, base + j:base + j + seg, :] * dw_ref[j, lb]
        cv_ref[s, lb] = acc
        return carry

    lax.fori_loop(0, nseg * nlb, body, 0)

    cv = jnp.concatenate([cv_ref[:, lb].reshape(tm, LANE) for lb in range(nlb)], axis=1) + dw_b
    mu = jnp.mean(cv, axis=-1, keepdims=True)
    cc = cv - mu
    var = jnp.mean(cc * cc, axis=-1, keepdims=True)
    z = cc * lax.rsqrt(var + LN_EPS) * ln_g + ln_b
    z = z * jax.nn.sigmoid(z)
    out = _dot(_bf(z), pw2_ref[...]) + pw2_b
    o_ref[0] = x + gt * out


def _conv_call(x, mod, per_batch, seg, vec, pw1, pw1b, dw, pw2):
    b, t, d = x.shape
    tm = min(512, t)
    seg = min(seg, tm)
    nlb = d // LANE
    mod_map = (lambda i, j: (i, 0, 0)) if per_batch else (lambda i, j: (0, 0, 0))
    tile = pl.BlockSpec((1, tm, d), lambda i, j: (i, j, 0))
    return pl.pallas_call(
        functools.partial(_conv_kernel, tm=tm, seg=seg),
        grid=(b, t // tm),
        in_specs=[tile, pl.BlockSpec((1, 6, d), mod_map), _const_spec(vec.shape),
                  _const_spec(pw1.shape), _const_spec(pw1b.shape), _const_spec(dw.shape),
                  _const_spec(pw2.shape)],
        out_specs=tile,
        out_shape=jax.ShapeDtypeStruct((b, t, d), F32),
        scratch_shapes=[pltpu.VMEM((tm // seg, nlb, seg + 2 * CONV_PAD, LANE), F32),
                        pltpu.VMEM((tm // seg, nlb, seg, LANE), F32)],
        compiler_params=_params(("arbitrary", "arbitrary")),
        name="conv_module",
    )(x, mod, vec, pw1, pw1b, dw, pw2)


def _router_gates_t(h, rwt_ref, rb_ref):
    h_hi, h_lo = _split(h)
    w_hi, w_lo = _split(rwt_ref[...])
    logits = _dot_nt(w_hi, h_hi) + _dot_nt(w_hi, h_lo) + _dot_nt(w_lo, h_hi) + rb_ref[...]
    m = jnp.max(logits, axis=0, keepdims=True)
    ex = jnp.exp(logits - m)
    p = ex / jnp.sum(ex, axis=0, keepdims=True)
    rows = [p[e:e + 1] for e in range(N_EXPERT_GROUPS * EXPERTS_PER_GROUP)]
    scores = []
    for gi in range(N_EXPERT_GROUPS):
        a, b, c, dd = rows[4 * gi:4 * gi + 4]
        hi1, lo1 = jnp.maximum(a, b), jnp.minimum(a, b)
        hi2, lo2 = jnp.maximum(c, dd), jnp.minimum(c, dd)
        scores.append(jnp.maximum(hi1, hi2) + jnp.maximum(jnp.minimum(hi1, hi2), jnp.maximum(lo1, lo2)))
    best = scores[0]
    sel = jnp.zeros(best.shape, jnp.int32)
    for gi in range(1, N_EXPERT_GROUPS):
        better = scores[gi] > best
        sel = jnp.where(better, gi, sel)
        best = jnp.where(better, scores[gi], best)
    qs = []
    for j in range(EXPERTS_PER_GROUP):
        qj = jnp.zeros(best.shape, F32)
        for gi in range(N_EXPERT_GROUPS):
            qj = jnp.where(sel == gi, rows[4 * gi + j], qj)
        qs.append(qj)

    def argmax4(vals):
        bv, bi = vals[0], jnp.zeros(best.shape, jnp.int32)
        for j in range(1, EXPERTS_PER_GROUP):
            better = vals[j] > bv
            bi = jnp.where(better, j, bi)
            bv = jnp.where(better, vals[j], bv)
        return bv, bi

    v1, i1 = argmax4(qs)
    v2, i2 = argmax4([jnp.where(i1 == j, -1.0, qs[j]) for j in range(EXPERTS_PER_GROUP)])
    den = v1 + v2
    w1, w2 = v1 / den, v2 / den
    out = []
    for gi in range(N_EXPERT_GROUPS):
        for j in range(EXPERTS_PER_GROUP):
            ge = jnp.where(i1 == j, w1, 0.0) + jnp.where(i2 == j, w2, 0.0)
            out.append(jnp.where(sel == gi, ge, 0.0))
    return jnp.concatenate(out, axis=0)


def _moe_kernel(x_ref, mod_ref, vec_ref, rwt_ref, rb_ref, wg_ref, wu_ref, wd_ref, o_ref,
                h_ref, gates_ref, acc_ref, *, n_exp, final):
    e = pl.program_id(2)
    tm = x_ref.shape[1]

    @pl.when(e == 0)
    def _():
        mod = mod_ref[0]
        h = _rms_mod(x_ref[0], vec_ref[0:1], mod[3:4], mod[4:5])
        h_ref[...] = _bf(h)
        gt = _router_gates_t(h, rwt_ref, rb_ref)
        gt = jnp.concatenate([gt, jnp.zeros((LANE - n_exp, tm), F32)], axis=0)
        gates_ref[...] = jnp.transpose(gt)
        acc_ref[...] = jnp.zeros(acc_ref.shape, F32)

    hb = h_ref[...]
    hg = _dot(hb, wg_ref[0])
    hu = _dot(hb, wu_ref[0])
    lane = lax.broadcasted_iota(jnp.int32, (tm, LANE), 1)
    gcol = jnp.sum(jnp.where(lane == e, gates_ref[...], 0.0), axis=1, keepdims=True)
    he = hg * jax.nn.sigmoid(hg) * hu * gcol
    acc_ref[...] += _dot(_bf(he), wd_ref[0])

    @pl.when(e == n_exp - 1)
    def _():
        out = x_ref[0] + mod_ref[0][5:6] * acc_ref[...]
        if final:
            out = out * lax.rsqrt(jnp.mean(out * out, axis=-1, keepdims=True) + RMS_EPS) * vec_ref[1:2]
        o_ref[0] = out


def _moe_call(x, mod, per_batch, vec, rwt, rb, wg, wu, wd, final):
    b, t, d = x.shape
    tm = min(1024, t)
    n_exp, _, de = wg.shape
    mod_map = (lambda i, j, e: (i, 0, 0)) if per_batch else (lambda i, j, e: (0, 0, 0))
    tile = pl.BlockSpec((1, tm, d), lambda i, j, e: (i, j, 0))
    cst = lambda shape: pl.BlockSpec(shape, lambda i, j, e: (0,) * len(shape))
    return pl.pallas_call(
        functools.partial(_moe_kernel, n_exp=n_exp, final=final),
        grid=(b, t // tm, n_exp),
        in_specs=[tile, pl.BlockSpec((1, 6, d), mod_map), cst(vec.shape), cst(rwt.shape), cst(rb.shape),
                  pl.BlockSpec((1, d, de), lambda i, j, e: (e, 0, 0)),
                  pl.BlockSpec((1, d, de), lambda i, j, e: (e, 0, 0)),
                  pl.BlockSpec((1, de, d), lambda i, j, e: (e, 0, 0))],
        out_specs=tile,
        out_shape=jax.ShapeDtypeStruct((b, t, d), F32),
        scratch_shapes=[pltpu.VMEM((tm, d), BF16), pltpu.VMEM((tm, LANE), F32), pltpu.VMEM((tm, d), F32)],
        compiler_params=_params(("arbitrary", "arbitrary", "arbitrary")),
        name="grouped_moe",
    )(x, mod, vec, rwt, rb, wg, wu, wd)


def _pad_rows(a, rows):
    return jnp.concatenate([a, jnp.zeros((rows - a.shape[0],) + a.shape[1:], a.dtype)], axis=0)


def _block_diag_state(s0):
    b, two, h, n, _ = s0.shape
    nq = h // HEADS_PER_BLOCK
    s = s0.reshape(b, two, nq, HEADS_PER_BLOCK, n, n)
    eye = jnp.eye(HEADS_PER_BLOCK, dtype=s0.dtype)
    bd = s[:, :, :, :, :, None, :] * eye[None, None, None, :, None, :, None]
    return bd.reshape(b, two, nq, HEADS_PER_BLOCK * n, HEADS_PER_BLOCK * n)


def kernel(x_prompt, x_sample, state_rwkv, c, c_ctx, norm_g, ada_w, ada_b, final_g, rwkv_mu, rwkv_w_rkv, rwkv_w_o, rwkv_w0, rwkv_w1, rwkv_w2, rwkv_a0, rwkv_a1, rwkv_a2, rwkv_g1, rwkv_g2, rwkv_k_k, rwkv_k_a, rwkv_r_k, rwkv_gn_g, rwkv_gn_b, conv_pw1, conv_pw1_b, conv_dw, conv_dw_b, conv_ln_g, conv_ln_b, conv_pw2, conv_pw2_b, router_w, router_b, moe_w_gate, moe_w_up, moe_w_down):
    d = x_prompt.shape[-1]
    depth = ada_w.shape[0]
    n_heads = d // HEAD_DIM
    dec_b = c.shape[0]

    cond = _pad_rows(jnp.concatenate([c_ctx[None, :], c], axis=0), COND_ROWS)
    mod = _mod_call(cond, ada_w, ada_b).reshape(depth, COND_ROWS, 6, d)

    head_of_lane = jnp.arange(d) // HEAD_DIM
    hsel = (head_of_lane[:, None] == jnp.arange(LANE)[None, :]).astype(BF16)
    hselt = hsel.T
    rwt = router_w.T
    rb = router_b[:, None]
    wg = _bf(moe_w_gate)
    wu = _bf(moe_w_up)
    wd = _bf(moe_w_down)

    def lora_pad(w2):
        z = jnp.zeros_like(w2[0])
        return _bf(jnp.stack([jnp.concatenate([w2[0], z], 0), jnp.concatenate([z, w2[1]], 0)]))

    groups = (
        dict(x=x_prompt, rows=slice(0, 1), per_batch=False, s0=None, want_state=True, seg=x_prompt.shape[1]),
        dict(x=x_sample, rows=slice(1, 1 + dec_b), per_batch=True, s0=state_rwkv, want_state=False, seg=GRID_W),
    )
    outs = []
    new_state = None
    for gr in groups:
        x = gr["x"]
        states = []
        for i in range(depth):
            m_i = mod[i, gr["rows"]]
            j = i // 2
            if i % 2 == 0:
                vec = jnp.stack([norm_g[i, 0], rwkv_k_k[j], rwkv_k_a[j], rwkv_r_k[j].reshape(d),
                                 rwkv_w0[j, 0], rwkv_w0[j, 1], rwkv_a0[j, 0], rwkv_a0[j, 1]])
                r, v, kkn, logd, kd, bd, gate, bonus = _rwkv_pre_call(
                    x, m_i, gr["per_batch"], vec, rwkv_mu[j], _bf(rwkv_w_rkv[j]), _bf(rwkv_g1[j]), _bf(rwkv_g2[j]),
                    _bf(jnp.concatenate([rwkv_w1[j, 0], rwkv_w1[j, 1]], axis=1)), lora_pad(rwkv_w2[j]),
                    _bf(jnp.concatenate([rwkv_a1[j, 0], rwkv_a1[j, 1]], axis=1)), lora_pad(rwkv_a2[j]),
                    hsel, hselt)
                s0_bd = None if gr["s0"] is None else _block_diag_state(gr["s0"][:, j])
                res = _wkv_call(r, v, kkn, logd, kd, bd, s0_bd, gr["want_state"])
                if gr["want_state"]:
                    states.append(res[2])
                vec = _pad_rows(jnp.stack([rwkv_gn_g[j], rwkv_gn_b[j]]), 8)
                x = _rwkv_post_call(x, res[0], res[1], bonus, gate, m_i, gr["per_batch"], vec,
                                    _bf(rwkv_w_o[j]), hsel, hselt)
            else:
                vec = _pad_rows(jnp.stack([norm_g[i, 0], conv_dw_b[j], conv_ln_g[j], conv_ln_b[j], conv_pw2_b[j]]), 8)
                dw = conv_dw[j].reshape(CONV_WIDTH, d // LANE, 1, LANE)
                x = _conv_call(x, m_i, gr["per_batch"], gr["seg"], vec, _bf(conv_pw1[j]),
                               conv_pw1_b[j][None, :], dw, _bf(conv_pw2[j]))
            vec = _pad_rows(jnp.stack([norm_g[i, 1], final_g]), 8)
            xm = x if gr["per_batch"] else x.reshape(1, -1, d)
            xm = _moe_call(xm, m_i, gr["per_batch"], vec, rwt, rb, wg[i], wu[i], wd[i], final=(i == depth - 1))
            x = xm.reshape(x.shape)
        outs.append(x)
        if gr["want_state"]:
            new_state = jnp.stack(states, axis=1)
    return (outs[0], outs[1], new_state)
```

```python
import functools

import jax
import jax.numpy as jnp
from jax import lax
from jax.experimental import pallas as pl
from jax.experimental.pallas import tpu as pltpu

F32 = jnp.float32
BF16 = jnp.bfloat16

HEAD_DIM = 64
HEADS_PER_BLOCK = 4
BLOCK_LANES = HEAD_DIM * HEADS_PER_BLOCK
CHUNK = 64
WKV_BLOCKS_PER_STEP = 4
GRID_W = 64
CONV_WIDTH = 31
CONV_PAD = 16
N_EXPERT_GROUPS = 4
EXPERTS_PER_GROUP = 4
LANE = 128
COND_ROWS = 16
RMS_EPS = 1e-6
LN_EPS = 1e-5
GN_EPS = 64e-5
VMEM_LIMIT_BYTES = 56 * 1024 * 1024


def _bf(x):
    return x.astype(BF16)


def _dot(a, b):
    return jnp.dot(a, b, preferred_element_type=F32)


def _dot_nt(a, b):
    return lax.dot_general(a, b, (((1,), (1,)), ((), ())), preferred_element_type=F32)


def _split(x):
    hi = _bf(x)
    lo = _bf(x - hi.astype(F32))
    return hi, lo


def _dot_split(a, b_exact):
    hi, lo = _split(a)
    return _dot(hi, b_exact) + _dot(lo, b_exact)


def _rms_mod(x, g, shift, scale):
    y = x * lax.rsqrt(jnp.mean(x * x, axis=-1, keepdims=True) + RMS_EPS) * g
    return y * (1.0 + scale) + shift


def _head_sum(z, hsel, hselt):
    return _dot_split(_dot_split(z, hsel), hselt)


def _params(sem):
    return pltpu.CompilerParams(dimension_semantics=sem, vmem_limit_bytes=VMEM_LIMIT_BYTES)


def _const_spec(shape):
    nd = len(shape)
    return pl.BlockSpec(shape, lambda *_: (0,) * nd, pipeline_mode=pl.Buffered(1))


def _mod_kernel(c_ref, w_ref, b_ref, o_ref):
    c = c_ref[...]
    s = c * jax.nn.sigmoid(c)
    o_ref[0] = jnp.dot(s, w_ref[0], preferred_element_type=F32,
                       precision=lax.Precision.HIGHEST) + b_ref[0]


def _mod_call(cond, ada_w, ada_b):
    depth, d, n = ada_w.shape
    tn = 1536
    return pl.pallas_call(
        _mod_kernel,
        grid=(depth, n // tn),
        in_specs=[
            pl.BlockSpec((COND_ROWS, d), lambda l, j: (0, 0)),
            pl.BlockSpec((1, d, tn), lambda l, j: (l, 0, j)),
            pl.BlockSpec((1, 1, tn), lambda l, j: (l, 0, j)),
        ],
        out_specs=pl.BlockSpec((1, COND_ROWS, tn), lambda l, j: (l, 0, j)),
        out_shape=jax.ShapeDtypeStruct((depth, COND_ROWS, n), F32),
        compiler_params=_params(("arbitrary", "arbitrary")),
        name="adaln_mod",
    )(cond, ada_w, ada_b.reshape(depth, 1, n))


def _rwkv_pre_kernel(x_ref, xp_ref, xn_ref, mod_ref, vec_ref, mu_ref, wrkv_ref, g1_ref, g2_ref,
                     w1_ref, w2_ref, a1_ref, a2_ref, hsel_ref, hselt_ref,
                     r_ref, v_ref, kkn_ref, logd_ref, kd_ref, bd_ref, gate_ref, bonus_ref,
                     *, tm, nt):
    t = pl.program_id(1)
    mod = mod_ref[0]
    shift, scale = mod[0:1], mod[1:2]
    vec = vec_ref[...]
    g, k_k, k_a, r_k = vec[0:1], vec[1:2], vec[2:3], vec[3:4]
    hsel = hsel_ref[...]
    hselt = hselt_ref[...]

    h = _rms_mod(x_ref[0], g, shift, scale)
    h_prev = _rms_mod(xp_ref[0], g, shift, scale)[7:8]
    h_next = _rms_mod(xn_ref[0], g, shift, scale)[0:1]
    h_prev = jnp.where(t == 0, 0.0, h_prev)
    h_next = jnp.where(t == nt - 1, 0.0, h_next)
    row = lax.broadcasted_iota(jnp.int32, (tm, 1), 0)
    prev = jnp.where(row == 0, h_prev, pltpu.roll(h, 1, 0))
    nxt = jnp.where(row == tm - 1, h_next, pltpu.roll(h, tm - 1, 0))
    xx = 0.5 * (prev + nxt) - h
    mu = mu_ref[...]

    def mix(i):
        return _bf(h + xx * mu[i:i + 1])

    r = _dot(mix(0), wrkv_ref[0])
    k = _dot(mix(2), wrkv_ref[1])
    v = _dot(mix(3), wrkv_ref[2])
    gate = _dot(_bf(jax.nn.sigmoid(_dot(mix(5), g1_ref[...]))), g2_ref[...])
    tw = _bf(jnp.tanh(_dot(mix(1), w1_ref[...])))
    ta = _bf(_dot(mix(4), a1_ref[...]))

    kk = k * k_k
    kkn = kk * lax.rsqrt(jnp.maximum(_head_sum(kk * kk, hsel, hselt), 1e-24))
    r_ref[0] = r
    v_ref[0] = v
    kkn_ref[0] = kkn
    gate_ref[0] = gate

    ksum = None
    for d in range(2):
        u = vec[4 + d:5 + d] + _dot(tw, w2_ref[d])
        w_log = jnp.minimum(u, 0.0) - jnp.log(1.0 + jnp.exp(-jnp.abs(u))) - 0.5
        logd_ref[d, 0] = -jnp.exp(w_log)
        a = jax.nn.sigmoid(vec[6 + d:7 + d] + _dot(ta, a2_ref[d]))
        kd = k * (1.0 + (a - 1.0) * k_a)
        kd_ref[d, 0] = kd
        bd_ref[d, 0] = kkn * a
        ksum = kd if ksum is None else ksum + kd
    bonus_ref[0] = _head_sum(r * r_k * ksum, hsel, hselt) * v


def _rwkv_pre_call(x, mod, per_batch, vec, mu, wrkv, g1, g2, w1c, w2p, a1c, a2p, hsel, hselt):
    b, t, d = x.shape
    tm = min(256, t)
    nt = t // tm
    r8 = tm // 8
    nb8 = t // 8
    mod_map = (lambda i, j: (i, 0, 0)) if per_batch else (lambda i, j: (0, 0, 0))
    tile = pl.BlockSpec((1, tm, d), lambda i, j: (i, j, 0))
    tile2 = pl.BlockSpec((2, 1, tm, d), lambda i, j: (0, i, j, 0))
    one = jax.ShapeDtypeStruct((b, t, d), F32)
    two = jax.ShapeDtypeStruct((2, b, t, d), F32)
    return pl.pallas_call(
        functools.partial(_rwkv_pre_kernel, tm=tm, nt=nt),
        grid=(b, nt),
        in_specs=[
            tile,
            pl.BlockSpec((1, 8, d), lambda i, j: (i, jnp.maximum(j * r8 - 1, 0), 0)),
            pl.BlockSpec((1, 8, d), lambda i, j: (i, jnp.minimum((j + 1) * r8, nb8 - 1), 0)),
            pl.BlockSpec((1, 6, d), mod_map),
            _const_spec(vec.shape), _const_spec(mu.shape), _const_spec(wrkv.shape),
            _const_spec(g1.shape), _const_spec(g2.shape), _const_spec(w1c.shape),
            _const_spec(w2p.shape), _const_spec(a1c.shape), _const_spec(a2p.shape),
            _const_spec(hsel.shape), _const_spec(hselt.shape),
        ],
        out_specs=[tile, tile, tile, tile2, tile2, tile2, tile, tile],
        out_shape=[one, one, one, two, two, two, one, one],
        compiler_params=_params(("arbitrary", "arbitrary")),
        name="rwkv_pre",
    )(x, x, x, mod, vec, mu, wrkv, g1, g2, w1c, w2p, a1c, a2p, hsel, hselt)


def _wkv_chains(chains, bd_mask):
    c = CHUNK
    n = BLOCK_LANES
    idx = range(len(chains))
    rev = [ch["reverse"] for ch in chains]
    ri = lax.broadcasted_iota(jnp.int32, (c, c), 0)
    ci = lax.broadcasted_iota(jnp.int32, (c, c), 1)
    lmat = {False: _bf(jnp.where(ri >= ci, 1.0, 0.0)), True: _bf(jnp.where(ri <= ci, 1.0, 0.0))}
    row_t = lax.broadcasted_iota(jnp.int32, (c, n), 0)
    col_s = lax.broadcasted_iota(jnp.int32, (c, n), 1) % c
    strict = {False: row_t > col_s, True: row_t < col_s}
    incl = {False: row_t >= col_s, True: row_t <= col_s}
    zeros = jnp.zeros((c, n), F32)

    def blockdiag(z):
        return jnp.concatenate([_bf(z)] * HEADS_PER_BLOCK, axis=0) * bd_mask

    logd = [ch["logd"] for ch in chains]
    split = [_split(x) for x in logd]
    cum = [_dot(lmat[rev[i]], split[i][0]) + _dot(lmat[rev[i]], split[i][1]) for i in idx]
    tot = [cum[i][0:1] if rev[i] else cum[i][c - 1:c] for i in idx]
    e_in = [jnp.exp(cum[i]) for i in idx]
    e_ex = [jnp.exp(cum[i] - logd[i]) for i in idx]
    e_inv = [jnp.exp(-cum[i]) for i in idx]
    e_bar = [jnp.exp(tot[i] - cum[i]) for i in idx]
    p_c = [jnp.exp(tot[i]) for i in idx]

    a_t = [-chains[i]["kkn"] * e_ex[i] for i in idx]
    r_t = [chains[i]["r"] * e_in[i] for i in idx]
    b_t = [chains[i]["bd"] * e_inv[i] for i in idx]
    k_t = [chains[i]["kd"] * e_inv[i] for i in idx]
    b_bar = [chains[i]["bd"] * e_bar[i] for i in idx]
    k_bar = [chains[i]["kd"] * e_bar[i] for i in idx]

    ycat = [jnp.concatenate([blockdiag(b_t[i]), blockdiag(k_t[i])], axis=0) for i in idx]
    g = [_dot_nt(_bf(jnp.concatenate([a_t[i], r_t[i]], axis=0)), ycat[i]) for i in idx]
    a_m = [jnp.where(strict[rev[i]], g[i][:c, :n], 0.0) for i in idx]
    ak_b = [_bf(jnp.where(strict[rev[i]], g[i][:c, n:], 0.0)) for i in idx]
    rb_b = [_bf(jnp.where(incl[rev[i]], g[i][c:, :n], 0.0)) for i in idx]
    rk_b = [_bf(jnp.where(incl[rev[i]], g[i][c:, n:], 0.0)) for i in idx]

    vst = [blockdiag(chains[i]["v"]) for i in idx]
    z = [_dot(ak_b[i], vst[i]) for i in idx]

    nm = a_m
    q = [_dot(_bf(a_m[i]), blockdiag(a_m[i])) for i in idx]
    levels = CHUNK.bit_length() - 2
    for j in range(levels):
        qb = [_bf(q[i]) for i in idx]
        nm = [nm[i] + q[i] + _dot(qb[i], blockdiag(nm[i])) for i in idx]
        if j + 1 < levels:
            q = [_dot(qb[i], blockdiag(q[i])) for i in idx]

    t_az = [jnp.concatenate([a_t[i], z[i]], axis=1)
            + _dot(_bf(nm[i]), jnp.concatenate([blockdiag(a_t[i]), blockdiag(z[i])], axis=1)) for i in idx]
    a_hat = [t_az[i][:, :n] for i in idx]
    tz = [t_az[i][:, n:] for i in idx]
    r_hat = [r_t[i] + _dot(rb_b[i], blockdiag(a_hat[i])) for i in idx]
    y0 = [_dot(rb_b[i], blockdiag(tz[i])) + _dot(rk_b[i], vst[i]) for i in idx]

    atz_t = [_bf(jnp.transpose(jnp.concatenate([a_hat[i], tz[i]], axis=0))) for i in idx]
    m_t = [_bf(_dot(atz_t[i], _bf(jnp.concatenate([b_bar[i], zeros], axis=0)))) * bd_mask for i in idx]
    k_rows = [jnp.concatenate([k_bar[i], zeros] if chains[i]["v_first"] else [zeros, k_bar[i]], axis=0) for i in idx]
    bd_mask_f = bd_mask.astype(F32)
    n0_bd = [(_dot(atz_t[i], _bf(jnp.concatenate([zeros, b_bar[i]], axis=0)))
              + _dot(_bf(chains[i]["vt"]), _bf(k_rows[i]))) * bd_mask_f for i in idx]
    n0 = [x[0:c] + x[c:2 * c] + x[2 * c:3 * c] + x[3 * c:4 * c] for x in n0_bd]

    st = [chains[i]["state"] for i in idx]
    y = [_dot_nt(_bf(r_hat[i]), blockdiag(st[i])) + y0[i] for i in idx]
    new_state = [st[i] * p_c[i] + _dot(_bf(st[i]), m_t[i]) + n0[i] for i in idx]
    return list(zip(y, new_state))


def _wkv_kernel(*refs, nc, has_s0, want_state):
    (rf, vf, af, ldf, kf, bf_, rb, vb, ab, ldb, kb, bb) = refs[:12]
    pos = 12
    s0_ref = None
    if has_s0:
        s0_ref = refs[pos]
        pos += 1
    yf_ref, yb_ref = refs[pos], refs[pos + 1]
    pos += 2
    so_ref = None
    if want_state:
        so_ref = refs[pos]
        pos += 1
    st_ref = refs[pos]
    ci = pl.program_id(2)
    n = BLOCK_LANES

    @pl.when(ci == 0)
    def _():
        if has_s0:
            st_ref[...] = s0_ref[0]
        else:
            st_ref[...] = jnp.zeros(st_ref.shape, F32)

    rowb = lax.broadcasted_iota(jnp.int32, (n, n), 0) // HEAD_DIM
    colb = lax.broadcasted_iota(jnp.int32, (n, n), 1) // HEAD_DIM
    bd_mask = _bf(jnp.where(rowb == colb, 1.0, 0.0))

    chains = []
    for qi in range(WKV_BLOCKS_PER_STEP):
        ln = slice(qi * n, (qi + 1) * n)
        v_f = vf[0, :, ln]
        v_b = vb[0, :, ln]
        vt = jnp.transpose(jnp.concatenate([v_f, v_b], axis=0))
        chains.append(dict(r=rf[0, :, ln], v=v_f, kkn=af[0, :, ln], logd=ldf[0, 0, :, ln], kd=kf[0, 0, :, ln],
                           bd=bf_[0, 0, :, ln], vt=vt, v_first=True, state=st_ref[0, :, ln], reverse=False))
        chains.append(dict(r=rb[0, :, ln], v=v_b, kkn=ab[0, :, ln], logd=ldb[0, 0, :, ln], kd=kb[0, 0, :, ln],
                           bd=bb[0, 0, :, ln], vt=vt, v_first=False, state=st_ref[1, :, ln], reverse=True))
    res = _wkv_chains(chains, bd_mask)
    finals = []
    for qi in range(WKV_BLOCKS_PER_STEP):
        ln = slice(qi * n, (qi + 1) * n)
        (y_f, s_f), (y_b, s_b) = res[2 * qi], res[2 * qi + 1]
        yf_ref[0, :, ln] = y_f
        yb_ref[0, :, ln] = y_b
        st_ref[0, :, ln] = s_f
        st_ref[1, :, ln] = s_b
        finals.append((s_f, s_b))

    if want_state:
        @pl.when(ci == nc - 1)
        def _():
            for qi, pair in enumerate(finals):
                for d, s in enumerate(pair):
                    for i in range(HEADS_PER_BLOCK):
                        so_ref[0, d, qi * HEADS_PER_BLOCK + i] = s[:, i * HEAD_DIM:(i + 1) * HEAD_DIM]


def _wkv_call(r, v, kkn, logd, kd, bd, s0_bd, want_state):
    b, t, d = r.shape
    c = CHUNK
    nc = t // c
    n = BLOCK_LANES * WKV_BLOCKS_PER_STEP
    nq = d // n
    fwd = pl.BlockSpec((1, c, n), lambda i, q, j: (i, j, q))
    bwd = pl.BlockSpec((1, c, n), lambda i, q, j: (i, nc - 1 - j, q))
    fwd2 = pl.BlockSpec((1, 1, c, n), lambda i, q, j: (0, i, j, q))
    bwd2 = pl.BlockSpec((1, 1, c, n), lambda i, q, j: (1, i, nc - 1 - j, q))
    in_specs = [fwd, fwd, fwd, fwd2, fwd2, fwd2, bwd, bwd, bwd, bwd2, bwd2, bwd2]
    args = [r, v, kkn, logd, kd, bd, r, v, kkn, logd, kd, bd]
    has_s0 = s0_bd is not None
    if has_s0:
        in_specs.append(pl.BlockSpec((1, 2, HEAD_DIM, n), lambda i, q, j: (i, 0, 0, q)))
        args.append(s0_bd)
    out_specs = [fwd, bwd]
    out_shape = [jax.ShapeDtypeStruct((b, t, d), F32), jax.ShapeDtypeStruct((b, t, d), F32)]
    if want_state:
        out_specs.append(pl.BlockSpec((1, 2, n // HEAD_DIM, HEAD_DIM, HEAD_DIM),
                                      lambda i, q, j: (i, 0, q, 0, 0)))
        out_shape.append(jax.ShapeDtypeStruct((b, 2, d // HEAD_DIM, HEAD_DIM, HEAD_DIM), F32))
    return pl.pallas_call(
        functools.partial(_wkv_kernel, nc=nc, has_s0=has_s0, want_state=want_state),
        grid=(b, nq, nc),
        in_specs=in_specs,
        out_specs=out_specs,
        out_shape=out_shape,
        scratch_shapes=[pltpu.VMEM((2, HEAD_DIM, n), F32)],
        compiler_params=_params(("arbitrary", "arbitrary", "arbitrary")),
        name="wkv_scan",
    )(*args)


def _rwkv_post_kernel(x_ref, yf_ref, yb_ref, bonus_ref, gate_ref, mod_ref, vec_ref, wo_ref,
                      hsel_ref, hselt_ref, o_ref):
    hsel = hsel_ref[...]
    hselt = hselt_ref[...]
    vec = vec_ref[...]
    gn_g, gn_b = vec[0:1], vec[1:2]
    gt = mod_ref[0][2:3]
    y = yf_ref[0] + yb_ref[0]
    inv = 1.0 / HEAD_DIM
    mean = _head_sum(y, hsel, hselt) * inv
    yc = y - mean
    var = _head_sum(yc * yc, hsel, hselt) * inv
    yn = yc * lax.rsqrt(var + GN_EPS) * gn_g + gn_b + bonus_ref[0]
    out = _dot(_bf(yn * gate_ref[0]), wo_ref[...])
    o_ref[0] = x_ref[0] + gt * out


def _rwkv_post_call(x, yf, yb, bonus, gate, mod, per_batch, vec, wo, hsel, hselt):
    b, t, d = x.shape
    tm = min(512, t)
    mod_map = (lambda i, j: (i, 0, 0)) if per_batch else (lambda i, j: (0, 0, 0))
    tile = pl.BlockSpec((1, tm, d), lambda i, j: (i, j, 0))
    return pl.pallas_call(
        _rwkv_post_kernel,
        grid=(b, t // tm),
        in_specs=[tile, tile, tile, tile, tile, pl.BlockSpec((1, 6, d), mod_map),
                  _const_spec(vec.shape), _const_spec(wo.shape),
                  _const_spec(hsel.shape), _const_spec(hselt.shape)],
        out_specs=tile,
        out_shape=jax.ShapeDtypeStruct((b, t, d), F32),
        compiler_params=_params(("arbitrary", "arbitrary")),
        name="rwkv_post",
    )(x, yf, yb, bonus, gate, mod, vec, wo, hsel, hselt)


def _conv_kernel(x_ref, mod_ref, vec_ref, pw1_ref, pw1b_ref, dw_ref, pw2_ref, o_ref,
                 pad_ref, cv_ref, *, tm, seg):
    d = x_ref.shape[-1]
    nlb = d // LANE
    nseg = tm // seg
    mod = mod_ref[0]
    shift, scale, gt = mod[0:1], mod[1:2], mod[2:3]
    vec = vec_ref[...]
    g, dw_b, ln_g, ln_b, pw2_b = vec[0:1], vec[1:2], vec[2:3], vec[3:4], vec[4:5]
    x = x_ref[0]
    h = _rms_mod(x, g, shift, scale)
    u = _dot(_bf(h), pw1_ref[...]) + pw1b_ref[...]
    u = u[:, :d] * jax.nn.sigmoid(u[:, d:])

    zpad = jnp.zeros((nseg, CONV_PAD, LANE), F32)
    for lb in range(nlb):
        pad_ref[:, lb, 0:CONV_PAD, :] = zpad
        pad_ref[:, lb, CONV_PAD + seg:2 * CONV_PAD + seg, :] = zpad
        pad_ref[:, lb, CONV_PAD:CONV_PAD + seg, :] = u[:, lb * LANE:(lb + 1) * LANE].reshape(nseg, seg, LANE)

    base = CONV_PAD - CONV_WIDTH // 2

    def body(idx, carry):
        s = idx // nlb
        lb = idx % nlb
        acc = jnp.zeros((seg, LANE), F32)
        for j in range(CONV_WIDTH):
            acc = acc + pad_ref[s, lb, base + j:base + j + seg, :] * dw_ref[j, lb]
        cv_ref[s, lb] = acc
        return carry

    lax.fori_loop(0, nseg * nlb, body, 0)

    cv = jnp.concatenate([cv_ref[:, lb].reshape(tm, LANE) for lb in range(nlb)], axis=1) + dw_b
    mu = jnp.mean(cv, axis=-1, keepdims=True)
    cc = cv - mu
    var = jnp.mean(cc * cc, axis=-1, keepdims=True)
    z = cc * lax.rsqrt(var + LN_EPS) * ln_g + ln_b
    z = z * jax.nn.sigmoid(z)
    out = _dot(_bf(z), pw2_ref[...]) + pw2_b
    o_ref[0] = x + gt * out


def _conv_call(x, mod, per_batch, seg, vec, pw1, pw1b, dw, pw2):
    b, t, d = x.shape
    tm = min(512, t)
    seg = min(seg, tm)
    nlb = d // LANE
    mod_map = (lambda i, j: (i, 0, 0)) if per_batch else (lambda i, j: (0, 0, 0))
    tile = pl.BlockSpec((1, tm, d), lambda i, j: (i, j, 0))
    return pl.pallas_call(
        functools.partial(_conv_kernel, tm=tm, seg=seg),
        grid=(b, t // tm),
        in_specs=[tile, pl.BlockSpec((1, 6, d), mod_map), _const_spec(vec.shape),
                  _const_spec(pw1.shape), _const_spec(pw1b.shape), _const_spec(dw.shape),
                  _const_spec(pw2.shape)],
        out_specs=tile,
        out_shape=jax.ShapeDtypeStruct((b, t, d), F32),
        scratch_shapes=[pltpu.VMEM((tm // seg, nlb, seg + 2 * CONV_PAD, LANE), F32),
                        pltpu.VMEM((tm // seg, nlb, seg, LANE), F32)],
        compiler_params=_params(("arbitrary", "arbitrary")),
        name="conv_module",
    )(x, mod, vec, pw1, pw1b, dw, pw2)


def _router_gates_t(h, rwt_ref, rb_ref):
    h_hi, h_lo = _split(h)
    w_hi, w_lo = _split(rwt_ref[...])
    logits = _dot_nt(w_hi, h_hi) + _dot_nt(w_hi, h_lo) + _dot_nt(w_lo, h_hi) + rb_ref[...]
    m = jnp.max(logits, axis=0, keepdims=True)
    ex = jnp.exp(logits - m)
    p = ex / jnp.sum(ex, axis=0, keepdims=True)
    rows = [p[e:e + 1] for e in range(N_EXPERT_GROUPS * EXPERTS_PER_GROUP)]
    scores = []
    for gi in range(N_EXPERT_GROUPS):
        a, b, c, dd = rows[4 * gi:4 * gi + 4]
        hi1, lo1 = jnp.maximum(a, b), jnp.minimum(a, b)
        hi2, lo2 = jnp.maximum(c, dd), jnp.minimum(c, dd)
        scores.append(jnp.maximum(hi1, hi2) + jnp.maximum(jnp.minimum(hi1, hi2), jnp.maximum(lo1, lo2)))
    best = scores[0]
    sel = jnp.zeros(best.shape, jnp.int32)
    for gi in range(1, N_EXPERT_GROUPS):
        better = scores[gi] > best
        sel = jnp.where(better, gi, sel)
        best = jnp.where(better, scores[gi], best)
    qs = []
    for j in range(EXPERTS_PER_GROUP):
        qj = jnp.zeros(best.shape, F32)
        for gi in range(N_EXPERT_GROUPS):
            qj = jnp.where(sel == gi, rows[4 * gi + j], qj)
        qs.append(qj)

    def argmax4(vals):
        bv, bi = vals[0], jnp.zeros(best.shape, jnp.int32)
        for j in range(1, EXPERTS_PER_GROUP):
            better = vals[j] > bv
            bi = jnp.where(better, j, bi)
            bv = jnp.where(better, vals[j], bv)
        return bv, bi

    v1, i1 = argmax4(qs)
    v2, i2 = argmax4([jnp.where(i1 == j, -1.0, qs[j]) for j in range(EXPERTS_PER_GROUP)])
    den = v1 + v2
    w1, w2 = v1 / den, v2 / den
    out = []
    for gi in range(N_EXPERT_GROUPS):
        for j in range(EXPERTS_PER_GROUP):
            ge = jnp.where(i1 == j, w1, 0.0) + jnp.where(i2 == j, w2, 0.0)
            out.append(jnp.where(sel == gi, ge, 0.0))
    return jnp.concatenate(out, axis=0)


def _moe_kernel(x_ref, mod_ref, vec_ref, rwt_ref, rb_ref, wg_ref, wu_ref, wd_ref, o_ref,
                h_ref, gates_ref, acc_ref, *, n_exp, final):
    e = pl.program_id(2)
    tm = x_ref.shape[1]

    @pl.when(e == 0)
    def _():
        mod = mod_ref[0]
        h = _rms_mod(x_ref[0], vec_ref[0:1], mod[3:4], mod[4:5])
        h_ref[...] = _bf(h)
        gt = _router_gates_t(h, rwt_ref, rb_ref)
        gt = jnp.concatenate([gt, jnp.zeros((LANE - n_exp, tm), F32)], axis=0)
        gates_ref[...] = jnp.transpose(gt)
        acc_ref[...] = jnp.zeros(acc_ref.shape, F32)

    hb = h_ref[...]
    hg = _dot(hb, wg_ref[0])
    hu = _dot(hb, wu_ref[0])
    lane = lax.broadcasted_iota(jnp.int32, (tm, LANE), 1)
    gcol = jnp.sum(jnp.where(lane == e, gates_ref[...], 0.0), axis=1, keepdims=True)
    he = hg * jax.nn.sigmoid(hg) * hu * gcol
    acc_ref[...] += _dot(_bf(he), wd_ref[0])

    @pl.when(e == n_exp - 1)
    def _():
        out = x_ref[0] + mod_ref[0][5:6] * acc_ref[...]
        if final:
            out = out * lax.rsqrt(jnp.mean(out * out, axis=-1, keepdims=True) + RMS_EPS) * vec_ref[1:2]
        o_ref[0] = out


def _moe_call(x, mod, per_batch, vec, rwt, rb, wg, wu, wd, final):
    b, t, d = x.shape
    tm = min(1024, t)
    n_exp, _, de = wg.shape
    mod_map = (lambda i, j, e: (i, 0, 0)) if per_batch else (lambda i, j, e: (0, 0, 0))
    tile = pl.BlockSpec((1, tm, d), lambda i, j, e: (i, j, 0))
    cst = lambda shape: pl.BlockSpec(shape, lambda i, j, e: (0,) * len(shape))
    return pl.pallas_call(
        functools.partial(_moe_kernel, n_exp=n_exp, final=final),
        grid=(b, t // tm, n_exp),
        in_specs=[tile, pl.BlockSpec((1, 6, d), mod_map), cst(vec.shape), cst(rwt.shape), cst(rb.shape),
                  pl.BlockSpec((1, d, de), lambda i, j, e: (e, 0, 0)),
                  pl.BlockSpec((1, d, de), lambda i, j, e: (e, 0, 0)),
                  pl.BlockSpec((1, de, d), lambda i, j, e: (e, 0, 0))],
        out_specs=tile,
        out_shape=jax.ShapeDtypeStruct((b, t, d), F32),
        scratch_shapes=[pltpu.VMEM((tm, d), BF16), pltpu.VMEM((tm, LANE), F32), pltpu.VMEM((tm, d), F32)],
        compiler_params=_params(("arbitrary", "arbitrary", "arbitrary")),
        name="grouped_moe",
    )(x, mod, vec, rwt, rb, wg, wu, wd)


def _pad_rows(a, rows):
    return jnp.concatenate([a, jnp.zeros((rows - a.shape[0],) + a.shape[1:], a.dtype)], axis=0)


def _lane_stacked_state(s0):
    b, two, h, n, _ = s0.shape
    return jnp.swapaxes(s0, 2, 3).reshape(b, two, n, h * n)


def kernel(x_prompt, x_sample, state_rwkv, c, c_ctx, norm_g, ada_w, ada_b, final_g, rwkv_mu, rwkv_w_rkv, rwkv_w_o, rwkv_w0, rwkv_w1, rwkv_w2, rwkv_a0, rwkv_a1, rwkv_a2, rwkv_g1, rwkv_g2, rwkv_k_k, rwkv_k_a, rwkv_r_k, rwkv_gn_g, rwkv_gn_b, conv_pw1, conv_pw1_b, conv_dw, conv_dw_b, conv_ln_g, conv_ln_b, conv_pw2, conv_pw2_b, router_w, router_b, moe_w_gate, moe_w_up, moe_w_down):
    d = x_prompt.shape[-1]
    depth = ada_w.shape[0]
    n_heads = d // HEAD_DIM
    dec_b = c.shape[0]

    cond = _pad_rows(jnp.concatenate([c_ctx[None, :], c], axis=0), COND_ROWS)
    mod = _mod_call(cond, ada_w, ada_b).reshape(depth, COND_ROWS, 6, d)

    head_of_lane = jnp.arange(d) // HEAD_DIM
    hsel = (head_of_lane[:, None] == jnp.arange(LANE)[None, :]).astype(BF16)
    hselt = hsel.T
    rwt = router_w.T
    rb = router_b[:, None]
    wg = _bf(moe_w_gate)
    wu = _bf(moe_w_up)
    wd = _bf(moe_w_down)

    def lora_pad(w2):
        z = jnp.zeros_like(w2[0])
        return _bf(jnp.stack([jnp.concatenate([w2[0], z], 0), jnp.concatenate([z, w2[1]], 0)]))

    groups = (
        dict(x=x_prompt, rows=slice(0, 1), per_batch=False, s0=None, want_state=True, seg=x_prompt.shape[1]),
        dict(x=x_sample, rows=slice(1, 1 + dec_b), per_batch=True, s0=state_rwkv, want_state=False, seg=GRID_W),
    )
    outs = []
    new_state = None
    for gr in groups:
        x = gr["x"]
        states = []
        for i in range(depth):
            m_i = mod[i, gr["rows"]]
            j = i // 2
            if i % 2 == 0:
                vec = jnp.stack([norm_g[i, 0], rwkv_k_k[j], rwkv_k_a[j], rwkv_r_k[j].reshape(d),
                                 rwkv_w0[j, 0], rwkv_w0[j, 1], rwkv_a0[j, 0], rwkv_a0[j, 1]])
                r, v, kkn, logd, kd, bd, gate, bonus = _rwkv_pre_call(
                    x, m_i, gr["per_batch"], vec, rwkv_mu[j], _bf(rwkv_w_rkv[j]), _bf(rwkv_g1[j]), _bf(rwkv_g2[j]),
                    _bf(jnp.concatenate([rwkv_w1[j, 0], rwkv_w1[j, 1]], axis=1)), lora_pad(rwkv_w2[j]),
                    _bf(jnp.concatenate([rwkv_a1[j, 0], rwkv_a1[j, 1]], axis=1)), lora_pad(rwkv_a2[j]),
                    hsel, hselt)
                s0_bd = None if gr["s0"] is None else _lane_stacked_state(gr["s0"][:, j])
                res = _wkv_call(r, v, kkn, logd, kd, bd, s0_bd, gr["want_state"])
                if gr["want_state"]:
                    states.append(res[2])
                vec = _pad_rows(jnp.stack([rwkv_gn_g[j], rwkv_gn_b[j]]), 8)
                x = _rwkv_post_call(x, res[0], res[1], bonus, gate, m_i, gr["per_batch"], vec,
                                    _bf(rwkv_w_o[j]), hsel, hselt)
            else:
                vec = _pad_rows(jnp.stack([norm_g[i, 0], conv_dw_b[j], conv_ln_g[j], conv_ln_b[j], conv_pw2_b[j]]), 8)
                dw = conv_dw[j].reshape(CONV_WIDTH, d // LANE, 1, LANE)
                x = _conv_call(x, m_i, gr["per_batch"], gr["seg"], vec, _bf(conv_pw1[j]),
                               conv_pw1_b[j][None, :], dw, _bf(conv_pw2[j]))
            vec = _pad_rows(jnp.stack([norm_g[i, 1], final_g]), 8)
            xm = x if gr["per_batch"] else x.reshape(1, -1, d)
            xm = _moe_call(xm, m_i, gr["per_batch"], vec, rwt, rb, wg[i], wu[i], wd[i], final=(i == depth - 1))
            x = xm.reshape(x.shape)
        outs.append(x)
        if gr["want_state"]:
            new_state = jnp.stack(states, axis=1)
    return (outs[0], outs[1], new_state)
```

```python
import functools

import jax
import jax.numpy as jnp
from jax import lax
from jax.experimental import pallas as pl
from jax.experimental.pallas import tpu as pltpu

F32 = jnp.float32
BF16 = jnp.bfloat16

HEAD_DIM = 64
HEADS_PER_BLOCK = 4
BLOCK_LANES = HEAD_DIM * HEADS_PER_BLOCK
CHUNK = 64
WKV_BLOCKS_PER_STEP = 4
GRID_W = 64
MOE_ROWS = 256
CONV_WIDTH = 31
CONV_PAD = 16
N_EXPERT_GROUPS = 4
EXPERTS_PER_GROUP = 4
LANE = 128
COND_ROWS = 16
RMS_EPS = 1e-6
LN_EPS = 1e-5
GN_EPS = 64e-5
VMEM_LIMIT_BYTES = 56 * 1024 * 1024


def _bf(x):
    return x.astype(BF16)


def _dot(a, b):
    return jnp.dot(a, b, preferred_element_type=F32)


def _dot_nt(a, b):
    return lax.dot_general(a, b, (((1,), (1,)), ((), ())), preferred_element_type=F32)


def _split(x):
    hi = _bf(x)
    lo = _bf(x - hi.astype(F32))
    return hi, lo


def _dot_split(a, b_exact):
    hi, lo = _split(a)
    return _dot(hi, b_exact) + _dot(lo, b_exact)


def _rms_mod(x, g, shift, scale):
    y = x * lax.rsqrt(jnp.mean(x * x, axis=-1, keepdims=True) + RMS_EPS) * g
    return y * (1.0 + scale) + shift


def _head_sum(z, hsel, hselt):
    return _dot_split(_dot_split(z, hsel), hselt)


def _params(sem):
    return pltpu.CompilerParams(dimension_semantics=sem, vmem_limit_bytes=VMEM_LIMIT_BYTES)


def _const_spec(shape):
    nd = len(shape)
    return pl.BlockSpec(shape, lambda *_: (0,) * nd, pipeline_mode=pl.Buffered(1))


def _mod_kernel(c_ref, w_ref, b_ref, o_ref):
    c = c_ref[...]
    s = c * jax.nn.sigmoid(c)
    o_ref[0] = jnp.dot(s, w_ref[0], preferred_element_type=F32,
                       precision=lax.Precision.HIGHEST) + b_ref[0]


def _mod_call(cond, ada_w, ada_b):
    depth, d, n = ada_w.shape
    tn = 1536
    return pl.pallas_call(
        _mod_kernel,
        grid=(depth, n // tn),
        in_specs=[
            pl.BlockSpec((COND_ROWS, d), lambda l, j: (0, 0)),
            pl.BlockSpec((1, d, tn), lambda l, j: (l, 0, j)),
            pl.BlockSpec((1, 1, tn), lambda l, j: (l, 0, j)),
        ],
        out_specs=pl.BlockSpec((1, COND_ROWS, tn), lambda l, j: (l, 0, j)),
        out_shape=jax.ShapeDtypeStruct((depth, COND_ROWS, n), F32),
        compiler_params=_params(("arbitrary", "arbitrary")),
        name="adaln_mod",
    )(cond, ada_w, ada_b.reshape(depth, 1, n))


def _rwkv_pre_kernel(x_ref, xp_ref, xn_ref, mod_ref, vec_ref, mu_ref, wrkv_ref, g1_ref, g2_ref,
                     w1_ref, w2_ref, a1_ref, a2_ref, hsel_ref, hselt_ref,
                     r_ref, v_ref, kkn_ref, logd_ref, kd_ref, bd_ref, gate_ref, bonus_ref,
                     *, tm, nt):
    t = pl.program_id(1)
    mod = mod_ref[0]
    shift, scale = mod[0:1], mod[1:2]
    vec = vec_ref[...]
    g, k_k, k_a, r_k = vec[0:1], vec[1:2], vec[2:3], vec[3:4]
    hsel = hsel_ref[...]
    hselt = hselt_ref[...]

    h = _rms_mod(x_ref[0], g, shift, scale)
    h_prev = _rms_mod(xp_ref[0], g, shift, scale)[7:8]
    h_next = _rms_mod(xn_ref[0], g, shift, scale)[0:1]
    h_prev = jnp.where(t == 0, 0.0, h_prev)
    h_next = jnp.where(t == nt - 1, 0.0, h_next)
    row = lax.broadcasted_iota(jnp.int32, (tm, 1), 0)
    prev = jnp.where(row == 0, h_prev, pltpu.roll(h, 1, 0))
    nxt = jnp.where(row == tm - 1, h_next, pltpu.roll(h, tm - 1, 0))
    xx = 0.5 * (prev + nxt) - h
    mu = mu_ref[...]

    def mix(i):
        return _bf(h + xx * mu[i:i + 1])

    r = _dot(mix(0), wrkv_ref[0])
    k = _dot(mix(2), wrkv_ref[1])
    v = _dot(mix(3), wrkv_ref[2])
    gate = _dot(_bf(jax.nn.sigmoid(_dot(mix(5), g1_ref[...]))), g2_ref[...])
    tw = _bf(jnp.tanh(_dot(mix(1), w1_ref[...])))
    ta = _bf(_dot(mix(4), a1_ref[...]))

    kk = k * k_k
    kkn = kk * lax.rsqrt(jnp.maximum(_head_sum(kk * kk, hsel, hselt), 1e-24))
    r_ref[0] = _bf(r)
    v_ref[0] = _bf(v)
    kkn_ref[0] = _bf(kkn)
    gate_ref[0] = _bf(gate)

    ksum = None
    for d in range(2):
        u = vec[4 + d:5 + d] + _dot(tw, w2_ref[d])
        w_log = jnp.minimum(u, 0.0) - jnp.log(1.0 + jnp.exp(-jnp.abs(u))) - 0.5
        logd_ref[d, 0] = -jnp.exp(w_log)
        a = jax.nn.sigmoid(vec[6 + d:7 + d] + _dot(ta, a2_ref[d]))
        kd = k * (1.0 + (a - 1.0) * k_a)
        kd_ref[d, 0] = _bf(kd)
        bd_ref[d, 0] = _bf(kkn * a)
        ksum = kd if ksum is None else ksum + kd
    bonus_ref[0] = _bf(_head_sum(r * r_k * ksum, hsel, hselt) * v)


def _rwkv_pre_call(x, mod, per_batch, vec, mu, wrkv, g1, g2, w1c, w2p, a1c, a2p, hsel, hselt):
    b, t, d = x.shape
    tm = min(256, t)
    nt = t // tm
    r8 = tm // 8
    nb8 = t // 8
    mod_map = (lambda i, j: (i, 0, 0)) if per_batch else (lambda i, j: (0, 0, 0))
    tile = pl.BlockSpec((1, tm, d), lambda i, j: (i, j, 0))
    tile2 = pl.BlockSpec((2, 1, tm, d), lambda i, j: (0, i, j, 0))
    one = jax.ShapeDtypeStruct((b, t, d), BF16)
    two = jax.ShapeDtypeStruct((2, b, t, d), BF16)
    two_f32 = jax.ShapeDtypeStruct((2, b, t, d), F32)
    return pl.pallas_call(
        functools.partial(_rwkv_pre_kernel, tm=tm, nt=nt),
        grid=(b, nt),
        in_specs=[
            tile,
            pl.BlockSpec((1, 8, d), lambda i, j: (i, jnp.maximum(j * r8 - 1, 0), 0)),
            pl.BlockSpec((1, 8, d), lambda i, j: (i, jnp.minimum((j + 1) * r8, nb8 - 1), 0)),
            pl.BlockSpec((1, 6, d), mod_map),
            _const_spec(vec.shape), _const_spec(mu.shape), _const_spec(wrkv.shape),
            _const_spec(g1.shape), _const_spec(g2.shape), _const_spec(w1c.shape),
            _const_spec(w2p.shape), _const_spec(a1c.shape), _const_spec(a2p.shape),
            _const_spec(hsel.shape), _const_spec(hselt.shape),
        ],
        out_specs=[tile, tile, tile, tile2, tile2, tile2, tile, tile],
        out_shape=[one, one, one, two_f32, two, two, one, one],
        compiler_params=_params(("arbitrary", "arbitrary")),
        name="rwkv_pre",
    )(x, x, x, mod, vec, mu, wrkv, g1, g2, w1c, w2p, a1c, a2p, hsel, hselt)


def _wkv_chains(chains, bd_mask):
    c = CHUNK
    n = BLOCK_LANES
    idx = range(len(chains))
    rev = [ch["reverse"] for ch in chains]
    ri = lax.broadcasted_iota(jnp.int32, (c, c), 0)
    ci = lax.broadcasted_iota(jnp.int32, (c, c), 1)
    lmat = {False: _bf(jnp.where(ri >= ci, 1.0, 0.0)), True: _bf(jnp.where(ri <= ci, 1.0, 0.0))}
    row_t = lax.broadcasted_iota(jnp.int32, (c, n), 0)
    col_s = lax.broadcasted_iota(jnp.int32, (c, n), 1) % c
    strict = {False: row_t > col_s, True: row_t < col_s}
    incl = {False: row_t >= col_s, True: row_t <= col_s}
    zeros = jnp.zeros((c, n), F32)

    def blockdiag(z):
        return jnp.concatenate([_bf(z)] * HEADS_PER_BLOCK, axis=0) * bd_mask

    logd = [ch["logd"] for ch in chains]
    split = [_split(x) for x in logd]
    cum = [_dot(lmat[rev[i]], split[i][0]) + _dot(lmat[rev[i]], split[i][1]) for i in idx]
    tot = [cum[i][0:1] if rev[i] else cum[i][c - 1:c] for i in idx]
    e_in = [jnp.exp(cum[i]) for i in idx]
    e_ex = [jnp.exp(cum[i] - logd[i]) for i in idx]
    e_inv = [jnp.exp(-cum[i]) for i in idx]
    e_bar = [jnp.exp(tot[i] - cum[i]) for i in idx]
    p_c = [jnp.exp(tot[i]) for i in idx]

    a_t = [-chains[i]["kkn"] * e_ex[i] for i in idx]
    r_t = [chains[i]["r"] * e_in[i] for i in idx]
    b_t = [chains[i]["bd"] * e_inv[i] for i in idx]
    k_t = [chains[i]["kd"] * e_inv[i] for i in idx]
    b_bar = [chains[i]["bd"] * e_bar[i] for i in idx]
    k_bar = [chains[i]["kd"] * e_bar[i] for i in idx]

    ycat = [jnp.concatenate([blockdiag(b_t[i]), blockdiag(k_t[i])], axis=0) for i in idx]
    g = [_dot_nt(_bf(jnp.concatenate([a_t[i], r_t[i]], axis=0)), ycat[i]) for i in idx]
    a_m = [jnp.where(strict[rev[i]], g[i][:c, :n], 0.0) for i in idx]
    ak_b = [_bf(jnp.where(strict[rev[i]], g[i][:c, n:], 0.0)) for i in idx]
    rb_b = [_bf(jnp.where(incl[rev[i]], g[i][c:, :n], 0.0)) for i in idx]
    rk_b = [_bf(jnp.where(incl[rev[i]], g[i][c:, n:], 0.0)) for i in idx]

    vst = [blockdiag(chains[i]["v"]) for i in idx]
    z = [_dot(ak_b[i], vst[i]) for i in idx]

    nm = a_m
    q = [_dot(_bf(a_m[i]), blockdiag(a_m[i])) for i in idx]
    levels = CHUNK.bit_length() - 2
    for j in range(levels):
        qb = [_bf(q[i]) for i in idx]
        nm = [nm[i] + q[i] + _dot(qb[i], blockdiag(nm[i])) for i in idx]
        if j + 1 < levels:
            q = [_dot(qb[i], blockdiag(q[i])) for i in idx]

    t_az = [jnp.concatenate([a_t[i], z[i]], axis=1)
            + _dot(_bf(nm[i]), jnp.concatenate([blockdiag(a_t[i]), blockdiag(z[i])], axis=1)) for i in idx]
    a_hat = [t_az[i][:, :n] for i in idx]
    tz = [t_az[i][:, n:] for i in idx]
    r_hat = [r_t[i] + _dot(rb_b[i], blockdiag(a_hat[i])) for i in idx]
    y0 = [_dot(rb_b[i], blockdiag(tz[i])) + _dot(rk_b[i], vst[i]) for i in idx]

    atz_t = [_bf(jnp.transpose(jnp.concatenate([a_hat[i], tz[i]], axis=0))) for i in idx]
    m_t = [_bf(_dot(atz_t[i], _bf(jnp.concatenate([b_bar[i], zeros], axis=0)))) * bd_mask for i in idx]
    k_rows = [jnp.concatenate([k_bar[i], zeros] if chains[i]["v_first"] else [zeros, k_bar[i]], axis=0) for i in idx]
    bd_mask_f = bd_mask.astype(F32)
    n0_bd = [(_dot(atz_t[i], _bf(jnp.concatenate([zeros, b_bar[i]], axis=0)))
              + _dot(_bf(chains[i]["vt"]), _bf(k_rows[i]))) * bd_mask_f for i in idx]
    n0 = [x[0:c] + x[c:2 * c] + x[2 * c:3 * c] + x[3 * c:4 * c] for x in n0_bd]

    st = [chains[i]["state"] for i in idx]
    y = [_dot_nt(_bf(r_hat[i]), blockdiag(st[i])) + y0[i] for i in idx]
    new_state = [st[i] * p_c[i] + _dot(_bf(st[i]), m_t[i]) + n0[i] for i in idx]
    return list(zip(y, new_state))


def _wkv_kernel(*refs, nc, has_s0, want_state):
    (rf, vf, af, ldf, kf, bf_, rb, vb, ab, ldb, kb, bb) = refs[:12]
    pos = 12
    s0_ref = None
    if has_s0:
        s0_ref = refs[pos]
        pos += 1
    yf_ref, yb_ref = refs[pos], refs[pos + 1]
    pos += 2
    so_ref = None
    if want_state:
        so_ref = refs[pos]
        pos += 1
    st_ref = refs[pos]
    ci = pl.program_id(2)
    n = BLOCK_LANES

    @pl.when(ci == 0)
    def _():
        if has_s0:
            st_ref[...] = s0_ref[0]
        else:
            st_ref[...] = jnp.zeros(st_ref.shape, F32)

    rowb = lax.broadcasted_iota(jnp.int32, (n, n), 0) // HEAD_DIM
    colb = lax.broadcasted_iota(jnp.int32, (n, n), 1) // HEAD_DIM
    bd_mask = _bf(jnp.where(rowb == colb, 1.0, 0.0))

    chains = []
    for qi in range(WKV_BLOCKS_PER_STEP):
        ln = slice(qi * n, (qi + 1) * n)
        f32 = lambda ref, *ix: ref[ix + (slice(None), ln)].astype(F32)
        v_f = f32(vf, 0)
        v_b = f32(vb, 0)
        vt = jnp.transpose(jnp.concatenate([v_f, v_b], axis=0))
        chains.append(dict(r=f32(rf, 0), v=v_f, kkn=f32(af, 0), logd=ldf[0, 0, :, ln], kd=f32(kf, 0, 0),
                           bd=f32(bf_, 0, 0), vt=vt, v_first=True, state=st_ref[0, :, ln], reverse=False))
        chains.append(dict(r=f32(rb, 0), v=v_b, kkn=f32(ab, 0), logd=ldb[0, 0, :, ln], kd=f32(kb, 0, 0),
                           bd=f32(bb, 0, 0), vt=vt, v_first=False, state=st_ref[1, :, ln], reverse=True))
    res = _wkv_chains(chains, bd_mask)
    finals = []
    for qi in range(WKV_BLOCKS_PER_STEP):
        ln = slice(qi * n, (qi + 1) * n)
        (y_f, s_f), (y_b, s_b) = res[2 * qi], res[2 * qi + 1]
        yf_ref[0, :, ln] = y_f
        yb_ref[0, :, ln] = y_b
        st_ref[0, :, ln] = s_f
        st_ref[1, :, ln] = s_b
        finals.append((s_f, s_b))

    if want_state:
        @pl.when(ci == nc - 1)
        def _():
            for qi, pair in enumerate(finals):
                for d, s in enumerate(pair):
                    for i in range(HEADS_PER_BLOCK):
                        so_ref[0, d, qi * HEADS_PER_BLOCK + i] = s[:, i * HEAD_DIM:(i + 1) * HEAD_DIM]


def _wkv_call(r, v, kkn, logd, kd, bd, s0_bd, want_state):
    b, t, d = r.shape
    c = CHUNK
    nc = t // c
    n = BLOCK_LANES * WKV_BLOCKS_PER_STEP
    nq = d // n
    fwd = pl.BlockSpec((1, c, n), lambda i, q, j: (i, j, q))
    bwd = pl.BlockSpec((1, c, n), lambda i, q, j: (i, nc - 1 - j, q))
    fwd2 = pl.BlockSpec((1, 1, c, n), lambda i, q, j: (0, i, j, q))
    bwd2 = pl.BlockSpec((1, 1, c, n), lambda i, q, j: (1, i, nc - 1 - j, q))
    in_specs = [fwd, fwd, fwd, fwd2, fwd2, fwd2, bwd, bwd, bwd, bwd2, bwd2, bwd2]
    args = [r, v, kkn, logd, kd, bd, r, v, kkn, logd, kd, bd]
    has_s0 = s0_bd is not None
    if has_s0:
        in_specs.append(pl.BlockSpec((1, 2, HEAD_DIM, n), lambda i, q, j: (i, 0, 0, q)))
        args.append(s0_bd)
    out_specs = [fwd, bwd]
    out_shape = [jax.ShapeDtypeStruct((b, t, d), F32), jax.ShapeDtypeStruct((b, t, d), F32)]
    if want_state:
        out_specs.append(pl.BlockSpec((1, 2, n // HEAD_DIM, HEAD_DIM, HEAD_DIM),
                                      lambda i, q, j: (i, 0, q, 0, 0)))
        out_shape.append(jax.ShapeDtypeStruct((b, 2, d // HEAD_DIM, HEAD_DIM, HEAD_DIM), F32))
    return pl.pallas_call(
        functools.partial(_wkv_kernel, nc=nc, has_s0=has_s0, want_state=want_state),
        grid=(b, nq, nc),
        in_specs=in_specs,
        out_specs=out_specs,
        out_shape=out_shape,
        scratch_shapes=[pltpu.VMEM((2, HEAD_DIM, n), F32)],
        compiler_params=_params(("arbitrary", "arbitrary", "arbitrary")),
        name="wkv_scan",
    )(*args)


def _rwkv_post_kernel(x_ref, yf_ref, yb_ref, bonus_ref, gate_ref, mod_ref, vec_ref, wo_ref,
                      hsel_ref, hselt_ref, o_ref):
    hsel = hsel_ref[...]
    hselt = hselt_ref[...]
    vec = vec_ref[...]
    gn_g, gn_b = vec[0:1], vec[1:2]
    gt = mod_ref[0][2:3]
    y = yf_ref[0] + yb_ref[0]
    inv = 1.0 / HEAD_DIM
    mean = _head_sum(y, hsel, hselt) * inv
    yc = y - mean
    var = _head_sum(yc * yc, hsel, hselt) * inv
    yn = yc * lax.rsqrt(var + GN_EPS) * gn_g + gn_b + bonus_ref[0].astype(F32)
    out = _dot(_bf(yn * gate_ref[0].astype(F32)), wo_ref[...])
    o_ref[0] = x_ref[0] + gt * out


def _rwkv_post_call(x, yf, yb, bonus, gate, mod, per_batch, vec, wo, hsel, hselt):
    b, t, d = x.shape
    tm = min(512, t)
    mod_map = (lambda i, j: (i, 0, 0)) if per_batch else (lambda i, j: (0, 0, 0))
    tile = pl.BlockSpec((1, tm, d), lambda i, j: (i, j, 0))
    return pl.pallas_call(
        _rwkv_post_kernel,
        grid=(b, t // tm),
        in_specs=[tile, tile, tile, tile, tile, pl.BlockSpec((1, 6, d), mod_map),
                  _const_spec(vec.shape), _const_spec(wo.shape),
                  _const_spec(hsel.shape), _const_spec(hselt.shape)],
        out_specs=tile,
        out_shape=jax.ShapeDtypeStruct((b, t, d), F32),
        compiler_params=_params(("arbitrary", "arbitrary")),
        name="rwkv_post",
    )(x, yf, yb, bonus, gate, mod, vec, wo, hsel, hselt)


def _conv_kernel(x_ref, mod_ref, vec_ref, pw1_ref, pw1b_ref, dw_ref, pw2_ref, o_ref,
                 pad_ref, cv_ref, *, tm, seg):
    d = x_ref.shape[-1]
    nlb = d // LANE
    nseg = tm // seg
    mod = mod_ref[0]
    shift, scale, gt = mod[0:1], mod[1:2], mod[2:3]
    vec = vec_ref[...]
    g, dw_b, ln_g, ln_b, pw2_b = vec[0:1], vec[1:2], vec[2:3], vec[3:4], vec[4:5]
    x = x_ref[0]
    h = _rms_mod(x, g, shift, scale)
    u = _dot(_bf(h), pw1_ref[...]) + pw1b_ref[...]
    u = u[:, :d] * jax.nn.sigmoid(u[:, d:])

    zpad = jnp.zeros((nseg, CONV_PAD, LANE), F32)
    for lb in range(nlb):
        pad_ref[:, lb, 0:CONV_PAD, :] = zpad
        pad_ref[:, lb, CONV_PAD + seg:2 * CONV_PAD + seg, :] = zpad
        pad_ref[:, lb, CONV_PAD:CONV_PAD + seg, :] = u[:, lb * LANE:(lb + 1) * LANE].reshape(nseg, seg, LANE)

    base = CONV_PAD - CONV_WIDTH // 2

    def body(idx, carry):
        s = idx // nlb
        lb = idx % nlb
        acc = jnp.zeros((seg, LANE), F32)
        for j in range(CONV_WIDTH):
            acc = acc + pad_ref[s, lb, base + j:base + j + seg, :] * dw_ref[j, lb]
        cv_ref[s, lb] = acc
        return carry

    lax.fori_loop(0, nseg * nlb, body, 0)

    cv = jnp.concatenate([cv_ref[:, lb].reshape(tm, LANE) for lb in range(nlb)], axis=1) + dw_b
    mu = jnp.mean(cv, axis=-1, keepdims=True)
    cc = cv - mu
    var = jnp.mean(cc * cc, axis=-1, keepdims=True)
    z = cc * lax.rsqrt(var + LN_EPS) * ln_g + ln_b
    z = z * jax.nn.sigmoid(z)
    out = _dot(_bf(z), pw2_ref[...]) + pw2_b
    o_ref[0] = x + gt * out


def _conv_call(x, mod, per_batch, seg, vec, pw1, pw1b, dw, pw2):
    b, t, d = x.shape
    tm = min(512, t)
    seg = min(seg, tm)
    nlb = d // LANE
    mod_map = (lambda i, j: (i, 0, 0)) if per_batch else (lambda i, j: (0, 0, 0))
    tile = pl.BlockSpec((1, tm, d), lambda i, j: (i, j, 0))
    return pl.pallas_call(
        functools.partial(_conv_kernel, tm=tm, seg=seg),
        grid=(b, t // tm),
        in_specs=[tile, pl.BlockSpec((1, 6, d), mod_map), _const_spec(vec.shape),
                  _const_spec(pw1.shape), _const_spec(pw1b.shape), _const_spec(dw.shape),
                  _const_spec(pw2.shape)],
        out_specs=tile,
        out_shape=jax.ShapeDtypeStruct((b, t, d), F32),
        scratch_shapes=[pltpu.VMEM((tm // seg, nlb, seg + 2 * CONV_PAD, LANE), F32),
                        pltpu.VMEM((tm // seg, nlb, seg, LANE), F32)],
        compiler_params=_params(("arbitrary", "arbitrary")),
        name="conv_module",
    )(x, mod, vec, pw1, pw1b, dw, pw2)


def _router_gates_t(h, rwt_ref, rb_ref):
    h_hi, h_lo = _split(h)
    w_hi, w_lo = _split(rwt_ref[...])
    logits = _dot_nt(w_hi, h_hi) + _dot_nt(w_hi, h_lo) + _dot_nt(w_lo, h_hi) + rb_ref[...]
    m = jnp.max(logits, axis=0, keepdims=True)
    ex = jnp.exp(logits - m)
    p = ex / jnp.sum(ex, axis=0, keepdims=True)
    rows = [p[e:e + 1] for e in range(N_EXPERT_GROUPS * EXPERTS_PER_GROUP)]
    scores = []
    for gi in range(N_EXPERT_GROUPS):
        a, b, c, dd = rows[4 * gi:4 * gi + 4]
        hi1, lo1 = jnp.maximum(a, b), jnp.minimum(a, b)
        hi2, lo2 = jnp.maximum(c, dd), jnp.minimum(c, dd)
        scores.append(jnp.maximum(hi1, hi2) + jnp.maximum(jnp.minimum(hi1, hi2), jnp.maximum(lo1, lo2)))
    best = scores[0]
    sel = jnp.zeros(best.shape, jnp.int32)
    for gi in range(1, N_EXPERT_GROUPS):
        better = scores[gi] > best
        sel = jnp.where(better, gi, sel)
        best = jnp.where(better, scores[gi], best)
    qs = []
    for j in range(EXPERTS_PER_GROUP):
        qj = jnp.zeros(best.shape, F32)
        for gi in range(N_EXPERT_GROUPS):
            qj = jnp.where(sel == gi, rows[4 * gi + j], qj)
        qs.append(qj)

    def argmax4(vals):
        bv, bi = vals[0], jnp.zeros(best.shape, jnp.int32)
        for j in range(1, EXPERTS_PER_GROUP):
            better = vals[j] > bv
            bi = jnp.where(better, j, bi)
            bv = jnp.where(better, vals[j], bv)
        return bv, bi

    v1, i1 = argmax4(qs)
    v2, i2 = argmax4([jnp.where(i1 == j, -1.0, qs[j]) for j in range(EXPERTS_PER_GROUP)])
    den = v1 + v2
    w1, w2 = v1 / den, v2 / den
    gates = []
    for gi in range(N_EXPERT_GROUPS):
        rows_g = [jnp.where(sel == gi, jnp.where(i1 == j, w1, 0.0) + jnp.where(i2 == j, w2, 0.0), 0.0)
                  for j in range(EXPERTS_PER_GROUP)]
        gates.append(jnp.concatenate(rows_g, axis=0))
    return gates, sel


def _moe_kernel(x_ref, mod_ref, vec_ref, rwt_ref, rb_ref, wg_ref, wu_ref, wd_ref, o_ref,
                h_ref, gt_ref, rk_ref, rc_ref, p_ref, pt_ref, xg_ref, yg_ref, wr_ref, acc_ref, nblk_ref,
                *, n_exp, final):
    e = pl.program_id(2)
    tm = x_ref.shape[1]
    nb = tm // MOE_ROWS
    g = e // EXPERTS_PER_GROUP
    j = e % EXPERTS_PER_GROUP
    sub8 = 8

    @pl.when(e == 0)
    def _():
        mod = mod_ref[0]
        h = _rms_mod(x_ref[0], vec_ref[0:1], mod[3:4], mod[4:5])
        h_ref[...] = _bf(h)
        gates, sel = _router_gates_t(h, rwt_ref, rb_ref)
        zero4 = jnp.zeros((sub8 - EXPERTS_PER_GROUP, tm), F32)
        for gi in range(N_EXPERT_GROUPS):
            gt_ref[gi] = jnp.concatenate([gates[gi], zero4], axis=0)
        chosen = [jnp.where(sel == gi, 1.0, 0.0) for gi in range(N_EXPERT_GROUPS)]
        selmat = _bf(jnp.concatenate(chosen + [jnp.zeros((sub8 - N_EXPERT_GROUPS, tm), F32)], axis=0))
        earlier = _bf(jnp.where(lax.broadcasted_iota(jnp.int32, (tm, tm), 0)
                                < lax.broadcasted_iota(jnp.int32, (tm, tm), 1), 1.0, 0.0))
        prefix = _dot(selmat, earlier)
        rk = jnp.where(selmat > 0, prefix, -1.0)
        rk_ref[...] = rk
        rc_ref[...] = jnp.transpose(jnp.concatenate([rk, jnp.zeros((LANE - sub8, tm), F32)], axis=0))
        acc_ref[...] = jnp.zeros(acc_ref.shape, F32)

    @pl.when(j == 0)
    def _():
        sub_g = lax.broadcasted_iota(jnp.int32, (sub8, tm), 0)
        rank_row = jnp.sum(jnp.where(sub_g == g, rk_ref[...], 0.0), axis=0, keepdims=True)
        lane_g = lax.broadcasted_iota(jnp.int32, (tm, LANE), 1)
        rank_col = jnp.sum(jnp.where(lane_g == g, rc_ref[...], 0.0), axis=1, keepdims=True)
        slot = lax.broadcasted_iota(jnp.int32, (tm, tm), 0).astype(F32)
        p_ref[...] = _bf(jnp.where(rank_row == slot, 1.0, 0.0))
        slot_l = lax.broadcasted_iota(jnp.int32, (tm, MOE_ROWS), 1).astype(F32)
        for b in range(nb):
            pt_ref[b] = _bf(jnp.where(rank_col == slot_l + float(b * MOE_ROWS), 1.0, 0.0))
        cnt = jnp.sum(jnp.where(rank_row >= 0.0, 1.0, 0.0)).astype(jnp.int32)
        nblk = (cnt + (MOE_ROWS - 1)) // MOE_ROWS
        nblk_ref[0] = nblk
        g_hi, g_lo = _split(gt_ref[g])

        def compact(blk, carry):
            rows = pl.ds(pl.multiple_of(blk * MOE_ROWS, MOE_ROWS), MOE_ROWS)
            pb = p_ref[rows, :]
            xg_ref[rows, :] = _bf(_dot(pb, h_ref[...]))
            wr_ref[rows, :] = _dot_nt(pb, g_hi) + _dot_nt(pb, g_lo)
            yg_ref[rows, :] = jnp.zeros((MOE_ROWS, yg_ref.shape[1]), F32)
            return carry

        lax.fori_loop(0, nblk, compact, 0)

    def expert(blk, carry):
        rows = pl.ds(pl.multiple_of(blk * MOE_ROWS, MOE_ROWS), MOE_ROWS)
        xb = xg_ref[rows, :]
        hg = _dot(xb, wg_ref[0, 0])
        hu = _dot(xb, wu_ref[0, 0])
        lane8 = lax.broadcasted_iota(jnp.int32, (MOE_ROWS, sub8), 1)
        wcol = jnp.sum(jnp.where(lane8 == j, wr_ref[rows, :], 0.0), axis=1, keepdims=True)
        he = hg * jax.nn.sigmoid(hg) * hu * wcol
        yg_ref[rows, :] += _dot(_bf(he), wd_ref[0, 0])
        return carry

    lax.fori_loop(0, nblk_ref[0], expert, 0)

    @pl.when(j == EXPERTS_PER_GROUP - 1)
    def _():
        def spread(blk, carry):
            rows = pl.ds(pl.multiple_of(blk * MOE_ROWS, MOE_ROWS), MOE_ROWS)
            acc_ref[...] += _dot(pt_ref[blk], _bf(yg_ref[rows, :]))
            return carry

        lax.fori_loop(0, nblk_ref[0], spread, 0)

    @pl.when(e == n_exp - 1)
    def _():
        out = x_ref[0] + mod_ref[0][5:6] * acc_ref[...]
        if final:
            out = out * lax.rsqrt(jnp.mean(out * out, axis=-1, keepdims=True) + RMS_EPS) * vec_ref[1:2]
        o_ref[0] = out


def _moe_call(x, mod, per_batch, vec, rwt, rb, wg, wu, wd, layer, final):
    b, t, d = x.shape
    tm = min(1024, t)
    _, n_exp, _, de = wg.shape
    mod_map = (lambda i, j, e: (i, 0, 0)) if per_batch else (lambda i, j, e: (0, 0, 0))
    tile = pl.BlockSpec((1, tm, d), lambda i, j, e: (i, j, 0))
    cst = lambda shape: pl.BlockSpec(shape, lambda i, j, e: (0,) * len(shape))
    return pl.pallas_call(
        functools.partial(_moe_kernel, n_exp=n_exp, final=final),
        grid=(b, t // tm, n_exp),
        in_specs=[tile, pl.BlockSpec((1, 6, d), mod_map), cst(vec.shape), cst(rwt.shape), cst(rb.shape),
                  pl.BlockSpec((1, 1, d, de), lambda i, j, e: (layer, e, 0, 0)),
                  pl.BlockSpec((1, 1, d, de), lambda i, j, e: (layer, e, 0, 0)),
                  pl.BlockSpec((1, 1, de, d), lambda i, j, e: (layer, e, 0, 0))],
        out_specs=tile,
        out_shape=jax.ShapeDtypeStruct((b, t, d), F32),
        scratch_shapes=[
            pltpu.VMEM((tm, d), BF16),
            pltpu.VMEM((N_EXPERT_GROUPS, 8, tm), F32),
            pltpu.VMEM((8, tm), F32),
            pltpu.VMEM((tm, LANE), F32),
            pltpu.VMEM((tm, tm), BF16),
            pltpu.VMEM((tm // MOE_ROWS, tm, MOE_ROWS), BF16),
            pltpu.VMEM((tm, d), BF16),
            pltpu.VMEM((tm, d), F32),
            pltpu.VMEM((tm, 8), F32),
            pltpu.VMEM((tm, d), F32),
            pltpu.SMEM((1,), jnp.int32),
        ],
        compiler_params=_params(("arbitrary", "arbitrary", "arbitrary")),
        name="grouped_moe",
    )(x, mod, vec, rwt, rb, wg, wu, wd)


def _pad_rows(a, rows):
    return jnp.concatenate([a, jnp.zeros((rows - a.shape[0],) + a.shape[1:], a.dtype)], axis=0)


def _lane_stacked_state(s0):
    b, two, h, n, _ = s0.shape
    return jnp.swapaxes(s0, 2, 3).reshape(b, two, n, h * n)


def kernel(x_prompt, x_sample, state_rwkv, c, c_ctx, norm_g, ada_w, ada_b, final_g, rwkv_mu, rwkv_w_rkv, rwkv_w_o, rwkv_w0, rwkv_w1, rwkv_w2, rwkv_a0, rwkv_a1, rwkv_a2, rwkv_g1, rwkv_g2, rwkv_k_k, rwkv_k_a, rwkv_r_k, rwkv_gn_g, rwkv_gn_b, conv_pw1, conv_pw1_b, conv_dw, conv_dw_b, conv_ln_g, conv_ln_b, conv_pw2, conv_pw2_b, router_w, router_b, moe_w_gate, moe_w_up, moe_w_down):
    d = x_prompt.shape[-1]
    depth = ada_w.shape[0]
    n_heads = d // HEAD_DIM
    dec_b = c.shape[0]

    cond = _pad_rows(jnp.concatenate([c_ctx[None, :], c], axis=0), COND_ROWS)
    mod = _mod_call(cond, ada_w, ada_b).reshape(depth, COND_ROWS, 6, d)

    head_of_lane = jnp.arange(d) // HEAD_DIM
    hsel = (head_of_lane[:, None] == jnp.arange(LANE)[None, :]).astype(BF16)
    hselt = hsel.T
    rwt = router_w.T
    rb = router_b[:, None]
    wg = _bf(moe_w_gate)
    wu = _bf(moe_w_up)
    wd = _bf(moe_w_down)

    def lora_pad(w2):
        z = jnp.zeros_like(w2[0])
        return _bf(jnp.stack([jnp.concatenate([w2[0], z], 0), jnp.concatenate([z, w2[1]], 0)]))

    groups = (
        dict(x=x_prompt, rows=slice(0, 1), per_batch=False, s0=None, want_state=True, seg=x_prompt.shape[1]),
        dict(x=x_sample, rows=slice(1, 1 + dec_b), per_batch=True, s0=state_rwkv, want_state=False, seg=GRID_W),
    )
    outs = []
    new_state = None
    for gr in groups:
        x = gr["x"]
        states = []
        for i in range(depth):
            m_i = mod[i, gr["rows"]]
            j = i // 2
            if i % 2 == 0:
                vec = jnp.stack([norm_g[i, 0], rwkv_k_k[j], rwkv_k_a[j], rwkv_r_k[j].reshape(d),
                                 rwkv_w0[j, 0], rwkv_w0[j, 1], rwkv_a0[j, 0], rwkv_a0[j, 1]])
                r, v, kkn, logd, kd, bd, gate, bonus = _rwkv_pre_call(
                    x, m_i, gr["per_batch"], vec, rwkv_mu[j], _bf(rwkv_w_rkv[j]), _bf(rwkv_g1[j]), _bf(rwkv_g2[j]),
                    _bf(jnp.concatenate([rwkv_w1[j, 0], rwkv_w1[j, 1]], axis=1)), lora_pad(rwkv_w2[j]),
                    _bf(jnp.concatenate([rwkv_a1[j, 0], rwkv_a1[j, 1]], axis=1)), lora_pad(rwkv_a2[j]),
                    hsel, hselt)
                s0_bd = None if gr["s0"] is None else _lane_stacked_state(gr["s0"][:, j])
                res = _wkv_call(r, v, kkn, logd, kd, bd, s0_bd, gr["want_state"])
                if gr["want_state"]:
                    states.append(res[2])
                vec = _pad_rows(jnp.stack([rwkv_gn_g[j], rwkv_gn_b[j]]), 8)
                x = _rwkv_post_call(x, res[0], res[1], bonus, gate, m_i, gr["per_batch"], vec,
                                    _bf(rwkv_w_o[j]), hsel, hselt)
            else:
                vec = _pad_rows(jnp.stack([norm_g[i, 0], conv_dw_b[j], conv_ln_g[j], conv_ln_b[j], conv_pw2_b[j]]), 8)
                dw = conv_dw[j].reshape(CONV_WIDTH, d // LANE, 1, LANE)
                x = _conv_call(x, m_i, gr["per_batch"], gr["seg"], vec, _bf(conv_pw1[j]),
                               conv_pw1_b[j][None, :], dw, _bf(conv_pw2[j]))
            vec = _pad_rows(jnp.stack([norm_g[i, 1], final_g]), 8)
            xm = x if gr["per_batch"] else x.reshape(1, -1, d)
            xm = _moe_call(xm, m_i, gr["per_batch"], vec, rwt, rb, wg, wu, wd, i, final=(i == depth - 1))
            x = xm.reshape(x.shape)
        outs.append(x)
        if gr["want_state"]:
            new_state = jnp.stack(states, axis=1)
    return (outs[0], outs[1], new_state)
```

```python
import functools

import jax
import jax.numpy as jnp
from jax import lax
from jax.experimental import pallas as pl
from jax.experimental.pallas import tpu as pltpu

F32 = jnp.float32
BF16 = jnp.bfloat16

HEAD_DIM = 64
HEADS_PER_BLOCK = 2
BLOCK_LANES = HEAD_DIM * HEADS_PER_BLOCK
CHUNK = 64
WKV_BLOCKS_PER_STEP = 8
GRID_W = 64
MOE_ROWS = 256
CONV_WIDTH = 31
CONV_PAD = 16
N_EXPERT_GROUPS = 4
EXPERTS_PER_GROUP = 4
LANE = 128
COND_ROWS = 16
RMS_EPS = 1e-6
LN_EPS = 1e-5
GN_EPS = 64e-5
NEG_EXP_M_HALF = -0.6065306597126334
VMEM_LIMIT_BYTES = 56 * 1024 * 1024


def _bf(x):
    return x.astype(BF16)


def _dot(a, b):
    return jnp.dot(a, b, preferred_element_type=F32)


def _dot_nt(a, b):
    return lax.dot_general(a, b, (((1,), (1,)), ((), ())), preferred_element_type=F32)


def _split(x):
    hi = _bf(x)
    lo = _bf(x - hi.astype(F32))
    return hi, lo


def _dot_split(a, b_exact):
    hi, lo = _split(a)
    return _dot(hi, b_exact) + _dot(lo, b_exact)


def _rms_mod(x, g, shift, scale):
    y = x * lax.rsqrt(jnp.mean(x * x, axis=-1, keepdims=True) + RMS_EPS) * g
    return y * (1.0 + scale) + shift


def _head_sum(z, hsel, hselt):
    return _dot_split(_dot(_bf(z), hsel), hselt)


def _params(sem):
    return pltpu.CompilerParams(dimension_semantics=sem, vmem_limit_bytes=VMEM_LIMIT_BYTES)


def _const_spec(shape):
    nd = len(shape)
    return pl.BlockSpec(shape, lambda *_: (0,) * nd, pipeline_mode=pl.Buffered(1))


def _mod_kernel(c_ref, w_ref, b_ref, o_ref):
    c = c_ref[...]
    s = c * jax.nn.sigmoid(c)
    o_ref[0] = jnp.dot(s, w_ref[0], preferred_element_type=F32,
                       precision=lax.Precision.HIGHEST) + b_ref[0]


def _mod_call(cond, ada_w, ada_b):
    depth, d, n = ada_w.shape
    tn = 1536
    return pl.pallas_call(
        _mod_kernel,
        grid=(depth, n // tn),
        in_specs=[
            pl.BlockSpec((COND_ROWS, d), lambda l, j: (0, 0)),
            pl.BlockSpec((1, d, tn), lambda l, j: (l, 0, j)),
            pl.BlockSpec((1, 1, tn), lambda l, j: (l, 0, j)),
        ],
        out_specs=pl.BlockSpec((1, COND_ROWS, tn), lambda l, j: (l, 0, j)),
        out_shape=jax.ShapeDtypeStruct((depth, COND_ROWS, n), F32),
        compiler_params=_params(("arbitrary", "arbitrary")),
        name="adaln_mod",
    )(cond, ada_w, ada_b.reshape(depth, 1, n))


def _rwkv_pre_kernel(x_ref, xp_ref, xn_ref, mod_ref, vec_ref, mu_ref, wrkv_ref, g1_ref, g2_ref,
                     w1_ref, w2_ref, a1_ref, a2_ref, hsel_ref, hselt_ref,
                     r_ref, v_ref, kkn_ref, logd_ref, kd_ref, bd_ref, gate_ref, bonus_ref,
                     *, tm, nt):
    t = pl.program_id(1)
    mod = mod_ref[0]
    shift, scale = mod[0:1], mod[1:2]
    vec = vec_ref[...]
    g, k_k, k_a, r_k = vec[0:1], vec[1:2], vec[2:3], vec[3:4]
    hsel = hsel_ref[...]
    hselt = hselt_ref[...]

    h = _rms_mod(x_ref[0], g, shift, scale)
    h_prev = _rms_mod(xp_ref[0], g, shift, scale)[7:8]
    h_next = _rms_mod(xn_ref[0], g, shift, scale)[0:1]
    h_prev = jnp.where(t == 0, 0.0, h_prev)
    h_next = jnp.where(t == nt - 1, 0.0, h_next)
    row = lax.broadcasted_iota(jnp.int32, (tm, 1), 0)
    prev = jnp.where(row == 0, h_prev, pltpu.roll(h, 1, 0))
    nxt = jnp.where(row == tm - 1, h_next, pltpu.roll(h, tm - 1, 0))
    xx = 0.5 * (prev + nxt) - h
    mu = mu_ref[...]

    def mix(i):
        return _bf(h + xx * mu[i:i + 1])

    r = _dot(mix(0), wrkv_ref[0])
    k = _dot(mix(2), wrkv_ref[1])
    v = _dot(mix(3), wrkv_ref[2])
    gate = _dot(_bf(jax.nn.sigmoid(_dot(mix(5), g1_ref[...]))), g2_ref[...])
    tw = _bf(jnp.tanh(_dot(mix(1), w1_ref[...])))
    ta = _bf(_dot(mix(4), a1_ref[...]))

    kk = k * k_k
    kkn = kk * lax.rsqrt(jnp.maximum(_head_sum(kk * kk, hsel, hselt), 1e-24))
    r_ref[0] = _bf(r)
    v_ref[0] = _bf(v)
    kkn_ref[0] = _bf(kkn)
    gate_ref[0] = _bf(gate)

    ksum = None
    for d in range(2):
        u = vec[4 + d:5 + d] + _dot(tw, w2_ref[d])
        logd_ref[d, 0] = NEG_EXP_M_HALF * jax.nn.sigmoid(u)
        a = jax.nn.sigmoid(vec[6 + d:7 + d] + _dot(ta, a2_ref[d]))
        kd = k * (1.0 + (a - 1.0) * k_a)
        kd_ref[d, 0] = _bf(kd)
        bd_ref[d, 0] = _bf(kkn * a)
        ksum = kd if ksum is None else ksum + kd
    bonus_ref[0] = _bf(_head_sum(r * r_k * ksum, hsel, hselt) * v)


def _rwkv_pre_call(x, mod, per_batch, vec, mu, wrkv, g1, g2, w1c, w2p, a1c, a2p, hsel, hselt):
    b, t, d = x.shape
    tm = min(256, t)
    nt = t // tm
    r8 = tm // 8
    nb8 = t // 8
    mod_map = (lambda i, j: (i, 0, 0)) if per_batch else (lambda i, j: (0, 0, 0))
    tile = pl.BlockSpec((1, tm, d), lambda i, j: (i, j, 0))
    tile2 = pl.BlockSpec((2, 1, tm, d), lambda i, j: (0, i, j, 0))
    one = jax.ShapeDtypeStruct((b, t, d), BF16)
    two = jax.ShapeDtypeStruct((2, b, t, d), BF16)
    two_f32 = jax.ShapeDtypeStruct((2, b, t, d), F32)
    return pl.pallas_call(
        functools.partial(_rwkv_pre_kernel, tm=tm, nt=nt),
        grid=(b, nt),
        in_specs=[
            tile,
            pl.BlockSpec((1, 8, d), lambda i, j: (i, jnp.maximum(j * r8 - 1, 0), 0)),
            pl.BlockSpec((1, 8, d), lambda i, j: (i, jnp.minimum((j + 1) * r8, nb8 - 1), 0)),
            pl.BlockSpec((1, 6, d), mod_map),
            _const_spec(vec.shape), _const_spec(mu.shape), _const_spec(wrkv.shape),
            _const_spec(g1.shape), _const_spec(g2.shape), _const_spec(w1c.shape),
            _const_spec(w2p.shape), _const_spec(a1c.shape), _const_spec(a2p.shape),
            _const_spec(hsel.shape), _const_spec(hselt.shape),
        ],
        out_specs=[tile, tile, tile, tile2, tile2, tile2, tile, tile],
        out_shape=[one, one, one, two_f32, two, two, one, one],
        compiler_params=_params(("arbitrary", "arbitrary")),
        name="rwkv_pre",
    )(x, x, x, mod, vec, mu, wrkv, g1, g2, w1c, w2p, a1c, a2p, hsel, hselt)


def _wkv_chains(chains, bd_mask):
    c = CHUNK
    n = BLOCK_LANES
    idx = range(len(chains))
    rev = [ch["reverse"] for ch in chains]
    ri = lax.broadcasted_iota(jnp.int32, (c, c), 0)
    ci = lax.broadcasted_iota(jnp.int32, (c, c), 1)
    lmat = {False: _bf(jnp.where(ri >= ci, 1.0, 0.0)), True: _bf(jnp.where(ri <= ci, 1.0, 0.0))}
    row_t = lax.broadcasted_iota(jnp.int32, (c, n), 0)
    col_s = lax.broadcasted_iota(jnp.int32, (c, n), 1) % c
    strict = {False: row_t > col_s, True: row_t < col_s}
    incl = {False: row_t >= col_s, True: row_t <= col_s}
    zeros = jnp.zeros((c, n), F32)

    def blockdiag(z):
        return jnp.concatenate([_bf(z)] * HEADS_PER_BLOCK, axis=0) * bd_mask

    logd = [ch["logd"] for ch in chains]
    split = [_split(x) for x in logd]
    cum = [_dot(lmat[rev[i]], split[i][0]) + _dot(lmat[rev[i]], split[i][1]) for i in idx]
    tot = [cum[i][0:1] if rev[i] else cum[i][c - 1:c] for i in idx]
    e_in = [jnp.exp(cum[i]) for i in idx]
    e_ex = [jnp.exp(cum[i] - logd[i]) for i in idx]
    e_inv = [jnp.exp(-cum[i]) for i in idx]
    e_bar = [jnp.exp(tot[i] - cum[i]) for i in idx]
    p_c = [jnp.exp(tot[i]) for i in idx]

    a_t = [-chains[i]["kkn"] * e_ex[i] for i in idx]
    r_t = [chains[i]["r"] * e_in[i] for i in idx]
    b_t = [chains[i]["bd"] * e_inv[i] for i in idx]
    k_t = [chains[i]["kd"] * e_inv[i] for i in idx]
    b_bar = [chains[i]["bd"] * e_bar[i] for i in idx]
    k_bar = [chains[i]["kd"] * e_bar[i] for i in idx]

    ycat = [jnp.concatenate([blockdiag(b_t[i]), blockdiag(k_t[i])], axis=0) for i in idx]
    g = [_dot_nt(_bf(jnp.concatenate([a_t[i], r_t[i]], axis=0)), ycat[i]) for i in idx]
    a_m = [jnp.where(strict[rev[i]], g[i][:c, :n], 0.0) for i in idx]
    ak_b = [_bf(jnp.where(strict[rev[i]], g[i][:c, n:], 0.0)) for i in idx]
    rb_b = [_bf(jnp.where(incl[rev[i]], g[i][c:, :n], 0.0)) for i in idx]
    rk_b = [_bf(jnp.where(incl[rev[i]], g[i][c:, n:], 0.0)) for i in idx]

    vst = [blockdiag(chains[i]["v"]) for i in idx]
    z = [_dot(ak_b[i], vst[i]) for i in idx]

    nm = a_m
    q = [_dot(_bf(a_m[i]), blockdiag(a_m[i])) for i in idx]
    levels = CHUNK.bit_length() - 2
    for j in range(levels):
        qb = [_bf(q[i]) for i in idx]
        nm = [nm[i] + q[i] + _dot(qb[i], blockdiag(nm[i])) for i in idx]
        if j + 1 < levels:
            q = [_dot(qb[i], blockdiag(q[i])) for i in idx]

    t_az = [jnp.concatenate([a_t[i], z[i]], axis=1)
            + _dot(_bf(nm[i]), jnp.concatenate([blockdiag(a_t[i]), blockdiag(z[i])], axis=1)) for i in idx]
    a_hat = [t_az[i][:, :n] for i in idx]
    tz = [t_az[i][:, n:] for i in idx]
    r_hat = [r_t[i] + _dot(rb_b[i], blockdiag(a_hat[i])) for i in idx]
    y0 = [_dot(rb_b[i], blockdiag(tz[i])) + _dot(rk_b[i], vst[i]) for i in idx]

    atz_t = [_bf(jnp.transpose(jnp.concatenate([a_hat[i], tz[i]], axis=0))) for i in idx]
    m_t = [_bf(_dot(atz_t[i], _bf(jnp.concatenate([b_bar[i], zeros], axis=0)))) * bd_mask for i in idx]
    k_rows = [jnp.concatenate([k_bar[i], zeros] if chains[i]["v_first"] else [zeros, k_bar[i]], axis=0) for i in idx]
    bd_mask_f = bd_mask.astype(F32)
    n0_bd = [(_dot(atz_t[i], _bf(jnp.concatenate([zeros, b_bar[i]], axis=0)))
              + _dot(_bf(chains[i]["vt"]), _bf(k_rows[i]))) * bd_mask_f for i in idx]
    n0 = [sum(x[hb * c:(hb + 1) * c] for hb in range(1, HEADS_PER_BLOCK)) + x[0:c] for x in n0_bd]

    st = [chains[i]["state"] for i in idx]
    y = [_dot_nt(_bf(r_hat[i]), blockdiag(st[i])) + y0[i] for i in idx]
    new_state = [st[i] * p_c[i] + _dot(_bf(st[i]), m_t[i]) + n0[i] for i in idx]
    return list(zip(y, new_state))


def _wkv_kernel(*refs, nc, has_s0, want_state):
    (rf, vf, af, ldf, kf, bf_, rb, vb, ab, ldb, kb, bb) = refs[:12]
    pos = 12
    s0_ref = None
    if has_s0:
        s0_ref = refs[pos]
        pos += 1
    yf_ref, yb_ref = refs[pos], refs[pos + 1]
    pos += 2
    so_ref = None
    if want_state:
        so_ref = refs[pos]
        pos += 1
    st_ref = refs[pos]
    ci = pl.program_id(2)
    n = BLOCK_LANES

    @pl.when(ci == 0)
    def _():
        if has_s0:
            st_ref[...] = s0_ref[0]
        else:
            st_ref[...] = jnp.zeros(st_ref.shape, F32)

    rowb = lax.broadcasted_iota(jnp.int32, (n, n), 0) // HEAD_DIM
    colb = lax.broadcasted_iota(jnp.int32, (n, n), 1) // HEAD_DIM
    bd_mask = _bf(jnp.where(rowb == colb, 1.0, 0.0))

    chains = []
    for qi in range(WKV_BLOCKS_PER_STEP):
        ln = slice(qi * n, (qi + 1) * n)
        f32 = lambda ref, *ix: ref[ix + (slice(None), ln)].astype(F32)
        v_f = f32(vf, 0)
        v_b = f32(vb, 0)
        vt = jnp.transpose(jnp.concatenate([v_f, v_b], axis=0))
        chains.append(dict(r=f32(rf, 0), v=v_f, kkn=f32(af, 0), logd=ldf[0, 0, :, ln], kd=f32(kf, 0, 0),
                           bd=f32(bf_, 0, 0), vt=vt, v_first=True, state=st_ref[0, :, ln], reverse=False))
        chains.append(dict(r=f32(rb, 0), v=v_b, kkn=f32(ab, 0), logd=ldb[0, 0, :, ln], kd=f32(kb, 0, 0),
                           bd=f32(bb, 0, 0), vt=vt, v_first=False, state=st_ref[1, :, ln], reverse=True))
    res = _wkv_chains(chains, bd_mask)
    finals = []
    for qi in range(WKV_BLOCKS_PER_STEP):
        ln = slice(qi * n, (qi + 1) * n)
        (y_f, s_f), (y_b, s_b) = res[2 * qi], res[2 * qi + 1]
        yf_ref[0, :, ln] = y_f
        yb_ref[0, :, ln] = y_b
        st_ref[0, :, ln] = s_f
        st_ref[1, :, ln] = s_b
        finals.append((s_f, s_b))

    if want_state:
        @pl.when(ci == nc - 1)
        def _():
            for qi, pair in enumerate(finals):
                for d, s in enumerate(pair):
                    for i in range(HEADS_PER_BLOCK):
                        so_ref[0, d, qi * HEADS_PER_BLOCK + i] = s[:, i * HEAD_DIM:(i + 1) * HEAD_DIM]


def _wkv_call(r, v, kkn, logd, kd, bd, s0_bd, want_state):
    b, t, d = r.shape
    c = CHUNK
    nc = t // c
    n = BLOCK_LANES * WKV_BLOCKS_PER_STEP
    nq = d // n
    fwd = pl.BlockSpec((1, c, n), lambda i, q, j: (i, j, q))
    bwd = pl.BlockSpec((1, c, n), lambda i, q, j: (i, nc - 1 - j, q))
    fwd2 = pl.BlockSpec((1, 1, c, n), lambda i, q, j: (0, i, j, q))
    bwd2 = pl.BlockSpec((1, 1, c, n), lambda i, q, j: (1, i, nc - 1 - j, q))
    in_specs = [fwd, fwd, fwd, fwd2, fwd2, fwd2, bwd, bwd, bwd, bwd2, bwd2, bwd2]
    args = [r, v, kkn, logd, kd, bd, r, v, kkn, logd, kd, bd]
    has_s0 = s0_bd is not None
    if has_s0:
        in_specs.append(pl.BlockSpec((1, 2, HEAD_DIM, n), lambda i, q, j: (i, 0, 0, q)))
        args.append(s0_bd)
    out_specs = [fwd, bwd]
    out_shape = [jax.ShapeDtypeStruct((b, t, d), F32), jax.ShapeDtypeStruct((b, t, d), F32)]
    if want_state:
        out_specs.append(pl.BlockSpec((1, 2, n // HEAD_DIM, HEAD_DIM, HEAD_DIM),
                                      lambda i, q, j: (i, 0, q, 0, 0)))
        out_shape.append(jax.ShapeDtypeStruct((b, 2, d // HEAD_DIM, HEAD_DIM, HEAD_DIM), F32))
    return pl.pallas_call(
        functools.partial(_wkv_kernel, nc=nc, has_s0=has_s0, want_state=want_state),
        grid=(b, nq, nc),
        in_specs=in_specs,
        out_specs=out_specs,
        out_shape=out_shape,
        scratch_shapes=[pltpu.VMEM((2, HEAD_DIM, n), F32)],
        compiler_params=_params(("arbitrary", "arbitrary", "arbitrary")),
        name="wkv_scan",
    )(*args)


def _rwkv_post_kernel(x_ref, yf_ref, yb_ref, bonus_ref, gate_ref, mod_ref, vec_ref, wo_ref,
                      hsel_ref, hselt_ref, o_ref):
    hsel = hsel_ref[...]
    hselt = hselt_ref[...]
    vec = vec_ref[...]
    gn_g, gn_b = vec[0:1], vec[1:2]
    gt = mod_ref[0][2:3]
    y = yf_ref[0] + yb_ref[0]
    inv = 1.0 / HEAD_DIM
    mean = _head_sum(y, hsel, hselt) * inv
    yc = y - mean
    var = _head_sum(yc * yc, hsel, hselt) * inv
    yn = yc * lax.rsqrt(var + GN_EPS) * gn_g + gn_b + bonus_ref[0].astype(F32)
    out = _dot(_bf(yn * gate_ref[0].astype(F32)), wo_ref[...])
    o_ref[0] = x_ref[0] + gt * out


def _rwkv_post_call(x, yf, yb, bonus, gate, mod, per_batch, vec, wo, hsel, hselt):
    b, t, d = x.shape
    tm = min(512, t)
    mod_map = (lambda i, j: (i, 0, 0)) if per_batch else (lambda i, j: (0, 0, 0))
    tile = pl.BlockSpec((1, tm, d), lambda i, j: (i, j, 0))
    return pl.pallas_call(
        _rwkv_post_kernel,
        grid=(b, t // tm),
        in_specs=[tile, tile, tile, tile, tile, pl.BlockSpec((1, 6, d), mod_map),
                  _const_spec(vec.shape), _const_spec(wo.shape),
                  _const_spec(hsel.shape), _const_spec(hselt.shape)],
        out_specs=tile,
        out_shape=jax.ShapeDtypeStruct((b, t, d), F32),
        compiler_params=_params(("arbitrary", "arbitrary")),
        name="rwkv_post",
    )(x, yf, yb, bonus, gate, mod, vec, wo, hsel, hselt)


def _conv_kernel(x_ref, mod_ref, vec_ref, pw1_ref, pw1b_ref, dw_ref, pw2_ref, o_ref,
                 pad_ref, cv_ref, *, tm, seg):
    d = x_ref.shape[-1]
    nlb = d // LANE
    nseg = tm // seg
    mod = mod_ref[0]
    shift, scale, gt = mod[0:1], mod[1:2], mod[2:3]
    vec = vec_ref[...]
    g, dw_b, ln_g, ln_b, pw2_b = vec[0:1], vec[1:2], vec[2:3], vec[3:4], vec[4:5]
    x = x_ref[0]
    h = _rms_mod(x, g, shift, scale)
    u = _dot(_bf(h), pw1_ref[...]) + pw1b_ref[...]
    u = u[:, :d] * jax.nn.sigmoid(u[:, d:])

    zpad = jnp.zeros((nseg, CONV_PAD, LANE), F32)
    for lb in range(nlb):
        pad_ref[:, lb, 0:CONV_PAD, :] = zpad
        pad_ref[:, lb, CONV_PAD + seg:2 * CONV_PAD + seg, :] = zpad
        pad_ref[:, lb, CONV_PAD:CONV_PAD + seg, :] = u[:, lb * LANE:(lb + 1) * LANE].reshape(nseg, seg, LANE)

    base = CONV_PAD - CONV_WIDTH // 2

    def body(idx, carry):
        s = idx // nlb
        lb = idx % nlb
        acc = jnp.zeros((seg, LANE), F32)
        for j in range(CONV_WIDTH):
            acc = acc + pad_ref[s, lb, base + j:base + j + seg, :] * dw_ref[j, lb]
        cv_ref[s, lb] = acc
        return carry

    lax.fori_loop(0, nseg * nlb, body, 0)

    cv = jnp.concatenate([cv_ref[:, lb].reshape(tm, LANE) for lb in range(nlb)], axis=1) + dw_b
    mu = jnp.mean(cv, axis=-1, keepdims=True)
    cc = cv - mu
    var = jnp.mean(cc * cc, axis=-1, keepdims=True)
    z = cc * lax.rsqrt(var + LN_EPS) * ln_g + ln_b
    z = z * jax.nn.sigmoid(z)
    out = _dot(_bf(z), pw2_ref[...]) + pw2_b
    o_ref[0] = x + gt * out


def _conv_call(x, mod, per_batch, seg, vec, pw1, pw1b, dw, pw2):
    b, t, d = x.shape
    tm = min(512, t)
    seg = min(seg, tm)
    nlb = d // LANE
    mod_map = (lambda i, j: (i, 0, 0)) if per_batch else (lambda i, j: (0, 0, 0))
    tile = pl.BlockSpec((1, tm, d), lambda i, j: (i, j, 0))
    return pl.pallas_call(
        functools.partial(_conv_kernel, tm=tm, seg=seg),
        grid=(b, t // tm),
        in_specs=[tile, pl.BlockSpec((1, 6, d), mod_map), _const_spec(vec.shape),
                  _const_spec(pw1.shape), _const_spec(pw1b.shape), _const_spec(dw.shape),
                  _const_spec(pw2.shape)],
        out_specs=tile,
        out_shape=jax.ShapeDtypeStruct((b, t, d), F32),
        scratch_shapes=[pltpu.VMEM((tm // seg, nlb, seg + 2 * CONV_PAD, LANE), F32),
                        pltpu.VMEM((tm // seg, nlb, seg, LANE), F32)],
        compiler_params=_params(("arbitrary", "arbitrary")),
        name="conv_module",
    )(x, mod, vec, pw1, pw1b, dw, pw2)


def _router_gates_t(h, rwt_ref, rb_ref):
    h_hi, h_lo = _split(h)
    w_hi, w_lo = _split(rwt_ref[...])
    logits = _dot_nt(w_hi, h_hi) + _dot_nt(w_hi, h_lo) + _dot_nt(w_lo, h_hi) + rb_ref[...]
    m = jnp.max(logits, axis=0, keepdims=True)
    ex = jnp.exp(logits - m)
    p = ex / jnp.sum(ex, axis=0, keepdims=True)
    rows = [p[e:e + 1] for e in range(N_EXPERT_GROUPS * EXPERTS_PER_GROUP)]
    scores = []
    for gi in range(N_EXPERT_GROUPS):
        a, b, c, dd = rows[4 * gi:4 * gi + 4]
        hi1, lo1 = jnp.maximum(a, b), jnp.minimum(a, b)
        hi2, lo2 = jnp.maximum(c, dd), jnp.minimum(c, dd)
        scores.append(jnp.maximum(hi1, hi2) + jnp.maximum(jnp.minimum(hi1, hi2), jnp.maximum(lo1, lo2)))
    best = scores[0]
    sel = jnp.zeros(best.shape, jnp.int32)
    for gi in range(1, N_EXPERT_GROUPS):
        better = scores[gi] > best
        sel = jnp.where(better, gi, sel)
        best = jnp.where(better, scores[gi], best)
    qs = []
    for j in range(EXPERTS_PER_GROUP):
        qj = jnp.zeros(best.shape, F32)
        for gi in range(N_EXPERT_GROUPS):
            qj = jnp.where(sel == gi, rows[4 * gi + j], qj)
        qs.append(qj)

    def argmax4(vals):
        bv, bi = vals[0], jnp.zeros(best.shape, jnp.int32)
        for j in range(1, EXPERTS_PER_GROUP):
            better = vals[j] > bv
            bi = jnp.where(better, j, bi)
            bv = jnp.where(better, vals[j], bv)
        return bv, bi

    v1, i1 = argmax4(qs)
    v2, i2 = argmax4([jnp.where(i1 == j, -1.0, qs[j]) for j in range(EXPERTS_PER_GROUP)])
    den = v1 + v2
    w1, w2 = v1 / den, v2 / den
    gates = []
    for gi in range(N_EXPERT_GROUPS):
        rows_g = [jnp.where(sel == gi, jnp.where(i1 == j, w1, 0.0) + jnp.where(i2 == j, w2, 0.0), 0.0)
                  for j in range(EXPERTS_PER_GROUP)]
        gates.append(jnp.concatenate(rows_g, axis=0))
    return gates, sel


def _moe_kernel(x_ref, mod_ref, vec_ref, rwt_ref, rb_ref, wg_ref, wu_ref, wd_ref, o_ref,
                h_ref, gt_ref, rk_ref, rc_ref, p_ref, pt_ref, xg_ref, yg_ref, wr_ref, acc_ref, nblk_ref,
                *, n_exp, final):
    e = pl.program_id(2)
    tm = x_ref.shape[1]
    nb = tm // MOE_ROWS
    g = e // EXPERTS_PER_GROUP
    j = e % EXPERTS_PER_GROUP
    sub8 = 8

    @pl.when(e == 0)
    def _():
        mod = mod_ref[0]
        h = _rms_mod(x_ref[0], vec_ref[0:1], mod[3:4], mod[4:5])
        h_ref[...] = _bf(h)
        gates, sel = _router_gates_t(h, rwt_ref, rb_ref)
        zero4 = jnp.zeros((sub8 - EXPERTS_PER_GROUP, tm), F32)
        for gi in range(N_EXPERT_GROUPS):
            gt_ref[gi] = jnp.concatenate([gates[gi], zero4], axis=0)
        chosen = [jnp.where(sel == gi, 1.0, 0.0) for gi in range(N_EXPERT_GROUPS)]
        selmat = _bf(jnp.concatenate(chosen + [jnp.zeros((sub8 - N_EXPERT_GROUPS, tm), F32)], axis=0))
        earlier = _bf(jnp.where(lax.broadcasted_iota(jnp.int32, (tm, tm), 0)
                                < lax.broadcasted_iota(jnp.int32, (tm, tm), 1), 1.0, 0.0))
        prefix = _dot(selmat, earlier)
        rk = jnp.where(selmat > 0, prefix, -1.0)
        rk_ref[...] = rk
        rc_ref[...] = jnp.transpose(jnp.concatenate([rk, jnp.zeros((LANE - sub8, tm), F32)], axis=0))
        acc_ref[...] = jnp.zeros(acc_ref.shape, F32)

    @pl.when(j == 0)
    def _():
        sub_g = lax.broadcasted_iota(jnp.int32, (sub8, tm), 0)
        rank_row = jnp.sum(jnp.where(sub_g == g, rk_ref[...], 0.0), axis=0, keepdims=True)
        lane_g = lax.broadcasted_iota(jnp.int32, (tm, LANE), 1)
        rank_col = jnp.sum(jnp.where(lane_g == g, rc_ref[...], 0.0), axis=1, keepdims=True)
        slot = lax.broadcasted_iota(jnp.int32, (tm, tm), 0).astype(F32)
        p_ref[...] = _bf(jnp.where(rank_row == slot, 1.0, 0.0))
        slot_l = lax.broadcasted_iota(jnp.int32, (tm, MOE_ROWS), 1).astype(F32)
        for b in range(nb):
            pt_ref[b] = _bf(jnp.where(rank_col == slot_l + float(b * MOE_ROWS), 1.0, 0.0))
        cnt = jnp.sum(jnp.where(rank_row >= 0.0, 1.0, 0.0)).astype(jnp.int32)
        nblk = (cnt + (MOE_ROWS - 1)) // MOE_ROWS
        nblk_ref[0] = nblk
        g_hi, g_lo = _split(gt_ref[g])

        def compact(blk, carry):
            rows = pl.ds(pl.multiple_of(blk * MOE_ROWS, MOE_ROWS), MOE_ROWS)
            pb = p_ref[rows, :]
            xg_ref[rows, :] = _bf(_dot(pb, h_ref[...]))
            wr_ref[rows, :] = _dot_nt(pb, g_hi) + _dot_nt(pb, g_lo)
            yg_ref[rows, :] = jnp.zeros((MOE_ROWS, yg_ref.shape[1]), F32)
            return carry

        lax.fori_loop(0, nblk, compact, 0)

    def expert(blk, carry):
        rows = pl.ds(pl.multiple_of(blk * MOE_ROWS, MOE_ROWS), MOE_ROWS)
        xb = xg_ref[rows, :]
        hg = _dot(xb, wg_ref[0, 0])
        hu = _dot(xb, wu_ref[0, 0])
        lane8 = lax.broadcasted_iota(jnp.int32, (MOE_ROWS, sub8), 1)
        wcol = jnp.sum(jnp.where(lane8 == j, wr_ref[rows, :], 0.0), axis=1, keepdims=True)
        he = hg * jax.nn.sigmoid(hg) * hu * wcol
        yg_ref[rows, :] += _dot(_bf(he), wd_ref[0, 0])
        return carry

    lax.fori_loop(0, nblk_ref[0], expert, 0)

    @pl.when(j == EXPERTS_PER_GROUP - 1)
    def _():
        def spread(blk, carry):
            rows = pl.ds(pl.multiple_of(blk * MOE_ROWS, MOE_ROWS), MOE_ROWS)
            acc_ref[...] += _dot(pt_ref[blk], _bf(yg_ref[rows, :]))
            return carry

        lax.fori_loop(0, nblk_ref[0], spread, 0)

    @pl.when(e == n_exp - 1)
    def _():
        out = x_ref[0] + mod_ref[0][5:6] * acc_ref[...]
        if final:
            out = out * lax.rsqrt(jnp.mean(out * out, axis=-1, keepdims=True) + RMS_EPS) * vec_ref[1:2]
        o_ref[0] = out


def _moe_call(x, mod, per_batch, vec, rwt, rb, wg, wu, wd, layer, final):
    b, t, d = x.shape
    tm = min(1024, t)
    _, n_exp, _, de = wg.shape
    mod_map = (lambda i, j, e: (i, 0, 0)) if per_batch else (lambda i, j, e: (0, 0, 0))
    tile = pl.BlockSpec((1, tm, d), lambda i, j, e: (i, j, 0))
    cst = lambda shape: pl.BlockSpec(shape, lambda i, j, e: (0,) * len(shape))
    return pl.pallas_call(
        functools.partial(_moe_kernel, n_exp=n_exp, final=final),
        grid=(b, t // tm, n_exp),
        in_specs=[tile, pl.BlockSpec((1, 6, d), mod_map), cst(vec.shape), cst(rwt.shape), cst(rb.shape),
                  pl.BlockSpec((1, 1, d, de), lambda i, j, e: (layer, e, 0, 0)),
                  pl.BlockSpec((1, 1, d, de), lambda i, j, e: (layer, e, 0, 0)),
                  pl.BlockSpec((1, 1, de, d), lambda i, j, e: (layer, e, 0, 0))],
        out_specs=tile,
        out_shape=jax.ShapeDtypeStruct((b, t, d), F32),
        scratch_shapes=[
            pltpu.VMEM((tm, d), BF16),
            pltpu.VMEM((N_EXPERT_GROUPS, 8, tm), F32),
            pltpu.VMEM((8, tm), F32),
            pltpu.VMEM((tm, LANE), F32),
            pltpu.VMEM((tm, tm), BF16),
            pltpu.VMEM((tm // MOE_ROWS, tm, MOE_ROWS), BF16),
            pltpu.VMEM((tm, d), BF16),
            pltpu.VMEM((tm, d), F32),
            pltpu.VMEM((tm, 8), F32),
            pltpu.VMEM((tm, d), F32),
            pltpu.SMEM((1,), jnp.int32),
        ],
        compiler_params=_params(("arbitrary", "arbitrary", "arbitrary")),
        name="grouped_moe",
    )(x, mod, vec, rwt, rb, wg, wu, wd)


def _pad_rows(a, rows):
    return jnp.concatenate([a, jnp.zeros((rows - a.shape[0],) + a.shape[1:], a.dtype)], axis=0)


def _lane_stacked_state(s0):
    b, two, h, n, _ = s0.shape
    return jnp.swapaxes(s0, 2, 3).reshape(b, two, n, h * n)


def kernel(x_prompt, x_sample, state_rwkv, c, c_ctx, norm_g, ada_w, ada_b, final_g, rwkv_mu, rwkv_w_rkv, rwkv_w_o, rwkv_w0, rwkv_w1, rwkv_w2, rwkv_a0, rwkv_a1, rwkv_a2, rwkv_g1, rwkv_g2, rwkv_k_k, rwkv_k_a, rwkv_r_k, rwkv_gn_g, rwkv_gn_b, conv_pw1, conv_pw1_b, conv_dw, conv_dw_b, conv_ln_g, conv_ln_b, conv_pw2, conv_pw2_b, router_w, router_b, moe_w_gate, moe_w_up, moe_w_down):
    d = x_prompt.shape[-1]
    depth = ada_w.shape[0]
    n_heads = d // HEAD_DIM
    dec_b = c.shape[0]

    cond = _pad_rows(jnp.concatenate([c_ctx[None, :], c], axis=0), COND_ROWS)
    mod = _mod_call(cond, ada_w, ada_b).reshape(depth, COND_ROWS, 6, d)

    head_of_lane = jnp.arange(d) // HEAD_DIM
    hsel = (head_of_lane[:, None] == jnp.arange(LANE)[None, :]).astype(BF16)
    hselt = hsel.T
    rwt = router_w.T
    rb = router_b[:, None]
    wg = _bf(moe_w_gate)
    wu = _bf(moe_w_up)
    wd = _bf(moe_w_down)

    def lora_pad(w2):
        z = jnp.zeros_like(w2[0])
        return _bf(jnp.stack([jnp.concatenate([w2[0], z], 0), jnp.concatenate([z, w2[1]], 0)]))

    groups = (
        dict(x=x_prompt, rows=slice(0, 1), per_batch=False, s0=None, want_state=True, seg=x_prompt.shape[1]),
        dict(x=x_sample, rows=slice(1, 1 + dec_b), per_batch=True, s0=state_rwkv, want_state=False, seg=GRID_W),
    )
    outs = []
    new_state = None
    for gr in groups:
        x = gr["x"]
        states = []
        for i in range(depth):
            m_i = mod[i, gr["rows"]]
            j = i // 2
            if i % 2 == 0:
                vec = jnp.stack([norm_g[i, 0], rwkv_k_k[j], rwkv_k_a[j], rwkv_r_k[j].reshape(d),
                                 rwkv_w0[j, 0], rwkv_w0[j, 1], rwkv_a0[j, 0], rwkv_a0[j, 1]])
                r, v, kkn, logd, kd, bd, gate, bonus = _rwkv_pre_call(
                    x, m_i, gr["per_batch"], vec, rwkv_mu[j], _bf(rwkv_w_rkv[j]), _bf(rwkv_g1[j]), _bf(rwkv_g2[j]),
                    _bf(jnp.concatenate([rwkv_w1[j, 0], rwkv_w1[j, 1]], axis=1)), lora_pad(rwkv_w2[j]),
                    _bf(jnp.concatenate([rwkv_a1[j, 0], rwkv_a1[j, 1]], axis=1)), lora_pad(rwkv_a2[j]),
                    hsel, hselt)
                s0_bd = None if gr["s0"] is None else _lane_stacked_state(gr["s0"][:, j])
                res = _wkv_call(r, v, kkn, logd, kd, bd, s0_bd, gr["want_state"])
                if gr["want_state"]:
                    states.append(res[2])
                vec = _pad_rows(jnp.stack([rwkv_gn_g[j], rwkv_gn_b[j]]), 8)
                x = _rwkv_post_call(x, res[0], res[1], bonus, gate, m_i, gr["per_batch"], vec,
                                    _bf(rwkv_w_o[j]), hsel, hselt)
            else:
                vec = _pad_rows(jnp.stack([norm_g[i, 0], conv_dw_b[j], conv_ln_g[j], conv_ln_b[j], conv_pw2_b[j]]), 8)
                dw = conv_dw[j].reshape(CONV_WIDTH, d // LANE, 1, LANE)
                x = _conv_call(x, m_i, gr["per_batch"], gr["seg"], vec, _bf(conv_pw1[j]),
                               conv_pw1_b[j][None, :], dw, _bf(conv_pw2[j]))
            vec = _pad_rows(jnp.stack([norm_g[i, 1], final_g]), 8)
            xm = x if gr["per_batch"] else x.reshape(1, -1, d)
            xm = _moe_call(xm, m_i, gr["per_batch"], vec, rwt, rb, wg, wu, wd, i, final=(i == depth - 1))
            x = xm.reshape(x.shape)
        outs.append(x)
        if gr["want_state"]:
            new_state = jnp.stack(states, axis=1)
    return (outs[0], outs[1], new_state)
```

```python
import functools

import jax
import jax.numpy as jnp
from jax import lax
from jax.experimental import pallas as pl
from jax.experimental.pallas import tpu as pltpu

F32 = jnp.float32
BF16 = jnp.bfloat16

HEAD_DIM = 64
HEADS_PER_BLOCK = 2
BLOCK_LANES = HEAD_DIM * HEADS_PER_BLOCK
CHUNK = 64
WKV_BLOCKS_PER_STEP = 8
GRID_W = 64
MOE_ROWS = 256
CONV_WIDTH = 31
CONV_PAD = 16
N_EXPERT_GROUPS = 4
EXPERTS_PER_GROUP = 4
LANE = 128
COND_ROWS = 16
RMS_EPS = 1e-6
LN_EPS = 1e-5
GN_EPS = 64e-5
NEG_EXP_M_HALF = -0.6065306597126334
VMEM_LIMIT_BYTES = 56 * 1024 * 1024


def _bf(x):
    return x.astype(BF16)


def _dot(a, b):
    return jnp.dot(a, b, preferred_element_type=F32)


def _dot_nt(a, b):
    return lax.dot_general(a, b, (((1,), (1,)), ((), ())), preferred_element_type=F32)


def _split(x):
    hi = _bf(x)
    lo = _bf(x - hi.astype(F32))
    return hi, lo


def _dot_split(a, b_exact):
    hi, lo = _split(a)
    return _dot(hi, b_exact) + _dot(lo, b_exact)


def _rms_mod(x, g, shift, scale):
    y = x * lax.rsqrt(jnp.mean(x * x, axis=-1, keepdims=True) + RMS_EPS) * g
    return y * (1.0 + scale) + shift


def _head_sum(z, hsel, hselt):
    return _dot_split(_dot(_bf(z), hsel), hselt)


def _params(sem):
    return pltpu.CompilerParams(dimension_semantics=sem, vmem_limit_bytes=VMEM_LIMIT_BYTES)


def _const_spec(shape):
    nd = len(shape)
    return pl.BlockSpec(shape, lambda *_: (0,) * nd, pipeline_mode=pl.Buffered(1))


def _mod_kernel(c_ref, w_ref, b_ref, o_ref):
    c = c_ref[...]
    s = c * jax.nn.sigmoid(c)
    o_ref[0] = jnp.dot(s, w_ref[0], preferred_element_type=F32,
                       precision=lax.Precision.HIGHEST) + b_ref[0]


def _mod_call(cond, ada_w, ada_b):
    depth, d, n = ada_w.shape
    tn = 1536
    return pl.pallas_call(
        _mod_kernel,
        grid=(depth, n // tn),
        in_specs=[
            pl.BlockSpec((COND_ROWS, d), lambda l, j: (0, 0)),
            pl.BlockSpec((1, d, tn), lambda l, j: (l, 0, j)),
            pl.BlockSpec((1, 1, tn), lambda l, j: (l, 0, j)),
        ],
        out_specs=pl.BlockSpec((1, COND_ROWS, tn), lambda l, j: (l, 0, j)),
        out_shape=jax.ShapeDtypeStruct((depth, COND_ROWS, n), F32),
        compiler_params=_params(("arbitrary", "arbitrary")),
        name="adaln_mod",
    )(cond, ada_w, ada_b.reshape(depth, 1, n))


def _rwkv_pre_kernel(x_ref, xp_ref, xn_ref, mod_ref, vec_ref, mu_ref, wrkv_ref, g1_ref, g2_ref,
                     w1_ref, w2_ref, a1_ref, a2_ref, hsel_ref, hselt_ref,
                     r_ref, v_ref, kkn_ref, logd_ref, kd_ref, bd_ref, gate_ref, bonus_ref,
                     *, tm, nt):
    t = pl.program_id(1)
    mod = mod_ref[0]
    shift, scale = mod[0:1], mod[1:2]
    vec = vec_ref[...]
    g, k_k, k_a, r_k = vec[0:1], vec[1:2], vec[2:3], vec[3:4]
    hsel = hsel_ref[...]
    hselt = hselt_ref[...]

    h = _rms_mod(x_ref[0], g, shift, scale)
    h_prev = _rms_mod(xp_ref[0], g, shift, scale)[7:8]
    h_next = _rms_mod(xn_ref[0], g, shift, scale)[0:1]
    h_prev = jnp.where(t == 0, 0.0, h_prev)
    h_next = jnp.where(t == nt - 1, 0.0, h_next)
    row = lax.broadcasted_iota(jnp.int32, (tm, 1), 0)
    prev = jnp.where(row == 0, h_prev, pltpu.roll(h, 1, 0))
    nxt = jnp.where(row == tm - 1, h_next, pltpu.roll(h, tm - 1, 0))
    xx = 0.5 * (prev + nxt) - h
    mu = mu_ref[...]

    def mix(i):
        return _bf(h + xx * mu[i:i + 1])

    r = _dot(mix(0), wrkv_ref[0])
    k = _dot(mix(2), wrkv_ref[1])
    v = _dot(mix(3), wrkv_ref[2])
    gate = _dot(_bf(jax.nn.sigmoid(_dot(mix(5), g1_ref[...]))), g2_ref[...])
    tw = _bf(jnp.tanh(_dot(mix(1), w1_ref[...])))
    ta = _bf(_dot(mix(4), a1_ref[...]))

    kk = k * k_k
    kkn = kk * lax.rsqrt(jnp.maximum(_head_sum(kk * kk, hsel, hselt), 1e-24))
    r_ref[0] = _bf(r)
    v_ref[0] = _bf(v)
    kkn_ref[0] = _bf(kkn)
    gate_ref[0] = _bf(gate)

    ksum = None
    for d in range(2):
        u = vec[4 + d:5 + d] + _dot(tw, w2_ref[d])
        logd_ref[d, 0] = NEG_EXP_M_HALF * jax.nn.sigmoid(u)
        a = jax.nn.sigmoid(vec[6 + d:7 + d] + _dot(ta, a2_ref[d]))
        kd = k * (1.0 + (a - 1.0) * k_a)
        kd_ref[d, 0] = _bf(kd)
        bd_ref[d, 0] = _bf(kkn * a)
        ksum = kd if ksum is None else ksum + kd
    bonus_ref[0] = _bf(_head_sum(r * r_k * ksum, hsel, hselt) * v)


def _rwkv_pre_call(x, mod, per_batch, vec, mu, wrkv, g1, g2, w1c, w2p, a1c, a2p, hsel, hselt):
    b, t, d = x.shape
    tm = min(512, t)
    nt = t // tm
    r8 = tm // 8
    nb8 = t // 8
    mod_map = (lambda i, j: (i, 0, 0)) if per_batch else (lambda i, j: (0, 0, 0))
    tile = pl.BlockSpec((1, tm, d), lambda i, j: (i, j, 0))
    tile2 = pl.BlockSpec((2, 1, tm, d), lambda i, j: (0, i, j, 0))
    one = jax.ShapeDtypeStruct((b, t, d), BF16)
    two = jax.ShapeDtypeStruct((2, b, t, d), BF16)
    two_f32 = jax.ShapeDtypeStruct((2, b, t, d), F32)
    return pl.pallas_call(
        functools.partial(_rwkv_pre_kernel, tm=tm, nt=nt),
        grid=(b, nt),
        in_specs=[
            tile,
            pl.BlockSpec((1, 8, d), lambda i, j: (i, jnp.maximum(j * r8 - 1, 0), 0)),
            pl.BlockSpec((1, 8, d), lambda i, j: (i, jnp.minimum((j + 1) * r8, nb8 - 1), 0)),
            pl.BlockSpec((1, 6, d), mod_map),
            _const_spec(vec.shape), _const_spec(mu.shape), _const_spec(wrkv.shape),
            _const_spec(g1.shape), _const_spec(g2.shape), _const_spec(w1c.shape),
            _const_spec(w2p.shape), _const_spec(a1c.shape), _const_spec(a2p.shape),
            _const_spec(hsel.shape), _const_spec(hselt.shape),
        ],
        out_specs=[tile, tile, tile, tile2, tile2, tile2, tile, tile],
        out_shape=[one, one, one, two_f32, two, two, one, one],
        compiler_params=_params(("arbitrary", "arbitrary")),
        name="rwkv_pre",
    )(x, x, x, mod, vec, mu, wrkv, g1, g2, w1c, w2p, a1c, a2p, hsel, hselt)


def _wkv_chains(chains, bd_mask):
    c = CHUNK
    n = BLOCK_LANES
    idx = range(len(chains))
    rev = [ch["reverse"] for ch in chains]
    ri = lax.broadcasted_iota(jnp.int32, (c, c), 0)
    ci = lax.broadcasted_iota(jnp.int32, (c, c), 1)
    lmat = {False: _bf(jnp.where(ri >= ci, 1.0, 0.0)), True: _bf(jnp.where(ri <= ci, 1.0, 0.0))}
    row_t = lax.broadcasted_iota(jnp.int32, (c, n), 0)
    col_s = lax.broadcasted_iota(jnp.int32, (c, n), 1) % c
    strict = {False: row_t > col_s, True: row_t < col_s}
    incl = {False: row_t >= col_s, True: row_t <= col_s}
    zeros = jnp.zeros((c, n), F32)

    def blockdiag(z):
        return jnp.concatenate([_bf(z)] * HEADS_PER_BLOCK, axis=0) * bd_mask

    logd = [ch["logd"] for ch in chains]
    split = [_split(x) for x in logd]
    cum = [_dot(lmat[rev[i]], split[i][0]) + _dot(lmat[rev[i]], split[i][1]) for i in idx]
    tot = [cum[i][0:1] if rev[i] else cum[i][c - 1:c] for i in idx]
    e_in = [jnp.exp(cum[i]) for i in idx]
    e_ex = [jnp.exp(cum[i] - logd[i]) for i in idx]
    e_inv = [jnp.exp(-cum[i]) for i in idx]
    e_bar = [jnp.exp(tot[i] - cum[i]) for i in idx]
    p_c = [jnp.exp(tot[i]) for i in idx]

    a_t = [-chains[i]["kkn"] * e_ex[i] for i in idx]
    r_t = [chains[i]["r"] * e_in[i] for i in idx]
    b_t = [chains[i]["bd"] * e_inv[i] for i in idx]
    k_t = [chains[i]["kd"] * e_inv[i] for i in idx]
    b_bar = [chains[i]["bd"] * e_bar[i] for i in idx]
    k_bar = [chains[i]["kd"] * e_bar[i] for i in idx]

    ycat = [jnp.concatenate([blockdiag(b_t[i]), blockdiag(k_t[i])], axis=0) for i in idx]
    g = [_dot_nt(_bf(jnp.concatenate([a_t[i], r_t[i]], axis=0)), ycat[i]) for i in idx]
    a_m = [jnp.where(strict[rev[i]], g[i][:c, :n], 0.0) for i in idx]
    ak_b = [_bf(jnp.where(strict[rev[i]], g[i][:c, n:], 0.0)) for i in idx]
    rb_b = [_bf(jnp.where(incl[rev[i]], g[i][c:, :n], 0.0)) for i in idx]
    rk_b = [_bf(jnp.where(incl[rev[i]], g[i][c:, n:], 0.0)) for i in idx]

    vst = [blockdiag(chains[i]["v"]) for i in idx]
    z = [_dot(ak_b[i], vst[i]) for i in idx]

    nm = a_m
    q = [_dot(_bf(a_m[i]), blockdiag(a_m[i])) for i in idx]
    levels = CHUNK.bit_length() - 2
    for j in range(levels):
        qb = [_bf(q[i]) for i in idx]
        nm = [nm[i] + q[i] + _dot(qb[i], blockdiag(nm[i])) for i in idx]
        if j + 1 < levels:
            q = [_dot(qb[i], blockdiag(q[i])) for i in idx]

    t_az = [jnp.concatenate([a_t[i], z[i]], axis=1)
            + _dot(_bf(nm[i]), jnp.concatenate([blockdiag(a_t[i]), blockdiag(z[i])], axis=1)) for i in idx]
    a_hat = [t_az[i][:, :n] for i in idx]
    tz = [t_az[i][:, n:] for i in idx]
    r_hat = [r_t[i] + _dot(rb_b[i], blockdiag(a_hat[i])) for i in idx]
    y0 = [_dot(rb_b[i], blockdiag(tz[i])) + _dot(rk_b[i], vst[i]) for i in idx]

    atz_t = [_bf(jnp.transpose(jnp.concatenate([a_hat[i], tz[i]], axis=0))) for i in idx]
    m_t = [_bf(_dot(atz_t[i], _bf(jnp.concatenate([b_bar[i], zeros], axis=0)))) * bd_mask for i in idx]
    k_rows = [jnp.concatenate([k_bar[i], zeros] if chains[i]["v_first"] else [zeros, k_bar[i]], axis=0) for i in idx]
    bd_mask_f = bd_mask.astype(F32)
    n0_bd = [(_dot(atz_t[i], _bf(jnp.concatenate([zeros, b_bar[i]], axis=0)))
              + _dot(_bf(chains[i]["vt"]), _bf(k_rows[i]))) * bd_mask_f for i in idx]
    n0 = [sum(x[hb * c:(hb + 1) * c] for hb in range(1, HEADS_PER_BLOCK)) + x[0:c] for x in n0_bd]

    st = [chains[i]["state"] for i in idx]
    y = [_dot_nt(_bf(r_hat[i]), blockdiag(st[i])) + y0[i] for i in idx]
    new_state = [st[i] * p_c[i] + _dot(_bf(st[i]), m_t[i]) + n0[i] for i in idx]
    return list(zip(y, new_state))


def _wkv_kernel(*refs, nc, has_s0, want_state):
    (rf, vf, af, ldf, kf, bf_, rb, vb, ab, ldb, kb, bb) = refs[:12]
    pos = 12
    s0_ref = None
    if has_s0:
        s0_ref = refs[pos]
        pos += 1
    yf_ref, yb_ref = refs[pos], refs[pos + 1]
    pos += 2
    so_ref = None
    if want_state:
        so_ref = refs[pos]
        pos += 1
    st_ref = refs[pos]
    ci = pl.program_id(2)
    n = BLOCK_LANES

    @pl.when(ci == 0)
    def _():
        if has_s0:
            st_ref[...] = s0_ref[0]
        else:
            st_ref[...] = jnp.zeros(st_ref.shape, F32)

    rowb = lax.broadcasted_iota(jnp.int32, (n, n), 0) // HEAD_DIM
    colb = lax.broadcasted_iota(jnp.int32, (n, n), 1) // HEAD_DIM
    bd_mask = _bf(jnp.where(rowb == colb, 1.0, 0.0))

    chains = []
    for qi in range(WKV_BLOCKS_PER_STEP):
        ln = slice(qi * n, (qi + 1) * n)
        f32 = lambda ref, *ix: ref[ix + (slice(None), ln)].astype(F32)
        v_f = f32(vf, 0)
        v_b = f32(vb, 0)
        vt = jnp.transpose(jnp.concatenate([v_f, v_b], axis=0))
        chains.append(dict(r=f32(rf, 0), v=v_f, kkn=f32(af, 0), logd=ldf[0, 0, :, ln], kd=f32(kf, 0, 0),
                           bd=f32(bf_, 0, 0), vt=vt, v_first=True, state=st_ref[0, :, ln], reverse=False))
        chains.append(dict(r=f32(rb, 0), v=v_b, kkn=f32(ab, 0), logd=ldb[0, 0, :, ln], kd=f32(kb, 0, 0),
                           bd=f32(bb, 0, 0), vt=vt, v_first=False, state=st_ref[1, :, ln], reverse=True))
    res = _wkv_chains(chains, bd_mask)
    finals = []
    for qi in range(WKV_BLOCKS_PER_STEP):
        ln = slice(qi * n, (qi + 1) * n)
        (y_f, s_f), (y_b, s_b) = res[2 * qi], res[2 * qi + 1]
        yf_ref[0, :, ln] = y_f
        yb_ref[0, :, ln] = y_b
        st_ref[0, :, ln] = s_f
        st_ref[1, :, ln] = s_b
        finals.append((s_f, s_b))

    if want_state:
        @pl.when(ci == nc - 1)
        def _():
            for qi, pair in enumerate(finals):
                for d, s in enumerate(pair):
                    for i in range(HEADS_PER_BLOCK):
                        so_ref[0, d, qi * HEADS_PER_BLOCK + i] = s[:, i * HEAD_DIM:(i + 1) * HEAD_DIM]


def _wkv_call(r, v, kkn, logd, kd, bd, s0_bd, want_state):
    b, t, d = r.shape
    c = CHUNK
    nc = t // c
    n = BLOCK_LANES * WKV_BLOCKS_PER_STEP
    nq = d // n
    fwd = pl.BlockSpec((1, c, n), lambda i, q, j: (i, j, q))
    bwd = pl.BlockSpec((1, c, n), lambda i, q, j: (i, nc - 1 - j, q))
    fwd2 = pl.BlockSpec((1, 1, c, n), lambda i, q, j: (0, i, j, q))
    bwd2 = pl.BlockSpec((1, 1, c, n), lambda i, q, j: (1, i, nc - 1 - j, q))
    in_specs = [fwd, fwd, fwd, fwd2, fwd2, fwd2, bwd, bwd, bwd, bwd2, bwd2, bwd2]
    args = [r, v, kkn, logd, kd, bd, r, v, kkn, logd, kd, bd]
    has_s0 = s0_bd is not None
    if has_s0:
        in_specs.append(pl.BlockSpec((1, 2, HEAD_DIM, n), lambda i, q, j: (i, 0, 0, q)))
        args.append(s0_bd)
    out_specs = [fwd, bwd]
    out_shape = [jax.ShapeDtypeStruct((b, t, d), F32), jax.ShapeDtypeStruct((b, t, d), F32)]
    if want_state:
        out_specs.append(pl.BlockSpec((1, 2, n // HEAD_DIM, HEAD_DIM, HEAD_DIM),
                                      lambda i, q, j: (i, 0, q, 0, 0)))
        out_shape.append(jax.ShapeDtypeStruct((b, 2, d // HEAD_DIM, HEAD_DIM, HEAD_DIM), F32))
    return pl.pallas_call(
        functools.partial(_wkv_kernel, nc=nc, has_s0=has_s0, want_state=want_state),
        grid=(b, nq, nc),
        in_specs=in_specs,
        out_specs=out_specs,
        out_shape=out_shape,
        scratch_shapes=[pltpu.VMEM((2, HEAD_DIM, n), F32)],
        compiler_params=_params(("arbitrary", "arbitrary", "arbitrary")),
        name="wkv_scan",
    )(*args)


def _rwkv_post_kernel(x_ref, yf_ref, yb_ref, bonus_ref, gate_ref, mod_ref, vec_ref, wo_ref,
                      hsel_ref, hselt_ref, o_ref):
    hsel = hsel_ref[...]
    hselt = hselt_ref[...]
    vec = vec_ref[...]
    gn_g, gn_b = vec[0:1], vec[1:2]
    gt = mod_ref[0][2:3]
    y = yf_ref[0] + yb_ref[0]
    inv = 1.0 / HEAD_DIM
    mean = _head_sum(y, hsel, hselt) * inv
    yc = y - mean
    var = _head_sum(yc * yc, hsel, hselt) * inv
    yn = yc * lax.rsqrt(var + GN_EPS) * gn_g + gn_b + bonus_ref[0].astype(F32)
    out = _dot(_bf(yn * gate_ref[0].astype(F32)), wo_ref[...])
    o_ref[0] = x_ref[0] + gt * out


def _rwkv_post_call(x, yf, yb, bonus, gate, mod, per_batch, vec, wo, hsel, hselt):
    b, t, d = x.shape
    tm = min(512, t)
    mod_map = (lambda i, j: (i, 0, 0)) if per_batch else (lambda i, j: (0, 0, 0))
    tile = pl.BlockSpec((1, tm, d), lambda i, j: (i, j, 0))
    return pl.pallas_call(
        _rwkv_post_kernel,
        grid=(b, t // tm),
        in_specs=[tile, tile, tile, tile, tile, pl.BlockSpec((1, 6, d), mod_map),
                  _const_spec(vec.shape), _const_spec(wo.shape),
                  _const_spec(hsel.shape), _const_spec(hselt.shape)],
        out_specs=tile,
        out_shape=jax.ShapeDtypeStruct((b, t, d), F32),
        compiler_params=_params(("arbitrary", "arbitrary")),
        name="rwkv_post",
    )(x, yf, yb, bonus, gate, mod, vec, wo, hsel, hselt)


def _conv_kernel(x_ref, mod_ref, vec_ref, pw1_ref, pw1b_ref, dw_ref, pw2_ref, o_ref,
                 pad_ref, cv_ref, *, tm, seg):
    d = x_ref.shape[-1]
    nlb = d // LANE
    nseg = tm // seg
    mod = mod_ref[0]
    shift, scale, gt = mod[0:1], mod[1:2], mod[2:3]
    vec = vec_ref[...]
    g, dw_b, ln_g, ln_b, pw2_b = vec[0:1], vec[1:2], vec[2:3], vec[3:4], vec[4:5]
    x = x_ref[0]
    h = _rms_mod(x, g, shift, scale)
    u = _dot(_bf(h), pw1_ref[...]) + pw1b_ref[...]
    u = u[:, :d] * jax.nn.sigmoid(u[:, d:])

    zpad = jnp.zeros((nseg, CONV_PAD, LANE), F32)
    for lb in range(nlb):
        pad_ref[:, lb, 0:CONV_PAD, :] = zpad
        pad_ref[:, lb, CONV_PAD + seg:2 * CONV_PAD + seg, :] = zpad
        pad_ref[:, lb, CONV_PAD:CONV_PAD + seg, :] = u[:, lb * LANE:(lb + 1) * LANE].reshape(nseg, seg, LANE)

    base = CONV_PAD - CONV_WIDTH // 2

    def body(idx, carry):
        s = idx // nlb
        lb = idx % nlb
        acc = jnp.zeros((seg, LANE), F32)
        for j in range(CONV_WIDTH):
            acc = acc + pad_ref[s, lb, base + j:base + j + seg, :] * dw_ref[j, lb]
        cv_ref[s, lb] = acc
        return carry

    lax.fori_loop(0, nseg * nlb, body, 0)

    cv = jnp.concatenate([cv_ref[:, lb].reshape(tm, LANE) for lb in range(nlb)], axis=1) + dw_b
    mu = jnp.mean(cv, axis=-1, keepdims=True)
    cc = cv - mu
    var = jnp.mean(cc * cc, axis=-1, keepdims=True)
    z = cc * lax.rsqrt(var + LN_EPS) * ln_g + ln_b
    z = z * jax.nn.sigmoid(z)
    out = _dot(_bf(z), pw2_ref[...]) + pw2_b
    o_ref[0] = x + gt * out


def _conv_call(x, mod, per_batch, seg, vec, pw1, pw1b, dw, pw2):
    b, t, d = x.shape
    tm = min(512, t)
    seg = min(seg, tm)
    nlb = d // LANE
    mod_map = (lambda i, j: (i, 0, 0)) if per_batch else (lambda i, j: (0, 0, 0))
    tile = pl.BlockSpec((1, tm, d), lambda i, j: (i, j, 0))
    return pl.pallas_call(
        functools.partial(_conv_kernel, tm=tm, seg=seg),
        grid=(b, t // tm),
        in_specs=[tile, pl.BlockSpec((1, 6, d), mod_map), _const_spec(vec.shape),
                  _const_spec(pw1.shape), _const_spec(pw1b.shape), _const_spec(dw.shape),
                  _const_spec(pw2.shape)],
        out_specs=tile,
        out_shape=jax.ShapeDtypeStruct((b, t, d), F32),
        scratch_shapes=[pltpu.VMEM((tm // seg, nlb, seg + 2 * CONV_PAD, LANE), F32),
                        pltpu.VMEM((tm // seg, nlb, seg, LANE), F32)],
        compiler_params=_params(("arbitrary", "arbitrary")),
        name="conv_module",
    )(x, mod, vec, pw1, pw1b, dw, pw2)


def _router_gates_t(h, rwt_ref, rb_ref):
    h_hi, h_lo = _split(h)
    w_hi, w_lo = _split(rwt_ref[...])
    logits = _dot_nt(w_hi, h_hi) + _dot_nt(w_hi, h_lo) + _dot_nt(w_lo, h_hi) + rb_ref[...]
    m = jnp.max(logits, axis=0, keepdims=True)
    ex = jnp.exp(logits - m)
    p = ex / jnp.sum(ex, axis=0, keepdims=True)
    rows = [p[e:e + 1] for e in range(N_EXPERT_GROUPS * EXPERTS_PER_GROUP)]
    scores = []
    for gi in range(N_EXPERT_GROUPS):
        a, b, c, dd = rows[4 * gi:4 * gi + 4]
        hi1, lo1 = jnp.maximum(a, b), jnp.minimum(a, b)
        hi2, lo2 = jnp.maximum(c, dd), jnp.minimum(c, dd)
        scores.append(jnp.maximum(hi1, hi2) + jnp.maximum(jnp.minimum(hi1, hi2), jnp.maximum(lo1, lo2)))
    best = scores[0]
    sel = jnp.zeros(best.shape, jnp.int32)
    for gi in range(1, N_EXPERT_GROUPS):
        better = scores[gi] > best
        sel = jnp.where(better, gi, sel)
        best = jnp.where(better, scores[gi], best)
    qs = []
    for j in range(EXPERTS_PER_GROUP):
        qj = jnp.zeros(best.shape, F32)
        for gi in range(N_EXPERT_GROUPS):
            qj = jnp.where(sel == gi, rows[4 * gi + j], qj)
        qs.append(qj)

    def argmax4(vals):
        bv, bi = vals[0], jnp.zeros(best.shape, jnp.int32)
        for j in range(1, EXPERTS_PER_GROUP):
            better = vals[j] > bv
            bi = jnp.where(better, j, bi)
            bv = jnp.where(better, vals[j], bv)
        return bv, bi

    v1, i1 = argmax4(qs)
    v2, i2 = argmax4([jnp.where(i1 == j, -1.0, qs[j]) for j in range(EXPERTS_PER_GROUP)])
    den = v1 + v2
    w1, w2 = v1 / den, v2 / den
    gates = []
    for gi in range(N_EXPERT_GROUPS):
        rows_g = [jnp.where(sel == gi, jnp.where(i1 == j, w1, 0.0) + jnp.where(i2 == j, w2, 0.0), 0.0)
                  for j in range(EXPERTS_PER_GROUP)]
        gates.append(jnp.concatenate(rows_g, axis=0))
    return gates, sel


def _moe_kernel(x_ref, mod_ref, vec_ref, rwt_ref, rb_ref, wg_ref, wu_ref, wd_ref, o_ref,
                h_ref, gt_ref, rk_ref, rc_ref, early_ref, xg_ref, yg_ref, wr_ref, acc_ref, nblk_ref,
                *, n_exp, final):
    e = pl.program_id(2)
    tm = x_ref.shape[1]
    g = e // EXPERTS_PER_GROUP
    j = e % EXPERTS_PER_GROUP
    sub8 = 8

    @pl.when((pl.program_id(0) == 0) & (pl.program_id(1) == 0) & (e == 0))
    def _():
        early_ref[...] = _bf(jnp.where(lax.broadcasted_iota(jnp.int32, (tm, tm), 0)
                                       < lax.broadcasted_iota(jnp.int32, (tm, tm), 1), 1.0, 0.0))

    def block_rows(blk, size):
        return pl.ds(pl.multiple_of(blk * MOE_ROWS, MOE_ROWS), size)

    def block_base(blk):
        return (blk * MOE_ROWS).astype(F32)

    @pl.when(e == 0)
    def _():
        mod = mod_ref[0]
        h = _rms_mod(x_ref[0], vec_ref[0:1], mod[3:4], mod[4:5])
        h_ref[...] = _bf(h)
        gates, sel = _router_gates_t(h, rwt_ref, rb_ref)
        zero4 = jnp.zeros((sub8 - EXPERTS_PER_GROUP, tm), F32)
        for gi in range(N_EXPERT_GROUPS):
            gt_ref[gi] = jnp.concatenate([gates[gi], zero4], axis=0)
        chosen = [jnp.where(sel == gi, 1.0, 0.0) for gi in range(N_EXPERT_GROUPS)]
        selmat = _bf(jnp.concatenate(chosen + [jnp.zeros((sub8 - N_EXPERT_GROUPS, tm), F32)], axis=0))
        prefix = _dot(selmat, early_ref[...])
        rk = jnp.where(selmat > 0, prefix, -1.0)
        rk_ref[...] = rk
        rc_ref[...] = jnp.transpose(jnp.concatenate([rk, jnp.zeros((LANE - sub8, tm), F32)], axis=0))
        acc_ref[...] = jnp.zeros(acc_ref.shape, F32)

    @pl.when(j == 0)
    def _():
        sub_g = lax.broadcasted_iota(jnp.int32, (sub8, tm), 0)
        rank_row = jnp.sum(jnp.where(sub_g == g, rk_ref[...], 0.0), axis=0, keepdims=True)
        cnt = jnp.sum(jnp.where(rank_row >= 0.0, 1.0, 0.0)).astype(jnp.int32)
        nblk = (cnt + (MOE_ROWS - 1)) // MOE_ROWS
        nblk_ref[0] = nblk
        g_hi, g_lo = _split(gt_ref[g])
        slot = lax.broadcasted_iota(jnp.int32, (MOE_ROWS, tm), 0).astype(F32)

        def compact(blk, carry):
            rows = block_rows(blk, MOE_ROWS)
            pb = _bf(jnp.where(rank_row - block_base(blk) == slot, 1.0, 0.0))
            xg_ref[rows, :] = _bf(_dot(pb, h_ref[...]))
            wr_ref[rows, :] = _dot_nt(pb, g_hi) + _dot_nt(pb, g_lo)
            yg_ref[rows, :] = jnp.zeros((MOE_ROWS, yg_ref.shape[1]), F32)
            return carry

        lax.fori_loop(0, nblk, compact, 0)

    def expert(blk, size):
        rows = block_rows(blk, size)
        xb = xg_ref[rows, :]
        hg = _dot(xb, wg_ref[0, 0])
        hu = _dot(xb, wu_ref[0, 0])
        lane8 = lax.broadcasted_iota(jnp.int32, (size, sub8), 1)
        wcol = jnp.sum(jnp.where(lane8 == j, wr_ref[rows, :], 0.0), axis=1, keepdims=True)
        he = hg * jax.nn.sigmoid(hg) * hu * wcol
        yg_ref[rows, :] += _dot(_bf(he), wd_ref[0, 0])

    n_used = nblk_ref[0]
    n_pairs = lax.shift_right_logical(n_used, 1)

    def expert_pair(p, carry):
        expert(2 * p, 2 * MOE_ROWS)
        return carry

    lax.fori_loop(0, n_pairs, expert_pair, 0)

    @pl.when(n_used - 2 * n_pairs == 1)
    def _():
        expert(n_used - 1, MOE_ROWS)

    @pl.when(j == EXPERTS_PER_GROUP - 1)
    def _():
        lane_g = lax.broadcasted_iota(jnp.int32, (tm, LANE), 1)
        rank_col = jnp.sum(jnp.where(lane_g == g, rc_ref[...], 0.0), axis=1, keepdims=True)
        slot_l = lax.broadcasted_iota(jnp.int32, (tm, MOE_ROWS), 1).astype(F32)

        def spread(blk, carry):
            pt = _bf(jnp.where(rank_col - block_base(blk) == slot_l, 1.0, 0.0))
            acc_ref[...] += _dot(pt, _bf(yg_ref[block_rows(blk, MOE_ROWS), :]))
            return carry

        lax.fori_loop(0, n_used, spread, 0)

    @pl.when(e == n_exp - 1)
    def _():
        out = x_ref[0] + mod_ref[0][5:6] * acc_ref[...]
        if final:
            out = out * lax.rsqrt(jnp.mean(out * out, axis=-1, keepdims=True) + RMS_EPS) * vec_ref[1:2]
        o_ref[0] = out


def _moe_call(x, mod, per_batch, vec, rwt, rb, wg, wu, wd, layer, final):
    b, t, d = x.shape
    tm = min(1024, t)
    cap = -(-tm // MOE_ROWS) * MOE_ROWS
    _, n_exp, _, de = wg.shape
    mod_map = (lambda i, j, e: (i, 0, 0)) if per_batch else (lambda i, j, e: (0, 0, 0))
    tile = pl.BlockSpec((1, tm, d), lambda i, j, e: (i, j, 0))
    cst = lambda shape: pl.BlockSpec(shape, lambda i, j, e: (0,) * len(shape))
    return pl.pallas_call(
        functools.partial(_moe_kernel, n_exp=n_exp, final=final),
        grid=(b, t // tm, n_exp),
        in_specs=[tile, pl.BlockSpec((1, 6, d), mod_map), cst(vec.shape), cst(rwt.shape), cst(rb.shape),
                  pl.BlockSpec((1, 1, d, de), lambda i, j, e: (layer, e, 0, 0)),
                  pl.BlockSpec((1, 1, d, de), lambda i, j, e: (layer, e, 0, 0)),
                  pl.BlockSpec((1, 1, de, d), lambda i, j, e: (layer, e, 0, 0))],
        out_specs=tile,
        out_shape=jax.ShapeDtypeStruct((b, t, d), F32),
        scratch_shapes=[
            pltpu.VMEM((tm, d), BF16),
            pltpu.VMEM((N_EXPERT_GROUPS, 8, tm), F32),
            pltpu.VMEM((8, tm), F32),
            pltpu.VMEM((tm, LANE), F32),
            pltpu.VMEM((tm, tm), BF16),
            pltpu.VMEM((cap, d), BF16),
            pltpu.VMEM((cap, d), F32),
            pltpu.VMEM((cap, 8), F32),
            pltpu.VMEM((tm, d), F32),
            pltpu.SMEM((1,), jnp.int32),
        ],
        compiler_params=_params(("arbitrary", "arbitrary", "arbitrary")),
        name="grouped_moe",
    )(x, mod, vec, rwt, rb, wg, wu, wd)


def _pad_rows(a, rows):
    return jnp.concatenate([a, jnp.zeros((rows - a.shape[0],) + a.shape[1:], a.dtype)], axis=0)


def _lane_stacked_state(s0):
    b, two, h, n, _ = s0.shape
    return jnp.swapaxes(s0, 2, 3).reshape(b, two, n, h * n)


def kernel(x_prompt, x_sample, state_rwkv, c, c_ctx, norm_g, ada_w, ada_b, final_g, rwkv_mu, rwkv_w_rkv, rwkv_w_o, rwkv_w0, rwkv_w1, rwkv_w2, rwkv_a0, rwkv_a1, rwkv_a2, rwkv_g1, rwkv_g2, rwkv_k_k, rwkv_k_a, rwkv_r_k, rwkv_gn_g, rwkv_gn_b, conv_pw1, conv_pw1_b, conv_dw, conv_dw_b, conv_ln_g, conv_ln_b, conv_pw2, conv_pw2_b, router_w, router_b, moe_w_gate, moe_w_up, moe_w_down):
    d = x_prompt.shape[-1]
    depth = ada_w.shape[0]
    n_heads = d // HEAD_DIM
    dec_b = c.shape[0]

    cond = _pad_rows(jnp.concatenate([c_ctx[None, :], c], axis=0), COND_ROWS)
    mod = _mod_call(cond, ada_w, ada_b).reshape(depth, COND_ROWS, 6, d)

    head_of_lane = jnp.arange(d) // HEAD_DIM
    hsel = (head_of_lane[:, None] == jnp.arange(LANE)[None, :]).astype(BF16)
    hselt = hsel.T
    rwt = router_w.T
    rb = router_b[:, None]
    wg = _bf(moe_w_gate)
    wu = _bf(moe_w_up)
    wd = _bf(moe_w_down)

    def lora_pad(w2):
        z = jnp.zeros_like(w2[0])
        return _bf(jnp.stack([jnp.concatenate([w2[0], z], 0), jnp.concatenate([z, w2[1]], 0)]))

    groups = (
        dict(x=x_prompt, rows=slice(0, 1), per_batch=False, s0=None, want_state=True, seg=x_prompt.shape[1]),
        dict(x=x_sample, rows=slice(1, 1 + dec_b), per_batch=True, s0=state_rwkv, want_state=False, seg=GRID_W),
    )
    outs = []
    new_state = None
    for gr in groups:
        x = gr["x"]
        states = []
        for i in range(depth):
            m_i = mod[i, gr["rows"]]
            j = i // 2
            if i % 2 == 0:
                vec = jnp.stack([norm_g[i, 0], rwkv_k_k[j], rwkv_k_a[j], rwkv_r_k[j].reshape(d),
                                 rwkv_w0[j, 0], rwkv_w0[j, 1], rwkv_a0[j, 0], rwkv_a0[j, 1]])
                r, v, kkn, logd, kd, bd, gate, bonus = _rwkv_pre_call(
                    x, m_i, gr["per_batch"], vec, rwkv_mu[j], _bf(rwkv_w_rkv[j]), _bf(rwkv_g1[j]), _bf(rwkv_g2[j]),
                    _bf(jnp.concatenate([rwkv_w1[j, 0], rwkv_w1[j, 1]], axis=1)), lora_pad(rwkv_w2[j]),
                    _bf(jnp.concatenate([rwkv_a1[j, 0], rwkv_a1[j, 1]], axis=1)), lora_pad(rwkv_a2[j]),
                    hsel, hselt)
                s0_bd = None if gr["s0"] is None else _lane_stacked_state(gr["s0"][:, j])
                res = _wkv_call(r, v, kkn, logd, kd, bd, s0_bd, gr["want_state"])
                if gr["want_state"]:
                    states.append(res[2])
                vec = _pad_rows(jnp.stack([rwkv_gn_g[j], rwkv_gn_b[j]]), 8)
                x = _rwkv_post_call(x, res[0], res[1], bonus, gate, m_i, gr["per_batch"], vec,
                                    _bf(rwkv_w_o[j]), hsel, hselt)
            else:
                vec = _pad_rows(jnp.stack([norm_g[i, 0], conv_dw_b[j], conv_ln_g[j], conv_ln_b[j], conv_pw2_b[j]]), 8)
                dw = conv_dw[j].reshape(CONV_WIDTH, d // LANE, 1, LANE)
                x = _conv_call(x, m_i, gr["per_batch"], gr["seg"], vec, _bf(conv_pw1[j]),
                               conv_pw1_b[j][None, :], dw, _bf(conv_pw2[j]))
            vec = _pad_rows(jnp.stack([norm_g[i, 1], final_g]), 8)
            xm = x if gr["per_batch"] else x.reshape(1, -1, d)
            xm = _moe_call(xm, m_i, gr["per_batch"], vec, rwt, rb, wg, wu, wd, i, final=(i == depth - 1))
            x = xm.reshape(x.shape)
        outs.append(x)
        if gr["want_state"]:
            new_state = jnp.stack(states, axis=1)
    return (outs[0], outs[1], new_state)
```

```python
import functools

import jax
import jax.numpy as jnp
from jax import lax
from jax.experimental import pallas as pl
from jax.experimental.pallas import tpu as pltpu

F32 = jnp.float32
BF16 = jnp.bfloat16

HEAD_DIM = 64
HEADS_PER_BLOCK = 2
BLOCK_LANES = HEAD_DIM * HEADS_PER_BLOCK
CHUNK = 64
WKV_BLOCKS_PER_STEP = 8
GRID_W = 64
MOE_ROWS = 256
CONV_WIDTH = 31
CONV_PAD = 16
N_EXPERT_GROUPS = 4
EXPERTS_PER_GROUP = 4
LANE = 128
COND_ROWS = 16
RMS_EPS = 1e-6
LN_EPS = 1e-5
GN_EPS = 64e-5
NEG_EXP_M_HALF = -0.6065306597126334
VMEM_LIMIT_BYTES = 56 * 1024 * 1024


def _bf(x):
    return x.astype(BF16)


def _dot(a, b):
    return jnp.dot(a, b, preferred_element_type=F32)


def _dot_nt(a, b):
    return lax.dot_general(a, b, (((1,), (1,)), ((), ())), preferred_element_type=F32)


def _split(x):
    hi = _bf(x)
    lo = _bf(x - hi.astype(F32))
    return hi, lo


def _rms_mod(x, g, shift, scale):
    y = x * lax.rsqrt(jnp.mean(x * x, axis=-1, keepdims=True) + RMS_EPS) * g
    return y * (1.0 + scale) + shift


def _head_sum(z, hsel, hselt2):
    hi, lo = _split(_dot(_bf(z), hsel))
    return _dot(jnp.concatenate([hi, lo], axis=1), hselt2)


def _params(sem):
    return pltpu.CompilerParams(dimension_semantics=sem, vmem_limit_bytes=VMEM_LIMIT_BYTES)


def _const_spec(shape):
    nd = len(shape)
    return pl.BlockSpec(shape, lambda *_: (0,) * nd, pipeline_mode=pl.Buffered(1))


def _mod_kernel(c_ref, w_ref, b_ref, o_ref):
    c = c_ref[...]
    s = c * jax.nn.sigmoid(c)
    o_ref[0] = jnp.dot(s, w_ref[0], preferred_element_type=F32,
                       precision=lax.Precision.HIGHEST) + b_ref[0]


def _mod_call(cond, ada_w, ada_b):
    depth, d, n = ada_w.shape
    tn = 1536
    return pl.pallas_call(
        _mod_kernel,
        grid=(depth, n // tn),
        in_specs=[
            pl.BlockSpec((COND_ROWS, d), lambda l, j: (0, 0)),
            pl.BlockSpec((1, d, tn), lambda l, j: (l, 0, j)),
            pl.BlockSpec((1, 1, tn), lambda l, j: (l, 0, j)),
        ],
        out_specs=pl.BlockSpec((1, COND_ROWS, tn), lambda l, j: (l, 0, j)),
        out_shape=jax.ShapeDtypeStruct((depth, COND_ROWS, n), F32),
        compiler_params=_params(("arbitrary", "arbitrary")),
        name="adaln_mod",
    )(cond, ada_w, ada_b.reshape(depth, 1, n))


def _rwkv_pre_kernel(x_ref, xp_ref, xn_ref, mod_ref, vec_ref, mu_ref, wrkv_ref, g1_ref, g2_ref,
                     w1_ref, w2_ref, a1_ref, a2_ref, hsel_ref, hselt_ref,
                     r_ref, v_ref, kkn_ref, logd_ref, kd_ref, bd_ref, gate_ref, bonus_ref,
                     *, tm, nt):
    t = pl.program_id(1)
    mod = mod_ref[0]
    shift, scale = mod[0:1], mod[1:2]
    vec = vec_ref[...]
    g, k_k, k_a, r_k = vec[0:1], vec[1:2], vec[2:3], vec[3:4]
    hsel = hsel_ref[...]
    hselt = hselt_ref[...]

    h = _rms_mod(x_ref[0], g, shift, scale)
    h_prev = _rms_mod(xp_ref[0], g, shift, scale)[7:8]
    h_next = _rms_mod(xn_ref[0], g, shift, scale)[0:1]
    h_prev = jnp.where(t == 0, 0.0, h_prev)
    h_next = jnp.where(t == nt - 1, 0.0, h_next)
    row = lax.broadcasted_iota(jnp.int32, (tm, 1), 0)
    prev = jnp.where(row == 0, h_prev, pltpu.roll(h, 1, 0))
    nxt = jnp.where(row == tm - 1, h_next, pltpu.roll(h, tm - 1, 0))
    xx = 0.5 * (prev + nxt) - h
    mu = mu_ref[...]

    def mix(i):
        return _bf(h + xx * mu[i:i + 1])

    r = _dot(mix(0), wrkv_ref[0])
    k = _dot(mix(2), wrkv_ref[1])
    v = _dot(mix(3), wrkv_ref[2])
    gate = _dot(_bf(jax.nn.sigmoid(_dot(mix(5), g1_ref[...]))), g2_ref[...])
    tw = _bf(jnp.tanh(_dot(mix(1), w1_ref[...])))
    ta = _bf(_dot(mix(4), a1_ref[...]))

    kk = k * k_k
    kkn = kk * lax.rsqrt(jnp.maximum(_head_sum(kk * kk, hsel, hselt), 1e-24))
    r_ref[0] = _bf(r)
    v_ref[0] = _bf(v)
    kkn_ref[0] = _bf(kkn)
    gate_ref[0] = _bf(gate)

    ksum = None
    for d in range(2):
        u = vec[4 + d:5 + d] + _dot(tw, w2_ref[d])
        logd_ref[d, 0] = NEG_EXP_M_HALF * jax.nn.sigmoid(u)
        a = jax.nn.sigmoid(vec[6 + d:7 + d] + _dot(ta, a2_ref[d]))
        kd = k * (1.0 + (a - 1.0) * k_a)
        kd_ref[d, 0] = _bf(kd)
        bd_ref[d, 0] = _bf(kkn * a)
        ksum = kd if ksum is None else ksum + kd
    bonus_ref[0] = _bf(_head_sum(r * r_k * ksum, hsel, hselt) * v)


def _rwkv_pre_call(x, mod, per_batch, vec, mu, wrkv, g1, g2, w1c, w2p, a1c, a2p, hsel, hselt):
    b, t, d = x.shape
    tm = min(512, t)
    nt = t // tm
    r8 = tm // 8
    nb8 = t // 8
    mod_map = (lambda i, j: (i, 0, 0)) if per_batch else (lambda i, j: (0, 0, 0))
    tile = pl.BlockSpec((1, tm, d), lambda i, j: (i, j, 0))
    tile2 = pl.BlockSpec((2, 1, tm, d), lambda i, j: (0, i, j, 0))
    one = jax.ShapeDtypeStruct((b, t, d), BF16)
    two = jax.ShapeDtypeStruct((2, b, t, d), BF16)
    two_f32 = jax.ShapeDtypeStruct((2, b, t, d), F32)
    return pl.pallas_call(
        functools.partial(_rwkv_pre_kernel, tm=tm, nt=nt),
        grid=(b, nt),
        in_specs=[
            tile,
            pl.BlockSpec((1, 8, d), lambda i, j: (i, jnp.maximum(j * r8 - 1, 0), 0)),
            pl.BlockSpec((1, 8, d), lambda i, j: (i, jnp.minimum((j + 1) * r8, nb8 - 1), 0)),
            pl.BlockSpec((1, 6, d), mod_map),
            _const_spec(vec.shape), _const_spec(mu.shape), _const_spec(wrkv.shape),
            _const_spec(g1.shape), _const_spec(g2.shape), _const_spec(w1c.shape),
            _const_spec(w2p.shape), _const_spec(a1c.shape), _const_spec(a2p.shape),
            _const_spec(hsel.shape), _const_spec(hselt.shape),
        ],
        out_specs=[tile, tile, tile, tile2, tile2, tile2, tile, tile],
        out_shape=[one, one, one, two_f32, two, two, one, one],
        compiler_params=_params(("arbitrary", "arbitrary")),
        name="rwkv_pre",
    )(x, x, x, mod, vec, mu, wrkv, g1, g2, w1c, w2p, a1c, a2p, hsel, hselt)


def _wkv_chains(chains, bd_mask):
    c = CHUNK
    n = BLOCK_LANES
    idx = range(len(chains))
    rev = [ch["reverse"] for ch in chains]
    row_t = lax.broadcasted_iota(jnp.int32, (c, n), 0)
    col_s = lax.broadcasted_iota(jnp.int32, (c, n), 1) % c
    strict = {False: row_t > col_s, True: row_t < col_s}
    incl = {False: row_t >= col_s, True: row_t <= col_s}
    zeros = jnp.zeros((c, n), F32)

    def blockdiag(z):
        return jnp.concatenate([_bf(z)] * HEADS_PER_BLOCK, axis=0) * bd_mask

    def chunk_cumsum(x, reverse):
        k = 1
        while k < c:
            if reverse:
                x = x + jnp.where(row_t < c - k, pltpu.roll(x, c - k, 0), 0.0)
            else:
                x = x + jnp.where(row_t >= k, pltpu.roll(x, k, 0), 0.0)
            k *= 2
        return x

    logd = [ch["logd"] for ch in chains]
    cum = [chunk_cumsum(logd[i], rev[i]) for i in idx]
    tot = [cum[i][0:1] if rev[i] else cum[i][c - 1:c] for i in idx]
    e_in = [jnp.exp(cum[i]) for i in idx]
    e_ex = [jnp.exp(cum[i] - logd[i]) for i in idx]
    e_inv = [jnp.exp(-cum[i]) for i in idx]
    e_bar = [jnp.exp(tot[i] - cum[i]) for i in idx]
    p_c = [jnp.exp(tot[i]) for i in idx]

    a_t = [-chains[i]["kkn"] * e_ex[i] for i in idx]
    r_t = [chains[i]["r"] * e_in[i] for i in idx]
    b_t = [chains[i]["bd"] * e_inv[i] for i in idx]
    k_t = [chains[i]["kd"] * e_inv[i] for i in idx]
    b_bar = [chains[i]["bd"] * e_bar[i] for i in idx]
    k_bar = [chains[i]["kd"] * e_bar[i] for i in idx]

    ycat = [jnp.concatenate([blockdiag(b_t[i]), blockdiag(k_t[i])], axis=0) for i in idx]
    g = [_dot_nt(_bf(jnp.concatenate([a_t[i], r_t[i]], axis=0)), ycat[i]) for i in idx]
    a_m = [jnp.where(strict[rev[i]], g[i][:c, :n], 0.0) for i in idx]
    ak_b = [_bf(jnp.where(strict[rev[i]], g[i][:c, n:], 0.0)) for i in idx]
    rb_b = [_bf(jnp.where(incl[rev[i]], g[i][c:, :n], 0.0)) for i in idx]
    rk_b = [_bf(jnp.where(incl[rev[i]], g[i][c:, n:], 0.0)) for i in idx]

    vst = [blockdiag(chains[i]["v"]) for i in idx]
    z = [_dot(ak_b[i], vst[i]) for i in idx]

    nm = a_m
    q = [_dot(_bf(a_m[i]), blockdiag(a_m[i])) for i in idx]
    levels = CHUNK.bit_length() - 2
    for j in range(levels):
        qb = [_bf(q[i]) for i in idx]
        nm = [nm[i] + q[i] + _dot(qb[i], blockdiag(nm[i])) for i in idx]
        if j + 1 < levels:
            q = [_dot(qb[i], blockdiag(q[i])) for i in idx]

    t_az = [jnp.concatenate([a_t[i], z[i]], axis=1)
            + _dot(_bf(nm[i]), jnp.concatenate([blockdiag(a_t[i]), blockdiag(z[i])], axis=1)) for i in idx]
    a_hat = [t_az[i][:, :n] for i in idx]
    tz = [t_az[i][:, n:] for i in idx]
    r_hat = [r_t[i] + _dot(rb_b[i], blockdiag(a_hat[i])) for i in idx]
    y0 = [_dot(rb_b[i], blockdiag(tz[i])) + _dot(rk_b[i], vst[i]) for i in idx]

    atz_t = [_bf(jnp.transpose(jnp.concatenate([a_hat[i], tz[i]], axis=0))) for i in idx]
    m_t = [_bf(_dot(atz_t[i], _bf(jnp.concatenate([b_bar[i], zeros], axis=0)))) * bd_mask for i in idx]
    k_rows = [jnp.concatenate([k_bar[i], zeros] if chains[i]["v_first"] else [zeros, k_bar[i]], axis=0) for i in idx]
    bd_mask_f = bd_mask.astype(F32)
    n0_bd = [(_dot(atz_t[i], _bf(jnp.concatenate([zeros, b_bar[i]], axis=0)))
              + _dot(_bf(chains[i]["vt"]), _bf(k_rows[i]))) * bd_mask_f for i in idx]
    n0 = [sum(x[hb * c:(hb + 1) * c] for hb in range(1, HEADS_PER_BLOCK)) + x[0:c] for x in n0_bd]

    st = [chains[i]["state"] for i in idx]
    y = [_dot_nt(_bf(r_hat[i]), blockdiag(st[i])) + y0[i] for i in idx]
    new_state = [st[i] * p_c[i] + _dot(_bf(st[i]), m_t[i]) + n0[i] for i in idx]
    return list(zip(y, new_state))


def _wkv_kernel(*refs, nc, has_s0, want_state):
    (rf, vf, af, ldf, kf, bf_, rb, vb, ab, ldb, kb, bb) = refs[:12]
    pos = 12
    s0_ref = None
    if has_s0:
        s0_ref = refs[pos]
        pos += 1
    yf_ref, yb_ref = refs[pos], refs[pos + 1]
    pos += 2
    so_ref = None
    if want_state:
        so_ref = refs[pos]
        pos += 1
    st_ref = refs[pos]
    ci = pl.program_id(2)
    n = BLOCK_LANES

    @pl.when(ci == 0)
    def _():
        if has_s0:
            st_ref[...] = s0_ref[0]
        else:
            st_ref[...] = jnp.zeros(st_ref.shape, F32)

    rowb = lax.broadcasted_iota(jnp.int32, (n, n), 0) // HEAD_DIM
    colb = lax.broadcasted_iota(jnp.int32, (n, n), 1) // HEAD_DIM
    bd_mask = _bf(jnp.where(rowb == colb, 1.0, 0.0))

    chains = []
    for qi in range(WKV_BLOCKS_PER_STEP):
        ln = slice(qi * n, (qi + 1) * n)
        f32 = lambda ref, *ix: ref[ix + (slice(None), ln)].astype(F32)
        v_f = f32(vf, 0)
        v_b = f32(vb, 0)
        vt = jnp.transpose(jnp.concatenate([v_f, v_b], axis=0))
        chains.append(dict(r=f32(rf, 0), v=v_f, kkn=f32(af, 0), logd=ldf[0, 0, :, ln], kd=f32(kf, 0, 0),
                           bd=f32(bf_, 0, 0), vt=vt, v_first=True, state=st_ref[0, :, ln], reverse=False))
        chains.append(dict(r=f32(rb, 0), v=v_b, kkn=f32(ab, 0), logd=ldb[0, 0, :, ln], kd=f32(kb, 0, 0),
                           bd=f32(bb, 0, 0), vt=vt, v_first=False, state=st_ref[1, :, ln], reverse=True))
    res = _wkv_chains(chains, bd_mask)
    finals = []
    for qi in range(WKV_BLOCKS_PER_STEP):
        ln = slice(qi * n, (qi + 1) * n)
        (y_f, s_f), (y_b, s_b) = res[2 * qi], res[2 * qi + 1]
        yf_ref[0, :, ln] = y_f
        yb_ref[0, :, ln] = y_b
        st_ref[0, :, ln] = s_f
        st_ref[1, :, ln] = s_b
        finals.append((s_f, s_b))

    if want_state:
        @pl.when(ci == nc - 1)
        def _():
            for qi, pair in enumerate(finals):
                for d, s in enumerate(pair):
                    for i in range(HEADS_PER_BLOCK):
                        so_ref[0, d, qi * HEADS_PER_BLOCK + i] = s[:, i * HEAD_DIM:(i + 1) * HEAD_DIM]


def _wkv_call(r, v, kkn, logd, kd, bd, s0_bd, want_state):
    b, t, d = r.shape
    c = CHUNK
    nc = t // c
    n = BLOCK_LANES * WKV_BLOCKS_PER_STEP
    nq = d // n
    fwd = pl.BlockSpec((1, c, n), lambda i, q, j: (i, j, q))
    bwd = pl.BlockSpec((1, c, n), lambda i, q, j: (i, nc - 1 - j, q))
    fwd2 = pl.BlockSpec((1, 1, c, n), lambda i, q, j: (0, i, j, q))
    bwd2 = pl.BlockSpec((1, 1, c, n), lambda i, q, j: (1, i, nc - 1 - j, q))
    in_specs = [fwd, fwd, fwd, fwd2, fwd2, fwd2, bwd, bwd, bwd, bwd2, bwd2, bwd2]
    args = [r, v, kkn, logd, kd, bd, r, v, kkn, logd, kd, bd]
    has_s0 = s0_bd is not None
    if has_s0:
        in_specs.append(pl.BlockSpec((1, 2, HEAD_DIM, n), lambda i, q, j: (i, 0, 0, q)))
        args.append(s0_bd)
    out_specs = [fwd, bwd]
    out_shape = [jax.ShapeDtypeStruct((b, t, d), F32), jax.ShapeDtypeStruct((b, t, d), F32)]
    if want_state:
        out_specs.append(pl.BlockSpec((1, 2, n // HEAD_DIM, HEAD_DIM, HEAD_DIM),
                                      lambda i, q, j: (i, 0, q, 0, 0)))
        out_shape.append(jax.ShapeDtypeStruct((b, 2, d // HEAD_DIM, HEAD_DIM, HEAD_DIM), F32))
    return pl.pallas_call(
        functools.partial(_wkv_kernel, nc=nc, has_s0=has_s0, want_state=want_state),
        grid=(b, nq, nc),
        in_specs=in_specs,
        out_specs=out_specs,
        out_shape=out_shape,
        scratch_shapes=[pltpu.VMEM((2, HEAD_DIM, n), F32)],
        compiler_params=_params(("arbitrary", "arbitrary", "arbitrary")),
        name="wkv_scan",
    )(*args)


def _rwkv_post_kernel(x_ref, yf_ref, yb_ref, bonus_ref, gate_ref, mod_ref, vec_ref, wo_ref,
                      hsel_ref, hselt_ref, o_ref):
    hsel = hsel_ref[...]
    hselt = hselt_ref[...]
    vec = vec_ref[...]
    gn_g, gn_b = vec[0:1], vec[1:2]
    gt = mod_ref[0][2:3]
    y = yf_ref[0] + yb_ref[0]
    inv = 1.0 / HEAD_DIM
    mean = _head_sum(y, hsel, hselt) * inv
    yc = y - mean
    var = _head_sum(yc * yc, hsel, hselt) * inv
    yn = yc * lax.rsqrt(var + GN_EPS) * gn_g + gn_b + bonus_ref[0].astype(F32)
    out = _dot(_bf(yn * gate_ref[0].astype(F32)), wo_ref[...])
    o_ref[0] = x_ref[0] + gt * out


def _rwkv_post_call(x, yf, yb, bonus, gate, mod, per_batch, vec, wo, hsel, hselt):
    b, t, d = x.shape
    tm = min(512, t)
    mod_map = (lambda i, j: (i, 0, 0)) if per_batch else (lambda i, j: (0, 0, 0))
    tile = pl.BlockSpec((1, tm, d), lambda i, j: (i, j, 0))
    return pl.pallas_call(
        _rwkv_post_kernel,
        grid=(b, t // tm),
        in_specs=[tile, tile, tile, tile, tile, pl.BlockSpec((1, 6, d), mod_map),
                  _const_spec(vec.shape), _const_spec(wo.shape),
                  _const_spec(hsel.shape), _const_spec(hselt.shape)],
        out_specs=tile,
        out_shape=jax.ShapeDtypeStruct((b, t, d), F32),
        compiler_params=_params(("arbitrary", "arbitrary")),
        name="rwkv_post",
    )(x, yf, yb, bonus, gate, mod, vec, wo, hsel, hselt)


def _conv_kernel(x_ref, mod_ref, vec_ref, pw1_ref, pw1b_ref, dw_ref, pw2_ref, o_ref,
                 pad_ref, cv_ref, *, tm, seg):
    d = x_ref.shape[-1]
    nlb = d // LANE
    nseg = tm // seg
    mod = mod_ref[0]
    shift, scale, gt = mod[0:1], mod[1:2], mod[2:3]
    vec = vec_ref[...]
    g, dw_b, ln_g, ln_b, pw2_b = vec[0:1], vec[1:2], vec[2:3], vec[3:4], vec[4:5]
    x = x_ref[0]
    h = _rms_mod(x, g, shift, scale)
    u = _dot(_bf(h), pw1_ref[...]) + pw1b_ref[...]
    u = u[:, :d] * jax.nn.sigmoid(u[:, d:])

    zpad = jnp.zeros((nseg, CONV_PAD, LANE), F32)
    for lb in range(nlb):
        pad_ref[:, lb, 0:CONV_PAD, :] = zpad
        pad_ref[:, lb, CONV_PAD + seg:2 * CONV_PAD + seg, :] = zpad
        pad_ref[:, lb, CONV_PAD:CONV_PAD + seg, :] = u[:, lb * LANE:(lb + 1) * LANE].reshape(nseg, seg, LANE)

    base = CONV_PAD - CONV_WIDTH // 2

    def body(idx, carry):
        s = idx // nlb
        lb = idx % nlb
        acc = jnp.zeros((seg, LANE), F32)
        for j in range(CONV_WIDTH):
            acc = acc + pad_ref[s, lb, base + j:base + j + seg, :] * dw_ref[j, lb]
        cv_ref[s, lb] = acc
        return carry

    lax.fori_loop(0, nseg * nlb, body, 0)

    cv = jnp.concatenate([cv_ref[:, lb].reshape(tm, LANE) for lb in range(nlb)], axis=1) + dw_b
    mu = jnp.mean(cv, axis=-1, keepdims=True)
    cc = cv - mu
    var = jnp.mean(cc * cc, axis=-1, keepdims=True)
    z = cc * lax.rsqrt(var + LN_EPS) * ln_g + ln_b
    z = z * jax.nn.sigmoid(z)
    out = _dot(_bf(z), pw2_ref[...]) + pw2_b
    o_ref[0] = x + gt * out


def _conv_call(x, mod, per_batch, seg, vec, pw1, pw1b, dw, pw2):
    b, t, d = x.shape
    tm = min(512, t)
    seg = min(seg, tm)
    nlb = d // LANE
    mod_map = (lambda i, j: (i, 0, 0)) if per_batch else (lambda i, j: (0, 0, 0))
    tile = pl.BlockSpec((1, tm, d), lambda i, j: (i, j, 0))
    return pl.pallas_call(
        functools.partial(_conv_kernel, tm=tm, seg=seg),
        grid=(b, t // tm),
        in_specs=[tile, pl.BlockSpec((1, 6, d), mod_map), _const_spec(vec.shape),
                  _const_spec(pw1.shape), _const_spec(pw1b.shape), _const_spec(dw.shape),
                  _const_spec(pw2.shape)],
        out_specs=tile,
        out_shape=jax.ShapeDtypeStruct((b, t, d), F32),
        scratch_shapes=[pltpu.VMEM((tm // seg, nlb, seg + 2 * CONV_PAD, LANE), F32),
                        pltpu.VMEM((tm // seg, nlb, seg, LANE), F32)],
        compiler_params=_params(("arbitrary", "arbitrary")),
        name="conv_module",
    )(x, mod, vec, pw1, pw1b, dw, pw2)


def _router_gates_t(h, rwt_ref, rb_ref):
    h_hi, h_lo = _split(h)
    w_hi, w_lo = _split(rwt_ref[...])
    logits = _dot_nt(w_hi, h_hi) + _dot_nt(w_hi, h_lo) + _dot_nt(w_lo, h_hi) + rb_ref[...]
    m = jnp.max(logits, axis=0, keepdims=True)
    ex = jnp.exp(logits - m)
    p = ex / jnp.sum(ex, axis=0, keepdims=True)
    rows = [p[e:e + 1] for e in range(N_EXPERT_GROUPS * EXPERTS_PER_GROUP)]
    scores = []
    for gi in range(N_EXPERT_GROUPS):
        a, b, c, dd = rows[4 * gi:4 * gi + 4]
        hi1, lo1 = jnp.maximum(a, b), jnp.minimum(a, b)
        hi2, lo2 = jnp.maximum(c, dd), jnp.minimum(c, dd)
        scores.append(jnp.maximum(hi1, hi2) + jnp.maximum(jnp.minimum(hi1, hi2), jnp.maximum(lo1, lo2)))
    best = scores[0]
    sel = jnp.zeros(best.shape, jnp.int32)
    for gi in range(1, N_EXPERT_GROUPS):
        better = scores[gi] > best
        sel = jnp.where(better, gi, sel)
        best = jnp.where(better, scores[gi], best)
    qs = []
    for j in range(EXPERTS_PER_GROUP):
        qj = jnp.zeros(best.shape, F32)
        for gi in range(N_EXPERT_GROUPS):
            qj = jnp.where(sel == gi, rows[4 * gi + j], qj)
        qs.append(qj)

    def argmax4(vals):
        bv, bi = vals[0], jnp.zeros(best.shape, jnp.int32)
        for j in range(1, EXPERTS_PER_GROUP):
            better = vals[j] > bv
            bi = jnp.where(better, j, bi)
            bv = jnp.where(better, vals[j], bv)
        return bv, bi

    v1, i1 = argmax4(qs)
    v2, i2 = argmax4([jnp.where(i1 == j, -1.0, qs[j]) for j in range(EXPERTS_PER_GROUP)])
    den = v1 + v2
    w1, w2 = v1 / den, v2 / den
    gates = []
    for gi in range(N_EXPERT_GROUPS):
        rows_g = [jnp.where(sel == gi, jnp.where(i1 == j, w1, 0.0) + jnp.where(i2 == j, w2, 0.0), 0.0)
                  for j in range(EXPERTS_PER_GROUP)]
        gates.append(jnp.concatenate(rows_g, axis=0))
    return gates, sel


def _moe_kernel(x_ref, mod_ref, vec_ref, rwt_ref, rb_ref, wg_ref, wu_ref, wd_ref, o_ref,
                h_ref, gt_ref, rk_ref, rc_ref, early_ref, xg_ref, yg_ref, wr_ref, acc_ref, nblk_ref,
                *, n_exp, final):
    e = pl.program_id(2)
    tm = x_ref.shape[1]
    g = e // EXPERTS_PER_GROUP
    j = e % EXPERTS_PER_GROUP
    sub8 = 8

    @pl.when((pl.program_id(0) == 0) & (pl.program_id(1) == 0) & (e == 0))
    def _():
        early_ref[...] = _bf(jnp.where(lax.broadcasted_iota(jnp.int32, (tm, tm), 0)
                                       < lax.broadcasted_iota(jnp.int32, (tm, tm), 1), 1.0, 0.0))

    def block_rows(blk, size):
        return pl.ds(pl.multiple_of(blk * MOE_ROWS, MOE_ROWS), size)

    def block_base(blk):
        return (blk * MOE_ROWS).astype(F32)

    @pl.when(e == 0)
    def _():
        mod = mod_ref[0]
        h = _rms_mod(x_ref[0], vec_ref[0:1], mod[3:4], mod[4:5])
        h_ref[...] = _bf(h)
        gates, sel = _router_gates_t(h, rwt_ref, rb_ref)
        zero4 = jnp.zeros((sub8 - EXPERTS_PER_GROUP, tm), F32)
        for gi in range(N_EXPERT_GROUPS):
            gt_ref[gi] = jnp.concatenate([gates[gi], zero4], axis=0)
        chosen = [jnp.where(sel == gi, 1.0, 0.0) for gi in range(N_EXPERT_GROUPS)]
        selmat = _bf(jnp.concatenate(chosen + [jnp.zeros((sub8 - N_EXPERT_GROUPS, tm), F32)], axis=0))
        prefix = _dot(selmat, early_ref[...])
        rk = jnp.where(selmat > 0, prefix, -1.0)
        rk_ref[...] = rk
        rc_ref[...] = jnp.transpose(jnp.concatenate([rk, jnp.zeros((LANE - sub8, tm), F32)], axis=0))
        acc_ref[...] = jnp.zeros(acc_ref.shape, F32)

    @pl.when(j == 0)
    def _():
        sub_g = lax.broadcasted_iota(jnp.int32, (sub8, tm), 0)
        rank_row = jnp.sum(jnp.where(sub_g == g, rk_ref[...], 0.0), axis=0, keepdims=True)
        cnt = jnp.sum(jnp.where(rank_row >= 0.0, 1.0, 0.0)).astype(jnp.int32)
        nblk = (cnt + (MOE_ROWS - 1)) // MOE_ROWS
        nblk_ref[0] = nblk
        g_hi, g_lo = _split(gt_ref[g])
        slot = lax.broadcasted_iota(jnp.int32, (MOE_ROWS, tm), 0).astype(F32)

        def compact(blk, carry):
            rows = block_rows(blk, MOE_ROWS)
            pb = _bf(jnp.where(rank_row - block_base(blk) == slot, 1.0, 0.0))
            xg_ref[rows, :] = _bf(_dot(pb, h_ref[...]))
            wr_ref[rows, :] = _dot_nt(pb, g_hi) + _dot_nt(pb, g_lo)
            yg_ref[rows, :] = jnp.zeros((MOE_ROWS, yg_ref.shape[1]), F32)
            return carry

        lax.fori_loop(0, nblk, compact, 0)

    def expert(blk, size):
        rows = block_rows(blk, size)
        xb = xg_ref[rows, :]
        hg = _dot(xb, wg_ref[0, 0])
        hu = _dot(xb, wu_ref[0, 0])
        lane8 = lax.broadcasted_iota(jnp.int32, (size, sub8), 1)
        wcol = jnp.sum(jnp.where(lane8 == j, wr_ref[rows, :], 0.0), axis=1, keepdims=True)
        he = hg * jax.nn.sigmoid(hg) * hu * wcol
        yg_ref[rows, :] += _dot(_bf(he), wd_ref[0, 0])

    n_used = nblk_ref[0]
    n_pairs = lax.shift_right_logical(n_used, 1)

    def expert_pair(p, carry):
        expert(2 * p, 2 * MOE_ROWS)
        return carry

    lax.fori_loop(0, n_pairs, expert_pair, 0)

    @pl.when(n_used - 2 * n_pairs == 1)
    def _():
        expert(n_used - 1, MOE_ROWS)

    @pl.when(j == EXPERTS_PER_GROUP - 1)
    def _():
        lane_g = lax.broadcasted_iota(jnp.int32, (tm, LANE), 1)
        rank_col = jnp.sum(jnp.where(lane_g == g, rc_ref[...], 0.0), axis=1, keepdims=True)
        slot_l = lax.broadcasted_iota(jnp.int32, (tm, MOE_ROWS), 1).astype(F32)

        def spread(blk, carry):
            pt = _bf(jnp.where(rank_col - block_base(blk) == slot_l, 1.0, 0.0))
            acc_ref[...] += _dot(pt, _bf(yg_ref[block_rows(blk, MOE_ROWS), :]))
            return carry

        lax.fori_loop(0, n_used, spread, 0)

    @pl.when(e == n_exp - 1)
    def _():
        out = x_ref[0] + mod_ref[0][5:6] * acc_ref[...]
        if final:
            out = out * lax.rsqrt(jnp.mean(out * out, axis=-1, keepdims=True) + RMS_EPS) * vec_ref[1:2]
        o_ref[0] = out


def _moe_call(x, mod, per_batch, vec, rwt, rb, wg, wu, wd, layer, final):
    b, t, d = x.shape
    tm = min(1024, t)
    cap = -(-tm // MOE_ROWS) * MOE_ROWS
    _, n_exp, _, de = wg.shape
    mod_map = (lambda i, j, e: (i, 0, 0)) if per_batch else (lambda i, j, e: (0, 0, 0))
    tile = pl.BlockSpec((1, tm, d), lambda i, j, e: (i, j, 0))
    cst = lambda shape: pl.BlockSpec(shape, lambda i, j, e: (0,) * len(shape))
    return pl.pallas_call(
        functools.partial(_moe_kernel, n_exp=n_exp, final=final),
        grid=(b, t // tm, n_exp),
        in_specs=[tile, pl.BlockSpec((1, 6, d), mod_map), cst(vec.shape), cst(rwt.shape), cst(rb.shape),
                  pl.BlockSpec((1, 1, d, de), lambda i, j, e: (layer, e, 0, 0)),
                  pl.BlockSpec((1, 1, d, de), lambda i, j, e: (layer, e, 0, 0)),
                  pl.BlockSpec((1, 1, de, d), lambda i, j, e: (layer, e, 0, 0))],
        out_specs=tile,
        out_shape=jax.ShapeDtypeStruct((b, t, d), F32),
        scratch_shapes=[
            pltpu.VMEM((tm, d), BF16),
            pltpu.VMEM((N_EXPERT_GROUPS, 8, tm), F32),
            pltpu.VMEM((8, tm), F32),
            pltpu.VMEM((tm, LANE), F32),
            pltpu.VMEM((tm, tm), BF16),
            pltpu.VMEM((cap, d), BF16),
            pltpu.VMEM((cap, d), F32),
            pltpu.VMEM((cap, 8), F32),
            pltpu.VMEM((tm, d), F32),
            pltpu.SMEM((1,), jnp.int32),
        ],
        compiler_params=_params(("arbitrary", "arbitrary", "arbitrary")),
        name="grouped_moe",
    )(x, mod, vec, rwt, rb, wg, wu, wd)


def _pad_rows(a, rows):
    return jnp.concatenate([a, jnp.zeros((rows - a.shape[0],) + a.shape[1:], a.dtype)], axis=0)


def _lane_stacked_state(s0):
    b, two, h, n, _ = s0.shape
    return jnp.swapaxes(s0, 2, 3).reshape(b, two, n, h * n)


def kernel(x_prompt, x_sample, state_rwkv, c, c_ctx, norm_g, ada_w, ada_b, final_g, rwkv_mu, rwkv_w_rkv, rwkv_w_o, rwkv_w0, rwkv_w1, rwkv_w2, rwkv_a0, rwkv_a1, rwkv_a2, rwkv_g1, rwkv_g2, rwkv_k_k, rwkv_k_a, rwkv_r_k, rwkv_gn_g, rwkv_gn_b, conv_pw1, conv_pw1_b, conv_dw, conv_dw_b, conv_ln_g, conv_ln_b, conv_pw2, conv_pw2_b, router_w, router_b, moe_w_gate, moe_w_up, moe_w_down):
    d = x_prompt.shape[-1]
    depth = ada_w.shape[0]
    n_heads = d // HEAD_DIM
    dec_b = c.shape[0]

    cond = _pad_rows(jnp.concatenate([c_ctx[None, :], c], axis=0), COND_ROWS)
    mod = _mod_call(cond, ada_w, ada_b).reshape(depth, COND_ROWS, 6, d)

    head_of_lane = jnp.arange(d) // HEAD_DIM
    hsel = (head_of_lane[:, None] == jnp.arange(LANE)[None, :]).astype(BF16)
    hselt = jnp.concatenate([hsel.T, hsel.T], axis=0)
    rwt = router_w.T
    rb = router_b[:, None]
    wg = _bf(moe_w_gate)
    wu = _bf(moe_w_up)
    wd = _bf(moe_w_down)

    def lora_pad(w2):
        z = jnp.zeros_like(w2[0])
        return _bf(jnp.stack([jnp.concatenate([w2[0], z], 0), jnp.concatenate([z, w2[1]], 0)]))

    groups = (
        dict(x=x_prompt, rows=slice(0, 1), per_batch=False, s0=None, want_state=True, seg=x_prompt.shape[1]),
        dict(x=x_sample, rows=slice(1, 1 + dec_b), per_batch=True, s0=state_rwkv, want_state=False, seg=GRID_W),
    )
    outs = []
    new_state = None
    for gr in groups:
        x = gr["x"]
        states = []
        for i in range(depth):
            m_i = mod[i, gr["rows"]]
            j = i // 2
            if i % 2 == 0:
                vec = jnp.stack([norm_g[i, 0], rwkv_k_k[j], rwkv_k_a[j], rwkv_r_k[j].reshape(d),
                                 rwkv_w0[j, 0], rwkv_w0[j, 1], rwkv_a0[j, 0], rwkv_a0[j, 1]])
                r, v, kkn, logd, kd, bd, gate, bonus = _rwkv_pre_call(
                    x, m_i, gr["per_batch"], vec, rwkv_mu[j], _bf(rwkv_w_rkv[j]), _bf(rwkv_g1[j]), _bf(rwkv_g2[j]),
                    _bf(jnp.concatenate([rwkv_w1[j, 0], rwkv_w1[j, 1]], axis=1)), lora_pad(rwkv_w2[j]),
                    _bf(jnp.concatenate([rwkv_a1[j, 0], rwkv_a1[j, 1]], axis=1)), lora_pad(rwkv_a2[j]),
                    hsel, hselt)
                s0_bd = None if gr["s0"] is None else _lane_stacked_state(gr["s0"][:, j])
                res = _wkv_call(r, v, kkn, logd, kd, bd, s0_bd, gr["want_state"])
                if gr["want_state"]:
                    states.append(res[2])
                vec = _pad_rows(jnp.stack([rwkv_gn_g[j], rwkv_gn_b[j]]), 8)
                x = _rwkv_post_call(x, res[0], res[1], bonus, gate, m_i, gr["per_batch"], vec,
                                    _bf(rwkv_w_o[j]), hsel, hselt)
            else:
                vec = _pad_rows(jnp.stack([norm_g[i, 0], conv_dw_b[j], conv_ln_g[j], conv_ln_b[j], conv_pw2_b[j]]), 8)
                dw = conv_dw[j].reshape(CONV_WIDTH, d // LANE, 1, LANE)
                x = _conv_call(x, m_i, gr["per_batch"], gr["seg"], vec, _bf(conv_pw1[j]),
                               conv_pw1_b[j][None, :], dw, _bf(conv_pw2[j]))
            vec = _pad_rows(jnp.stack([norm_g[i, 1], final_g]), 8)
            xm = x if gr["per_batch"] else x.reshape(1, -1, d)
            xm = _moe_call(xm, m_i, gr["per_batch"], vec, rwt, rb, wg, wu, wd, i, final=(i == depth - 1))
            x = xm.reshape(x.shape)
        outs.append(x)
        if gr["want_state"]:
            new_state = jnp.stack(states, axis=1)
    return (outs[0], outs[1], new_state)
```

```python
import functools

import jax
import jax.numpy as jnp
from jax import lax
from jax.experimental import pallas as pl
from jax.experimental.pallas import tpu as pltpu

F32 = jnp.float32
BF16 = jnp.bfloat16

HEAD_DIM = 64
HEADS_PER_BLOCK = 2
BLOCK_LANES = HEAD_DIM * HEADS_PER_BLOCK
CHUNK = 64
WKV_BLOCKS_PER_STEP = 8
GRID_W = 64
MOE_ROWS = 256
CONV_WIDTH = 31
CONV_PAD = 16
N_EXPERT_GROUPS = 4
EXPERTS_PER_GROUP = 4
LANE = 128
COND_ROWS = 16
RMS_EPS = 1e-6
LN_EPS = 1e-5
GN_EPS = 64e-5
NEG_EXP_M_HALF = -0.6065306597126334
VMEM_LIMIT_BYTES = 56 * 1024 * 1024


def _bf(x):
    return x.astype(BF16)


def _dot(a, b):
    return jnp.dot(a, b, preferred_element_type=F32)


def _dot_nt(a, b):
    return lax.dot_general(a, b, (((1,), (1,)), ((), ())), preferred_element_type=F32)


def _split(x):
    hi = _bf(x)
    lo = _bf(x - hi.astype(F32))
    return hi, lo


def _rms_mod(x, g, shift, scale):
    y = x * lax.rsqrt(jnp.mean(x * x, axis=-1, keepdims=True) + RMS_EPS) * g
    return y * (1.0 + scale) + shift


def _head_sum(z, hsel, hselt2):
    hi, lo = _split(_dot(_bf(z), hsel))
    return _dot(jnp.concatenate([hi, lo], axis=1), hselt2)


def _params(sem):
    return pltpu.CompilerParams(dimension_semantics=sem, vmem_limit_bytes=VMEM_LIMIT_BYTES)


def _const_spec(shape):
    nd = len(shape)
    return pl.BlockSpec(shape, lambda *_: (0,) * nd, pipeline_mode=pl.Buffered(1))


def _mod_kernel(c_ref, w_ref, b_ref, o_ref):
    c = c_ref[...]
    s = c * jax.nn.sigmoid(c)
    o_ref[0] = jnp.dot(s, w_ref[0], preferred_element_type=F32,
                       precision=lax.Precision.HIGHEST) + b_ref[0]


def _mod_call(cond, ada_w, ada_b):
    depth, d, n = ada_w.shape
    tn = 1536
    return pl.pallas_call(
        _mod_kernel,
        grid=(depth, n // tn),
        in_specs=[
            pl.BlockSpec((COND_ROWS, d), lambda l, j: (0, 0)),
            pl.BlockSpec((1, d, tn), lambda l, j: (l, 0, j)),
            pl.BlockSpec((1, 1, tn), lambda l, j: (l, 0, j)),
        ],
        out_specs=pl.BlockSpec((1, COND_ROWS, tn), lambda l, j: (l, 0, j)),
        out_shape=jax.ShapeDtypeStruct((depth, COND_ROWS, n), F32),
        compiler_params=_params(("arbitrary", "arbitrary")),
        name="adaln_mod",
    )(cond, ada_w, ada_b.reshape(depth, 1, n))


def _rwkv_pre_kernel(x_ref, xp_ref, xn_ref, mod_ref, vec_ref, mu_ref, wrkv_ref, g1_ref, g2_ref,
                     w1_ref, w2_ref, a1_ref, a2_ref, hsel_ref, hselt_ref,
                     r_ref, v_ref, kkn_ref, logd_ref, kd_ref, bd_ref, gate_ref, bonus_ref,
                     *, tm, nt):
    t = pl.program_id(1)
    mod = mod_ref[0]
    shift, scale = mod[0:1], mod[1:2]
    vec = vec_ref[...]
    g, k_k, k_a, r_k = vec[0:1], vec[1:2], vec[2:3], vec[3:4]
    hsel = hsel_ref[...]
    hselt = hselt_ref[...]

    h = _rms_mod(x_ref[0], g, shift, scale)
    h_prev = _rms_mod(xp_ref[0], g, shift, scale)[7:8]
    h_next = _rms_mod(xn_ref[0], g, shift, scale)[0:1]
    h_prev = jnp.where(t == 0, 0.0, h_prev)
    h_next = jnp.where(t == nt - 1, 0.0, h_next)
    row = lax.broadcasted_iota(jnp.int32, (tm, 1), 0)
    prev = jnp.where(row == 0, h_prev, pltpu.roll(h, 1, 0))
    nxt = jnp.where(row == tm - 1, h_next, pltpu.roll(h, tm - 1, 0))
    xx = 0.5 * (prev + nxt) - h
    mu = mu_ref[...]

    def mix(i):
        return _bf(h + xx * mu[i:i + 1])

    r = _dot(mix(0), wrkv_ref[0])
    k = _dot(mix(2), wrkv_ref[1])
    v = _dot(mix(3), wrkv_ref[2])
    gate = _dot(_bf(jax.nn.sigmoid(_dot(mix(5), g1_ref[...]))), g2_ref[...])
    tw = _bf(jnp.tanh(_dot(mix(1), w1_ref[...])))
    ta = _bf(_dot(mix(4), a1_ref[...]))

    kk = k * k_k
    kkn = kk * lax.rsqrt(jnp.maximum(_head_sum(kk * kk, hsel, hselt), 1e-24))
    r_ref[0] = _bf(r)
    v_ref[0] = _bf(v)
    kkn_ref[0] = _bf(kkn)
    gate_ref[0] = _bf(gate)

    ksum = None
    for d in range(2):
        u = vec[4 + d:5 + d] + _dot(tw, w2_ref[d])
        logd_ref[d, 0] = NEG_EXP_M_HALF * jax.nn.sigmoid(u)
        a = jax.nn.sigmoid(vec[6 + d:7 + d] + _dot(ta, a2_ref[d]))
        kd = k * (1.0 + (a - 1.0) * k_a)
        kd_ref[d, 0] = _bf(kd)
        bd_ref[d, 0] = _bf(kkn * a)
        ksum = kd if ksum is None else ksum + kd
    bonus_ref[0] = _bf(_head_sum(r * r_k * ksum, hsel, hselt) * v)


def _rwkv_pre_call(x, mod, per_batch, vec, mu, wrkv, g1, g2, w1c, w2p, a1c, a2p, hsel, hselt):
    b, t, d = x.shape
    tm = min(512, t)
    nt = t // tm
    r8 = tm // 8
    nb8 = t // 8
    mod_map = (lambda i, j: (i, 0, 0)) if per_batch else (lambda i, j: (0, 0, 0))
    tile = pl.BlockSpec((1, tm, d), lambda i, j: (i, j, 0))
    tile2 = pl.BlockSpec((2, 1, tm, d), lambda i, j: (0, i, j, 0))
    one = jax.ShapeDtypeStruct((b, t, d), BF16)
    two = jax.ShapeDtypeStruct((2, b, t, d), BF16)
    two_f32 = jax.ShapeDtypeStruct((2, b, t, d), F32)
    return pl.pallas_call(
        functools.partial(_rwkv_pre_kernel, tm=tm, nt=nt),
        grid=(b, nt),
        in_specs=[
            tile,
            pl.BlockSpec((1, 8, d), lambda i, j: (i, jnp.maximum(j * r8 - 1, 0), 0)),
            pl.BlockSpec((1, 8, d), lambda i, j: (i, jnp.minimum((j + 1) * r8, nb8 - 1), 0)),
            pl.BlockSpec((1, 6, d), mod_map),
            _const_spec(vec.shape), _const_spec(mu.shape), _const_spec(wrkv.shape),
            _const_spec(g1.shape), _const_spec(g2.shape), _const_spec(w1c.shape),
            _const_spec(w2p.shape), _const_spec(a1c.shape), _const_spec(a2p.shape),
            _const_spec(hsel.shape), _const_spec(hselt.shape),
        ],
        out_specs=[tile, tile, tile, tile2, tile2, tile2, tile, tile],
        out_shape=[one, one, one, two_f32, two, two, one, one],
        compiler_params=_params(("arbitrary", "arbitrary")),
        name="rwkv_pre",
    )(x, x, x, mod, vec, mu, wrkv, g1, g2, w1c, w2p, a1c, a2p, hsel, hselt)


def _wkv_chains(chains, bd_mask):
    c = CHUNK
    n = BLOCK_LANES
    idx = range(len(chains))
    rev = [ch["reverse"] for ch in chains]
    row_t = lax.broadcasted_iota(jnp.int32, (c, n), 0)
    col_s = lax.broadcasted_iota(jnp.int32, (c, n), 1) % c
    strict = {False: row_t > col_s, True: row_t < col_s}
    incl = {False: row_t >= col_s, True: row_t <= col_s}
    zeros = jnp.zeros((c, n), F32)

    def blockdiag(z):
        return jnp.concatenate([_bf(z)] * HEADS_PER_BLOCK, axis=0) * bd_mask

    def chunk_cumsum(x, reverse):
        k = 1
        while k < c:
            if reverse:
                x = x + jnp.where(row_t < c - k, pltpu.roll(x, c - k, 0), 0.0)
            else:
                x = x + jnp.where(row_t >= k, pltpu.roll(x, k, 0), 0.0)
            k *= 2
        return x

    logd = [ch["logd"] for ch in chains]
    cum = [chunk_cumsum(logd[i], rev[i]) for i in idx]
    tot = [cum[i][0:1] if rev[i] else cum[i][c - 1:c] for i in idx]
    e_in = [jnp.exp(cum[i]) for i in idx]
    e_ex = [jnp.exp(cum[i] - logd[i]) for i in idx]
    e_inv = [jnp.exp(-cum[i]) for i in idx]
    e_bar = [jnp.exp(tot[i] - cum[i]) for i in idx]
    p_c = [jnp.exp(tot[i]) for i in idx]

    a_t = [-chains[i]["kkn"] * e_ex[i] for i in idx]
    r_t = [chains[i]["r"] * e_in[i] for i in idx]
    b_t = [chains[i]["bd"] * e_inv[i] for i in idx]
    k_t = [chains[i]["kd"] * e_inv[i] for i in idx]
    b_bar = [chains[i]["bd"] * e_bar[i] for i in idx]
    k_bar = [chains[i]["kd"] * e_bar[i] for i in idx]

    ycat = [jnp.concatenate([blockdiag(b_t[i]), blockdiag(k_t[i])], axis=0) for i in idx]
    g = [_dot_nt(_bf(jnp.concatenate([a_t[i], r_t[i]], axis=0)), ycat[i]) for i in idx]
    a_m = [jnp.where(strict[rev[i]], g[i][:c, :n], 0.0) for i in idx]
    ak_b = [_bf(jnp.where(strict[rev[i]], g[i][:c, n:], 0.0)) for i in idx]
    rb_b = [_bf(jnp.where(incl[rev[i]], g[i][c:, :n], 0.0)) for i in idx]
    rk_b = [_bf(jnp.where(incl[rev[i]], g[i][c:, n:], 0.0)) for i in idx]

    vst = [blockdiag(chains[i]["v"]) for i in idx]
    z = [_dot(ak_b[i], vst[i]) for i in idx]

    nm = a_m
    q = [_dot(_bf(a_m[i]), blockdiag(a_m[i])) for i in idx]
    levels = CHUNK.bit_length() - 2
    for j in range(levels):
        qb = [_bf(q[i]) for i in idx]
        nm = [nm[i] + q[i] + _dot(qb[i], blockdiag(nm[i])) for i in idx]
        if j + 1 < levels:
            q = [_dot(qb[i], blockdiag(q[i])) for i in idx]

    t_az = [jnp.concatenate([a_t[i], z[i]], axis=1)
            + _dot(_bf(nm[i]), jnp.concatenate([blockdiag(a_t[i]), blockdiag(z[i])], axis=1)) for i in idx]
    a_hat = [t_az[i][:, :n] for i in idx]
    tz = [t_az[i][:, n:] for i in idx]
    r_hat = [r_t[i] + _dot(rb_b[i], blockdiag(a_hat[i])) for i in idx]
    y0 = [_dot(rb_b[i], blockdiag(tz[i])) + _dot(rk_b[i], vst[i]) for i in idx]

    atz_t = [_bf(jnp.transpose(jnp.concatenate([a_hat[i], tz[i]], axis=0))) for i in idx]
    m_t = [_bf(_dot(atz_t[i], _bf(jnp.concatenate([b_bar[i], zeros], axis=0)))) * bd_mask for i in idx]
    k_rows = [jnp.concatenate([k_bar[i], zeros] if chains[i]["v_first"] else [zeros, k_bar[i]], axis=0) for i in idx]
    bd_mask_f = bd_mask.astype(F32)
    n0_bd = [(_dot(atz_t[i], _bf(jnp.concatenate([zeros, b_bar[i]], axis=0)))
              + _dot(_bf(chains[i]["vt"]), _bf(k_rows[i]))) * bd_mask_f for i in idx]
    n0 = [sum(x[hb * c:(hb + 1) * c] for hb in range(1, HEADS_PER_BLOCK)) + x[0:c] for x in n0_bd]

    st = [chains[i]["state"] for i in idx]
    y = [_dot_nt(_bf(r_hat[i]), blockdiag(st[i])) + y0[i] for i in idx]
    new_state = [st[i] * p_c[i] + _dot(_bf(st[i]), m_t[i]) + n0[i] for i in idx]
    return list(zip(y, new_state))


def _wkv_kernel(*refs, nc, has_s0, want_state):
    (rf, vf, af, ldf, kf, bf_, rb, vb, ab, ldb, kb, bb) = refs[:12]
    pos = 12
    s0_ref = None
    if has_s0:
        s0_ref = refs[pos]
        pos += 1
    yf_ref, yb_ref = refs[pos], refs[pos + 1]
    pos += 2
    so_ref = None
    if want_state:
        so_ref = refs[pos]
        pos += 1
    st_ref = refs[pos]
    ci = pl.program_id(2)
    n = BLOCK_LANES

    @pl.when(ci == 0)
    def _():
        if has_s0:
            st_ref[...] = s0_ref[0]
        else:
            st_ref[...] = jnp.zeros(st_ref.shape, F32)

    rowb = lax.broadcasted_iota(jnp.int32, (n, n), 0) // HEAD_DIM
    colb = lax.broadcasted_iota(jnp.int32, (n, n), 1) // HEAD_DIM
    bd_mask = _bf(jnp.where(rowb == colb, 1.0, 0.0))

    chains = []
    for qi in range(WKV_BLOCKS_PER_STEP):
        ln = slice(qi * n, (qi + 1) * n)
        f32 = lambda ref, *ix: ref[ix + (slice(None), ln)].astype(F32)
        v_f = f32(vf, 0)
        v_b = f32(vb, 0)
        vt = jnp.transpose(jnp.concatenate([v_f, v_b], axis=0))
        chains.append(dict(r=f32(rf, 0), v=v_f, kkn=f32(af, 0), logd=ldf[0, 0, :, ln], kd=f32(kf, 0, 0),
                           bd=f32(bf_, 0, 0), vt=vt, v_first=True, state=st_ref[0, :, ln], reverse=False))
        chains.append(dict(r=f32(rb, 0), v=v_b, kkn=f32(ab, 0), logd=ldb[0, 0, :, ln], kd=f32(kb, 0, 0),
                           bd=f32(bb, 0, 0), vt=vt, v_first=False, state=st_ref[1, :, ln], reverse=True))
    res = _wkv_chains(chains, bd_mask)
    finals = []
    for qi in range(WKV_BLOCKS_PER_STEP):
        ln = slice(qi * n, (qi + 1) * n)
        (y_f, s_f), (y_b, s_b) = res[2 * qi], res[2 * qi + 1]
        yf_ref[0, :, ln] = _bf(y_f)
        yb_ref[0, :, ln] = _bf(y_b)
        st_ref[0, :, ln] = s_f
        st_ref[1, :, ln] = s_b
        finals.append((s_f, s_b))

    if want_state:
        @pl.when(ci == nc - 1)
        def _():
            for qi, pair in enumerate(finals):
                for d, s in enumerate(pair):
                    for i in range(HEADS_PER_BLOCK):
                        so_ref[0, d, qi * HEADS_PER_BLOCK + i] = s[:, i * HEAD_DIM:(i + 1) * HEAD_DIM]


def _wkv_call(r, v, kkn, logd, kd, bd, s0_bd, want_state):
    b, t, d = r.shape
    c = CHUNK
    nc = t // c
    n = BLOCK_LANES * WKV_BLOCKS_PER_STEP
    nq = d // n
    fwd = pl.BlockSpec((1, c, n), lambda i, q, j: (i, j, q))
    bwd = pl.BlockSpec((1, c, n), lambda i, q, j: (i, nc - 1 - j, q))
    fwd2 = pl.BlockSpec((1, 1, c, n), lambda i, q, j: (0, i, j, q))
    bwd2 = pl.BlockSpec((1, 1, c, n), lambda i, q, j: (1, i, nc - 1 - j, q))
    in_specs = [fwd, fwd, fwd, fwd2, fwd2, fwd2, bwd, bwd, bwd, bwd2, bwd2, bwd2]
    args = [r, v, kkn, logd, kd, bd, r, v, kkn, logd, kd, bd]
    has_s0 = s0_bd is not None
    if has_s0:
        in_specs.append(pl.BlockSpec((1, 2, HEAD_DIM, n), lambda i, q, j: (i, 0, 0, q)))
        args.append(s0_bd)
    out_specs = [fwd, bwd]
    out_shape = [jax.ShapeDtypeStruct((b, t, d), BF16), jax.ShapeDtypeStruct((b, t, d), BF16)]
    if want_state:
        out_specs.append(pl.BlockSpec((1, 2, n // HEAD_DIM, HEAD_DIM, HEAD_DIM),
                                      lambda i, q, j: (i, 0, q, 0, 0)))
        out_shape.append(jax.ShapeDtypeStruct((b, 2, d // HEAD_DIM, HEAD_DIM, HEAD_DIM), F32))
    return pl.pallas_call(
        functools.partial(_wkv_kernel, nc=nc, has_s0=has_s0, want_state=want_state),
        grid=(b, nq, nc),
        in_specs=in_specs,
        out_specs=out_specs,
        out_shape=out_shape,
        scratch_shapes=[pltpu.VMEM((2, HEAD_DIM, n), F32)],
        compiler_params=_params(("arbitrary", "arbitrary", "arbitrary")),
        name="wkv_scan",
    )(*args)


def _rwkv_post_kernel(x_ref, yf_ref, yb_ref, bonus_ref, gate_ref, mod_ref, vec_ref, wo_ref,
                      hsel_ref, hselt_ref, o_ref):
    hsel = hsel_ref[...]
    hselt = hselt_ref[...]
    vec = vec_ref[...]
    gn_g, gn_b = vec[0:1], vec[1:2]
    gt = mod_ref[0][2:3]
    y = yf_ref[0].astype(F32) + yb_ref[0].astype(F32)
    inv = 1.0 / HEAD_DIM
    mean = _head_sum(y, hsel, hselt) * inv
    yc = y - mean
    var = _head_sum(yc * yc, hsel, hselt) * inv
    yn = yc * lax.rsqrt(var + GN_EPS) * gn_g + gn_b + bonus_ref[0].astype(F32)
    out = _dot(_bf(yn * gate_ref[0].astype(F32)), wo_ref[...])
    o_ref[0] = x_ref[0] + gt * out


def _rwkv_post_call(x, yf, yb, bonus, gate, mod, per_batch, vec, wo, hsel, hselt):
    b, t, d = x.shape
    tm = min(512, t)
    mod_map = (lambda i, j: (i, 0, 0)) if per_batch else (lambda i, j: (0, 0, 0))
    tile = pl.BlockSpec((1, tm, d), lambda i, j: (i, j, 0))
    return pl.pallas_call(
        _rwkv_post_kernel,
        grid=(b, t // tm),
        in_specs=[tile, tile, tile, tile, tile, pl.BlockSpec((1, 6, d), mod_map),
                  _const_spec(vec.shape), _const_spec(wo.shape),
                  _const_spec(hsel.shape), _const_spec(hselt.shape)],
        out_specs=tile,
        out_shape=jax.ShapeDtypeStruct((b, t, d), F32),
        compiler_params=_params(("arbitrary", "arbitrary")),
        name="rwkv_post",
    )(x, yf, yb, bonus, gate, mod, vec, wo, hsel, hselt)


def _conv_kernel(x_ref, mod_ref, vec_ref, pw1_ref, pw1b_ref, dw_ref, pw2_ref, o_ref,
                 pad_ref, cv_ref, *, tm, seg):
    d = x_ref.shape[-1]
    nlb = d // LANE
    nseg = tm // seg
    mod = mod_ref[0]
    shift, scale, gt = mod[0:1], mod[1:2], mod[2:3]
    vec = vec_ref[...]
    g, dw_b, ln_g, ln_b, pw2_b = vec[0:1], vec[1:2], vec[2:3], vec[3:4], vec[4:5]
    x = x_ref[0]
    h = _rms_mod(x, g, shift, scale)
    u = _dot(_bf(h), pw1_ref[...]) + pw1b_ref[...]
    u = u[:, :d] * jax.nn.sigmoid(u[:, d:])

    zpad = jnp.zeros((nseg, CONV_PAD, LANE), F32)
    for lb in range(nlb):
        pad_ref[:, lb, 0:CONV_PAD, :] = zpad
        pad_ref[:, lb, CONV_PAD + seg:2 * CONV_PAD + seg, :] = zpad
        pad_ref[:, lb, CONV_PAD:CONV_PAD + seg, :] = u[:, lb * LANE:(lb + 1) * LANE].reshape(nseg, seg, LANE)

    base = CONV_PAD - CONV_WIDTH // 2

    def body(idx, carry):
        s = idx // nlb
        lb = idx % nlb
        acc = jnp.zeros((seg, LANE), F32)
        for j in range(CONV_WIDTH):
            acc = acc + pad_ref[s, lb, base + j:base + j + seg, :] * dw_ref[j, lb]
        cv_ref[s, lb] = acc
        return carry

    lax.fori_loop(0, nseg * nlb, body, 0)

    cv = jnp.concatenate([cv_ref[:, lb].reshape(tm, LANE) for lb in range(nlb)], axis=1) + dw_b
    mu = jnp.mean(cv, axis=-1, keepdims=True)
    cc = cv - mu
    var = jnp.mean(cc * cc, axis=-1, keepdims=True)
    z = cc * lax.rsqrt(var + LN_EPS) * ln_g + ln_b
    z = z * jax.nn.sigmoid(z)
    out = _dot(_bf(z), pw2_ref[...]) + pw2_b
    o_ref[0] = x + gt * out


def _conv_call(x, mod, per_batch, seg, vec, pw1, pw1b, dw, pw2):
    b, t, d = x.shape
    tm = min(512, t)
    seg = min(seg, tm)
    nlb = d // LANE
    mod_map = (lambda i, j: (i, 0, 0)) if per_batch else (lambda i, j: (0, 0, 0))
    tile = pl.BlockSpec((1, tm, d), lambda i, j: (i, j, 0))
    return pl.pallas_call(
        functools.partial(_conv_kernel, tm=tm, seg=seg),
        grid=(b, t // tm),
        in_specs=[tile, pl.BlockSpec((1, 6, d), mod_map), _const_spec(vec.shape),
                  _const_spec(pw1.shape), _const_spec(pw1b.shape), _const_spec(dw.shape),
                  _const_spec(pw2.shape)],
        out_specs=tile,
        out_shape=jax.ShapeDtypeStruct((b, t, d), F32),
        scratch_shapes=[pltpu.VMEM((tm // seg, nlb, seg + 2 * CONV_PAD, LANE), F32),
                        pltpu.VMEM((tm // seg, nlb, seg, LANE), F32)],
        compiler_params=_params(("arbitrary", "arbitrary")),
        name="conv_module",
    )(x, mod, vec, pw1, pw1b, dw, pw2)


def _router_gates_t(h, rwt_ref, rb_ref):
    h_hi, h_lo = _split(h)
    w_hi, w_lo = _split(rwt_ref[...])
    logits = _dot_nt(w_hi, h_hi) + _dot_nt(w_hi, h_lo) + _dot_nt(w_lo, h_hi) + rb_ref[...]
    m = jnp.max(logits, axis=0, keepdims=True)
    ex = jnp.exp(logits - m)
    p = ex / jnp.sum(ex, axis=0, keepdims=True)
    rows = [p[e:e + 1] for e in range(N_EXPERT_GROUPS * EXPERTS_PER_GROUP)]
    scores = []
    for gi in range(N_EXPERT_GROUPS):
        a, b, c, dd = rows[4 * gi:4 * gi + 4]
        hi1, lo1 = jnp.maximum(a, b), jnp.minimum(a, b)
        hi2, lo2 = jnp.maximum(c, dd), jnp.minimum(c, dd)
        scores.append(jnp.maximum(hi1, hi2) + jnp.maximum(jnp.minimum(hi1, hi2), jnp.maximum(lo1, lo2)))
    best = scores[0]
    sel = jnp.zeros(best.shape, jnp.int32)
    for gi in range(1, N_EXPERT_GROUPS):
        better = scores[gi] > best
        sel = jnp.where(better, gi, sel)
        best = jnp.where(better, scores[gi], best)
    qs = []
    for j in range(EXPERTS_PER_GROUP):
        qj = jnp.zeros(best.shape, F32)
        for gi in range(N_EXPERT_GROUPS):
            qj = jnp.where(sel == gi, rows[4 * gi + j], qj)
        qs.append(qj)

    def argmax4(vals):
        bv, bi = vals[0], jnp.zeros(best.shape, jnp.int32)
        for j in range(1, EXPERTS_PER_GROUP):
            better = vals[j] > bv
            bi = jnp.where(better, j, bi)
            bv = jnp.where(better, vals[j], bv)
        return bv, bi

    v1, i1 = argmax4(qs)
    v2, i2 = argmax4([jnp.where(i1 == j, -1.0, qs[j]) for j in range(EXPERTS_PER_GROUP)])
    den = v1 + v2
    w1, w2 = v1 / den, v2 / den
    gates = []
    for gi in range(N_EXPERT_GROUPS):
        rows_g = [jnp.where(sel == gi, jnp.where(i1 == j, w1, 0.0) + jnp.where(i2 == j, w2, 0.0), 0.0)
                  for j in range(EXPERTS_PER_GROUP)]
        gates.append(jnp.concatenate(rows_g, axis=0))
    return gates, sel


def _moe_kernel(x_ref, mod_ref, vec_ref, rwt_ref, rb_ref, wg_ref, wu_ref, wd_ref, o_ref,
                h_ref, gt_ref, rk_ref, rc_ref, early_ref, xg_ref, yg_ref, wr_ref, acc_ref, nblk_ref,
                *, n_exp, final):
    e = pl.program_id(2)
    tm = x_ref.shape[1]
    g = e // EXPERTS_PER_GROUP
    j = e % EXPERTS_PER_GROUP
    sub8 = 8

    @pl.when((pl.program_id(0) == 0) & (pl.program_id(1) == 0) & (e == 0))
    def _():
        early_ref[...] = _bf(jnp.where(lax.broadcasted_iota(jnp.int32, (tm, tm), 0)
                                       < lax.broadcasted_iota(jnp.int32, (tm, tm), 1), 1.0, 0.0))

    def block_rows(blk, size):
        return pl.ds(pl.multiple_of(blk * MOE_ROWS, MOE_ROWS), size)

    def block_base(blk):
        return (blk * MOE_ROWS).astype(F32)

    @pl.when(e == 0)
    def _():
        mod = mod_ref[0]
        h = _rms_mod(x_ref[0], vec_ref[0:1], mod[3:4], mod[4:5])
        h_ref[...] = _bf(h)
        gates, sel = _router_gates_t(h, rwt_ref, rb_ref)
        zero4 = jnp.zeros((sub8 - EXPERTS_PER_GROUP, tm), F32)
        for gi in range(N_EXPERT_GROUPS):
            gt_ref[gi] = jnp.concatenate([gates[gi], zero4], axis=0)
        chosen = [jnp.where(sel == gi, 1.0, 0.0) for gi in range(N_EXPERT_GROUPS)]
        selmat = _bf(jnp.concatenate(chosen + [jnp.zeros((sub8 - N_EXPERT_GROUPS, tm), F32)], axis=0))
        prefix = _dot(selmat, early_ref[...])
        rk = jnp.where(selmat > 0, prefix, -1.0)
        rk_ref[...] = rk
        rc_ref[...] = jnp.transpose(jnp.concatenate([rk, jnp.zeros((LANE - sub8, tm), F32)], axis=0))
        acc_ref[...] = jnp.zeros(acc_ref.shape, F32)

    @pl.when(j == 0)
    def _():
        sub_g = lax.broadcasted_iota(jnp.int32, (sub8, tm), 0)
        rank_row = jnp.sum(jnp.where(sub_g == g, rk_ref[...], 0.0), axis=0, keepdims=True)
        cnt = jnp.sum(jnp.where(rank_row >= 0.0, 1.0, 0.0)).astype(jnp.int32)
        nblk = (cnt + (MOE_ROWS - 1)) // MOE_ROWS
        nblk_ref[0] = nblk
        nblk_ref[1] = (cnt + (MOE_ROWS // 2 - 1)) // (MOE_ROWS // 2)
        g_hi, g_lo = _split(gt_ref[g])
        slot = lax.broadcasted_iota(jnp.int32, (MOE_ROWS, tm), 0).astype(F32)

        def compact(blk, carry):
            rows = block_rows(blk, MOE_ROWS)
            pb = _bf(jnp.where(rank_row - block_base(blk) == slot, 1.0, 0.0))
            xg_ref[rows, :] = _bf(_dot(pb, h_ref[...]))
            wr_ref[rows, :] = _dot_nt(pb, g_hi) + _dot_nt(pb, g_lo)
            yg_ref[rows, :] = jnp.zeros((MOE_ROWS, yg_ref.shape[1]), F32)
            return carry

        lax.fori_loop(0, nblk, compact, 0)

    def expert(blk, size):
        rows = block_rows(blk, size)
        xb = xg_ref[rows, :]
        hg = _dot(xb, wg_ref[0, 0])
        hu = _dot(xb, wu_ref[0, 0])
        lane8 = lax.broadcasted_iota(jnp.int32, (size, sub8), 1)
        wcol = jnp.sum(jnp.where(lane8 == j, wr_ref[rows, :], 0.0), axis=1, keepdims=True)
        he = hg * jax.nn.sigmoid(hg) * hu * wcol
        yg_ref[rows, :] += _dot(_bf(he), wd_ref[0, 0])

    n_used = nblk_ref[0]
    n_half = nblk_ref[1]

    assert xg_ref.shape[0] >= 2 * MOE_ROWS

    @pl.when(n_half == 3)
    def _():
        expert(0, MOE_ROWS + MOE_ROWS // 2)

    @pl.when(n_half != 3)
    def _():
        n_pairs = lax.shift_right_logical(n_used, 1)

        def expert_pair(p, carry):
            expert(2 * p, 2 * MOE_ROWS)
            return carry

        lax.fori_loop(0, n_pairs, expert_pair, 0)

        @pl.when(n_used - 2 * n_pairs == 1)
        def _():
            expert(n_used - 1, MOE_ROWS)

    @pl.when(j == EXPERTS_PER_GROUP - 1)
    def _():
        lane_g = lax.broadcasted_iota(jnp.int32, (tm, LANE), 1)
        rank_col = jnp.sum(jnp.where(lane_g == g, rc_ref[...], 0.0), axis=1, keepdims=True)
        slot_l = lax.broadcasted_iota(jnp.int32, (tm, MOE_ROWS), 1).astype(F32)

        def spread(blk, carry):
            pt = _bf(jnp.where(rank_col - block_base(blk) == slot_l, 1.0, 0.0))
            acc_ref[...] += _dot(pt, _bf(yg_ref[block_rows(blk, MOE_ROWS), :]))
            return carry

        lax.fori_loop(0, n_used, spread, 0)

    @pl.when(e == n_exp - 1)
    def _():
        out = x_ref[0] + mod_ref[0][5:6] * acc_ref[...]
        if final:
            out = out * lax.rsqrt(jnp.mean(out * out, axis=-1, keepdims=True) + RMS_EPS) * vec_ref[1:2]
        o_ref[0] = out


def _moe_call(x, mod, per_batch, vec, rwt, rb, wg, wu, wd, layer, final):
    b, t, d = x.shape
    tm = min(1024, t)
    cap = max(-(-tm // MOE_ROWS), 2) * MOE_ROWS
    _, n_exp, _, de = wg.shape
    mod_map = (lambda i, j, e: (i, 0, 0)) if per_batch else (lambda i, j, e: (0, 0, 0))
    tile = pl.BlockSpec((1, tm, d), lambda i, j, e: (i, j, 0))
    cst = lambda shape: pl.BlockSpec(shape, lambda i, j, e: (0,) * len(shape))
    return pl.pallas_call(
        functools.partial(_moe_kernel, n_exp=n_exp, final=final),
        grid=(b, t // tm, n_exp),
        in_specs=[tile, pl.BlockSpec((1, 6, d), mod_map), cst(vec.shape), cst(rwt.shape), cst(rb.shape),
                  pl.BlockSpec((1, 1, d, de), lambda i, j, e: (layer, e, 0, 0)),
                  pl.BlockSpec((1, 1, d, de), lambda i, j, e: (layer, e, 0, 0)),
                  pl.BlockSpec((1, 1, de, d), lambda i, j, e: (layer, e, 0, 0))],
        out_specs=tile,
        out_shape=jax.ShapeDtypeStruct((b, t, d), F32),
        scratch_shapes=[
            pltpu.VMEM((tm, d), BF16),
            pltpu.VMEM((N_EXPERT_GROUPS, 8, tm), F32),
            pltpu.VMEM((8, tm), F32),
            pltpu.VMEM((tm, LANE), F32),
            pltpu.VMEM((tm, tm), BF16),
            pltpu.VMEM((cap, d), BF16),
            pltpu.VMEM((cap, d), F32),
            pltpu.VMEM((cap, 8), F32),
            pltpu.VMEM((tm, d), F32),
            pltpu.SMEM((2,), jnp.int32),
        ],
        compiler_params=_params(("arbitrary", "arbitrary", "arbitrary")),
        name="grouped_moe",
    )(x, mod, vec, rwt, rb, wg, wu, wd)


def _pad_rows(a, rows):
    return jnp.concatenate([a, jnp.zeros((rows - a.shape[0],) + a.shape[1:], a.dtype)], axis=0)


def _lane_stacked_state(s0):
    b, two, h, n, _ = s0.shape
    return jnp.swapaxes(s0, 2, 3).reshape(b, two, n, h * n)


def kernel(x_prompt, x_sample, state_rwkv, c, c_ctx, norm_g, ada_w, ada_b, final_g, rwkv_mu, rwkv_w_rkv, rwkv_w_o, rwkv_w0, rwkv_w1, rwkv_w2, rwkv_a0, rwkv_a1, rwkv_a2, rwkv_g1, rwkv_g2, rwkv_k_k, rwkv_k_a, rwkv_r_k, rwkv_gn_g, rwkv_gn_b, conv_pw1, conv_pw1_b, conv_dw, conv_dw_b, conv_ln_g, conv_ln_b, conv_pw2, conv_pw2_b, router_w, router_b, moe_w_gate, moe_w_up, moe_w_down):
    d = x_prompt.shape[-1]
    depth = ada_w.shape[0]
    n_heads = d // HEAD_DIM
    dec_b = c.shape[0]

    cond = _pad_rows(jnp.concatenate([c_ctx[None, :], c], axis=0), COND_ROWS)
    mod = _mod_call(cond, ada_w, ada_b).reshape(depth, COND_ROWS, 6, d)

    head_of_lane = jnp.arange(d) // HEAD_DIM
    hsel = (head_of_lane[:, None] == jnp.arange(LANE)[None, :]).astype(BF16)
    hselt = jnp.concatenate([hsel.T, hsel.T], axis=0)
    rwt = router_w.T
    rb = router_b[:, None]
    wg = _bf(moe_w_gate)
    wu = _bf(moe_w_up)
    wd = _bf(moe_w_down)

    def lora_pad(w2):
        z = jnp.zeros_like(w2[0])
        return _bf(jnp.stack([jnp.concatenate([w2[0], z], 0), jnp.concatenate([z, w2[1]], 0)]))

    groups = (
        dict(x=x_prompt, rows=slice(0, 1), per_batch=False, s0=None, want_state=True, seg=x_prompt.shape[1]),
        dict(x=x_sample, rows=slice(1, 1 + dec_b), per_batch=True, s0=state_rwkv, want_state=False, seg=GRID_W),
    )
    outs = []
    new_state = None
    for gr in groups:
        x = gr["x"]
        states = []
        for i in range(depth):
            m_i = mod[i, gr["rows"]]
            j = i // 2
            if i % 2 == 0:
                vec = jnp.stack([norm_g[i, 0], rwkv_k_k[j], rwkv_k_a[j], rwkv_r_k[j].reshape(d),
                                 rwkv_w0[j, 0], rwkv_w0[j, 1], rwkv_a0[j, 0], rwkv_a0[j, 1]])
                r, v, kkn, logd, kd, bd, gate, bonus = _rwkv_pre_call(
                    x, m_i, gr["per_batch"], vec, rwkv_mu[j], _bf(rwkv_w_rkv[j]), _bf(rwkv_g1[j]), _bf(rwkv_g2[j]),
                    _bf(jnp.concatenate([rwkv_w1[j, 0], rwkv_w1[j, 1]], axis=1)), lora_pad(rwkv_w2[j]),
                    _bf(jnp.concatenate([rwkv_a1[j, 0], rwkv_a1[j, 1]], axis=1)), lora_pad(rwkv_a2[j]),
                    hsel, hselt)
                s0_bd = None if gr["s0"] is None else _lane_stacked_state(gr["s0"][:, j])
                res = _wkv_call(r, v, kkn, logd, kd, bd, s0_bd, gr["want_state"])
                if gr["want_state"]:
                    states.append(res[2])
                vec = _pad_rows(jnp.stack([rwkv_gn_g[j], rwkv_gn_b[j]]), 8)
                x = _rwkv_post_call(x, res[0], res[1], bonus, gate, m_i, gr["per_batch"], vec,
                                    _bf(rwkv_w_o[j]), hsel, hselt)
            else:
                vec = _pad_rows(jnp.stack([norm_g[i, 0], conv_dw_b[j], conv_ln_g[j], conv_ln_b[j], conv_pw2_b[j]]), 8)
                dw = conv_dw[j].reshape(CONV_WIDTH, d // LANE, 1, LANE)
                x = _conv_call(x, m_i, gr["per_batch"], gr["seg"], vec, _bf(conv_pw1[j]),
                               conv_pw1_b[j][None, :], dw, _bf(conv_pw2[j]))
            vec = _pad_rows(jnp.stack([norm_g[i, 1], final_g]), 8)
            xm = x if gr["per_batch"] else x.reshape(1, -1, d)
            xm = _moe_call(xm, m_i, gr["per_batch"], vec, rwt, rb, wg, wu, wd, i, final=(i == depth - 1))
            x = xm.reshape(x.shape)
        outs.append(x)
        if gr["want_state"]:
            new_state = jnp.stack(states, axis=1)
    return (outs[0], outs[1], new_state)
```

```python
import functools

import jax
import jax.numpy as jnp
from jax import lax
from jax.experimental import pallas as pl
from jax.experimental.pallas import tpu as pltpu

F32 = jnp.float32
BF16 = jnp.bfloat16

HEAD_DIM = 64
HEADS_PER_BLOCK = 2
BLOCK_LANES = HEAD_DIM * HEADS_PER_BLOCK
CHUNK = 64
WKV_BLOCKS_PER_STEP = 8
GRID_W = 64
MOE_ROWS = 256
MOE_ROW_STEP = 64
CONV_WIDTH = 31
CONV_PAD = 16
N_EXPERT_GROUPS = 4
EXPERTS_PER_GROUP = 4
LANE = 128
COND_ROWS = 16
RMS_EPS = 1e-6
LN_EPS = 1e-5
GN_EPS = 64e-5
NEG_EXP_M_HALF = -0.6065306597126334
VMEM_LIMIT_BYTES = 56 * 1024 * 1024


def _bf(x):
    return x.astype(BF16)


def _dot(a, b):
    return jnp.dot(a, b, preferred_element_type=F32)


def _dot_nt(a, b):
    return lax.dot_general(a, b, (((1,), (1,)), ((), ())), preferred_element_type=F32)


def _split(x):
    hi = _bf(x)
    lo = _bf(x - hi.astype(F32))
    return hi, lo


def _rms_mod(x, g, shift, scale):
    y = x * lax.rsqrt(jnp.mean(x * x, axis=-1, keepdims=True) + RMS_EPS) * g
    return y * (1.0 + scale) + shift


def _head_sum(z, hsel, hselt2):
    hi, lo = _split(_dot(_bf(z), hsel))
    return _dot(jnp.concatenate([hi, lo], axis=1), hselt2)


def _params(sem):
    return pltpu.CompilerParams(dimension_semantics=sem, vmem_limit_bytes=VMEM_LIMIT_BYTES)


def _const_spec(shape):
    nd = len(shape)
    return pl.BlockSpec(shape, lambda *_: (0,) * nd, pipeline_mode=pl.Buffered(1))


def _mod_kernel(c_ref, w_ref, b_ref, o_ref):
    c = c_ref[...]
    s = c * jax.nn.sigmoid(c)
    o_ref[0] = jnp.dot(s, w_ref[0], preferred_element_type=F32,
                       precision=lax.Precision.HIGHEST) + b_ref[0]


def _mod_call(cond, ada_w, ada_b):
    depth, d, n = ada_w.shape
    tn = 1536
    return pl.pallas_call(
        _mod_kernel,
        grid=(depth, n // tn),
        in_specs=[
            pl.BlockSpec((COND_ROWS, d), lambda l, j: (0, 0)),
            pl.BlockSpec((1, d, tn), lambda l, j: (l, 0, j)),
            pl.BlockSpec((1, 1, tn), lambda l, j: (l, 0, j)),
        ],
        out_specs=pl.BlockSpec((1, COND_ROWS, tn), lambda l, j: (l, 0, j)),
        out_shape=jax.ShapeDtypeStruct((depth, COND_ROWS, n), F32),
        compiler_params=_params(("arbitrary", "arbitrary")),
        name="adaln_mod",
    )(cond, ada_w, ada_b.reshape(depth, 1, n))


def _rwkv_pre_kernel(x_ref, xp_ref, xn_ref, mod_ref, vec_ref, mu_ref, wrkv_ref, g1_ref, g2_ref,
                     w1_ref, w2_ref, a1_ref, a2_ref, hsel_ref, hselt_ref,
                     r_ref, v_ref, kkn_ref, logd_ref, kd_ref, bd_ref, gate_ref, bonus_ref,
                     *, tm, nt):
    t = pl.program_id(1)
    mod = mod_ref[0]
    shift, scale = mod[0:1], mod[1:2]
    vec = vec_ref[...]
    g, k_k, k_a, r_k = vec[0:1], vec[1:2], vec[2:3], vec[3:4]
    hsel = hsel_ref[...]
    hselt = hselt_ref[...]

    h = _rms_mod(x_ref[0], g, shift, scale)
    h_prev = _rms_mod(xp_ref[0], g, shift, scale)[7:8]
    h_next = _rms_mod(xn_ref[0], g, shift, scale)[0:1]
    h_prev = jnp.where(t == 0, 0.0, h_prev)
    h_next = jnp.where(t == nt - 1, 0.0, h_next)
    row = lax.broadcasted_iota(jnp.int32, (tm, 1), 0)
    prev = jnp.where(row == 0, h_prev, pltpu.roll(h, 1, 0))
    nxt = jnp.where(row == tm - 1, h_next, pltpu.roll(h, tm - 1, 0))
    xx = 0.5 * (prev + nxt) - h
    mu = mu_ref[...]

    def mix(i):
        return _bf(h + xx * mu[i:i + 1])

    r = _dot(mix(0), wrkv_ref[0])
    k = _dot(mix(2), wrkv_ref[1])
    v = _dot(mix(3), wrkv_ref[2])
    gate = _dot(_bf(jax.nn.sigmoid(_dot(mix(5), g1_ref[...]))), g2_ref[...])
    tw = _bf(jnp.tanh(_dot(mix(1), w1_ref[...])))
    ta = _bf(_dot(mix(4), a1_ref[...]))

    kk = k * k_k
    kkn = kk * lax.rsqrt(jnp.maximum(_head_sum(kk * kk, hsel, hselt), 1e-24))
    r_ref[0] = _bf(r)
    v_ref[0] = _bf(v)
    kkn_ref[0] = _bf(kkn)
    gate_ref[0] = _bf(gate)

    ksum = None
    for d in range(2):
        u = vec[4 + d:5 + d] + _dot(tw, w2_ref[d])
        logd_ref[d, 0] = NEG_EXP_M_HALF * jax.nn.sigmoid(u)
        a = jax.nn.sigmoid(vec[6 + d:7 + d] + _dot(ta, a2_ref[d]))
        kd = k * (1.0 + (a - 1.0) * k_a)
        kd_ref[d, 0] = _bf(kd)
        bd_ref[d, 0] = _bf(kkn * a)
        ksum = kd if ksum is None else ksum + kd
    bonus_ref[0] = _bf(_head_sum(r * r_k * ksum, hsel, hselt) * v)


def _rwkv_pre_call(x, mod, per_batch, vec, mu, wrkv, g1, g2, w1c, w2p, a1c, a2p, hsel, hselt):
    b, t, d = x.shape
    tm = min(512, t)
    nt = t // tm
    r8 = tm // 8
    nb8 = t // 8
    mod_map = (lambda i, j: (i, 0, 0)) if per_batch else (lambda i, j: (0, 0, 0))
    tile = pl.BlockSpec((1, tm, d), lambda i, j: (i, j, 0))
    tile2 = pl.BlockSpec((2, 1, tm, d), lambda i, j: (0, i, j, 0))
    one = jax.ShapeDtypeStruct((b, t, d), BF16)
    two = jax.ShapeDtypeStruct((2, b, t, d), BF16)
    two_f32 = jax.ShapeDtypeStruct((2, b, t, d), F32)
    return pl.pallas_call(
        functools.partial(_rwkv_pre_kernel, tm=tm, nt=nt),
        grid=(b, nt),
        in_specs=[
            tile,
            pl.BlockSpec((1, 8, d), lambda i, j: (i, jnp.maximum(j * r8 - 1, 0), 0)),
            pl.BlockSpec((1, 8, d), lambda i, j: (i, jnp.minimum((j + 1) * r8, nb8 - 1), 0)),
            pl.BlockSpec((1, 6, d), mod_map),
            _const_spec(vec.shape), _const_spec(mu.shape), _const_spec(wrkv.shape),
            _const_spec(g1.shape), _const_spec(g2.shape), _const_spec(w1c.shape),
            _const_spec(w2p.shape), _const_spec(a1c.shape), _const_spec(a2p.shape),
            _const_spec(hsel.shape), _const_spec(hselt.shape),
        ],
        out_specs=[tile, tile, tile, tile2, tile2, tile2, tile, tile],
        out_shape=[one, one, one, two_f32, two, two, one, one],
        compiler_params=_params(("arbitrary", "arbitrary")),
        name="rwkv_pre",
    )(x, x, x, mod, vec, mu, wrkv, g1, g2, w1c, w2p, a1c, a2p, hsel, hselt)


def _wkv_chains(chains, bd_mask):
    c = CHUNK
    n = BLOCK_LANES
    idx = range(len(chains))
    rev = [ch["reverse"] for ch in chains]
    row_t = lax.broadcasted_iota(jnp.int32, (c, n), 0)
    col_s = lax.broadcasted_iota(jnp.int32, (c, n), 1) % c
    strict = {False: row_t > col_s, True: row_t < col_s}
    incl = {False: row_t >= col_s, True: row_t <= col_s}
    zeros = jnp.zeros((c, n), F32)

    def blockdiag(z):
        return jnp.concatenate([_bf(z)] * HEADS_PER_BLOCK, axis=0) * bd_mask

    def chunk_cumsum(x, reverse):
        k = 1
        while k < c:
            if reverse:
                x = x + jnp.where(row_t < c - k, pltpu.roll(x, c - k, 0), 0.0)
            else:
                x = x + jnp.where(row_t >= k, pltpu.roll(x, k, 0), 0.0)
            k *= 2
        return x

    logd = [ch["logd"] for ch in chains]
    cum = [chunk_cumsum(logd[i], rev[i]) for i in idx]
    tot = [cum[i][0:1] if rev[i] else cum[i][c - 1:c] for i in idx]
    e_in = [jnp.exp(cum[i]) for i in idx]
    e_ex = [jnp.exp(cum[i] - logd[i]) for i in idx]
    e_inv = [jnp.exp(-cum[i]) for i in idx]
    e_bar = [jnp.exp(tot[i] - cum[i]) for i in idx]
    p_c = [jnp.exp(tot[i]) for i in idx]

    a_t = [-chains[i]["kkn"] * e_ex[i] for i in idx]
    r_t = [chains[i]["r"] * e_in[i] for i in idx]
    b_t = [chains[i]["bd"] * e_inv[i] for i in idx]
    k_t = [chains[i]["kd"] * e_inv[i] for i in idx]
    b_bar = [chains[i]["bd"] * e_bar[i] for i in idx]
    k_bar = [chains[i]["kd"] * e_bar[i] for i in idx]

    ycat = [jnp.concatenate([blockdiag(b_t[i]), blockdiag(k_t[i])], axis=0) for i in idx]
    g = [_dot_nt(_bf(jnp.concatenate([a_t[i], r_t[i]], axis=0)), ycat[i]) for i in idx]
    a_m = [jnp.where(strict[rev[i]], g[i][:c, :n], 0.0) for i in idx]
    ak_b = [_bf(jnp.where(strict[rev[i]], g[i][:c, n:], 0.0)) for i in idx]
    rb_b = [_bf(jnp.where(incl[rev[i]], g[i][c:, :n], 0.0)) for i in idx]
    rk_b = [_bf(jnp.where(incl[rev[i]], g[i][c:, n:], 0.0)) for i in idx]

    vst = [blockdiag(chains[i]["v"]) for i in idx]
    zk = [_dot(jnp.concatenate([ak_b[i], rk_b[i]], axis=0), vst[i]) for i in idx]
    z = [zk[i][:c] for i in idx]

    nm = a_m
    q = [_dot(_bf(a_m[i]), blockdiag(a_m[i])) for i in idx]
    levels = CHUNK.bit_length() - 2
    for j in range(levels):
        qb = [_bf(q[i]) for i in idx]
        if j + 1 < levels:
            both = [_dot(qb[i], jnp.concatenate([blockdiag(nm[i]), blockdiag(q[i])], axis=1)) for i in idx]
            nm = [nm[i] + q[i] + both[i][:, :n] for i in idx]
            q = [both[i][:, n:] for i in idx]
        else:
            nm = [nm[i] + q[i] + _dot(qb[i], blockdiag(nm[i])) for i in idx]

    t_az = [jnp.concatenate([a_t[i], z[i]], axis=1)
            + _dot(_bf(nm[i]), jnp.concatenate([blockdiag(a_t[i]), blockdiag(z[i])], axis=1)) for i in idx]
    a_hat = [t_az[i][:, :n] for i in idx]
    tz = [t_az[i][:, n:] for i in idx]
    rb_at = [_dot(rb_b[i], jnp.concatenate([blockdiag(a_hat[i]), blockdiag(tz[i])], axis=1)) for i in idx]
    r_hat = [r_t[i] + rb_at[i][:, :n] for i in idx]
    y0 = [rb_at[i][:, n:] + zk[i][c:] for i in idx]

    atz_t = [_bf(jnp.transpose(jnp.concatenate([a_hat[i], tz[i]], axis=0))) for i in idx]
    bb2 = [_bf(jnp.concatenate([jnp.concatenate([b_bar[i], zeros], axis=1),
                                jnp.concatenate([zeros, b_bar[i]], axis=1)], axis=0)) for i in idx]
    mn = [_dot(atz_t[i], bb2[i]) for i in idx]
    m_t = [_bf(mn[i][:, :n]) * bd_mask for i in idx]
    vk = []
    for p in range(0, len(chains), 2):
        assert chains[p]["v_first"] and not chains[p + 1]["v_first"]
        kk2 = _bf(jnp.concatenate([jnp.concatenate([k_bar[p], zeros], axis=1),
                                   jnp.concatenate([zeros, k_bar[p + 1]], axis=1)], axis=0))
        both = _dot(_bf(chains[p]["vt"]), kk2)
        vk += [both[:, :n], both[:, n:]]
    bd_mask_f = bd_mask.astype(F32)
    n0_bd = [(mn[i][:, n:] + vk[i]) * bd_mask_f for i in idx]
    n0 = [sum(x[hb * c:(hb + 1) * c] for hb in range(1, HEADS_PER_BLOCK)) + x[0:c] for x in n0_bd]

    st = [chains[i]["state"] for i in idx]
    y = [_dot_nt(_bf(r_hat[i]), blockdiag(st[i])) + y0[i] for i in idx]
    new_state = [st[i] * p_c[i] + _dot(_bf(st[i]), m_t[i]) + n0[i] for i in idx]
    return list(zip(y, new_state))


def _wkv_kernel(*refs, nc, has_s0, want_state):
    (rf, vf, af, ldf, kf, bf_, rb, vb, ab, ldb, kb, bb) = refs[:12]
    pos = 12
    s0_ref = None
    if has_s0:
        s0_ref = refs[pos]
        pos += 1
    yf_ref, yb_ref = refs[pos], refs[pos + 1]
    pos += 2
    so_ref = None
    if want_state:
        so_ref = refs[pos]
        pos += 1
    st_ref = refs[pos]
    ci = pl.program_id(2)
    n = BLOCK_LANES

    @pl.when(ci == 0)
    def _():
        if has_s0:
            st_ref[...] = s0_ref[0]
        else:
            st_ref[...] = jnp.zeros(st_ref.shape, F32)

    rowb = lax.broadcasted_iota(jnp.int32, (n, n), 0) // HEAD_DIM
    colb = lax.broadcasted_iota(jnp.int32, (n, n), 1) // HEAD_DIM
    bd_mask = _bf(jnp.where(rowb == colb, 1.0, 0.0))

    chains = []
    for qi in range(WKV_BLOCKS_PER_STEP):
        ln = slice(qi * n, (qi + 1) * n)
        f32 = lambda ref, *ix: ref[ix + (slice(None), ln)].astype(F32)
        v_f = f32(vf, 0)
        v_b = f32(vb, 0)
        vt = jnp.transpose(jnp.concatenate([v_f, v_b], axis=0))
        chains.append(dict(r=f32(rf, 0), v=v_f, kkn=f32(af, 0), logd=ldf[0, 0, :, ln], kd=f32(kf, 0, 0),
                           bd=f32(bf_, 0, 0), vt=vt, v_first=True, state=st_ref[0, :, ln], reverse=False))
        chains.append(dict(r=f32(rb, 0), v=v_b, kkn=f32(ab, 0), logd=ldb[0, 0, :, ln], kd=f32(kb, 0, 0),
                           bd=f32(bb, 0, 0), vt=vt, v_first=False, state=st_ref[1, :, ln], reverse=True))
    res = _wkv_chains(chains, bd_mask)
    finals = []
    for qi in range(WKV_BLOCKS_PER_STEP):
        ln = slice(qi * n, (qi + 1) * n)
        (y_f, s_f), (y_b, s_b) = res[2 * qi], res[2 * qi + 1]
        yf_ref[0, :, ln] = _bf(y_f)
        yb_ref[0, :, ln] = _bf(y_b)
        st_ref[0, :, ln] = s_f
        st_ref[1, :, ln] = s_b
        finals.append((s_f, s_b))

    if want_state:
        @pl.when(ci == nc - 1)
        def _():
            for qi, pair in enumerate(finals):
                for d, s in enumerate(pair):
                    for i in range(HEADS_PER_BLOCK):
                        so_ref[0, d, qi * HEADS_PER_BLOCK + i] = s[:, i * HEAD_DIM:(i + 1) * HEAD_DIM]


def _wkv_call(r, v, kkn, logd, kd, bd, s0_bd, want_state):
    b, t, d = r.shape
    c = CHUNK
    nc = t // c
    n = BLOCK_LANES * WKV_BLOCKS_PER_STEP
    nq = d // n
    fwd = pl.BlockSpec((1, c, n), lambda i, q, j: (i, j, q))
    bwd = pl.BlockSpec((1, c, n), lambda i, q, j: (i, nc - 1 - j, q))
    fwd2 = pl.BlockSpec((1, 1, c, n), lambda i, q, j: (0, i, j, q))
    bwd2 = pl.BlockSpec((1, 1, c, n), lambda i, q, j: (1, i, nc - 1 - j, q))
    in_specs = [fwd, fwd, fwd, fwd2, fwd2, fwd2, bwd, bwd, bwd, bwd2, bwd2, bwd2]
    args = [r, v, kkn, logd, kd, bd, r, v, kkn, logd, kd, bd]
    has_s0 = s0_bd is not None
    if has_s0:
        in_specs.append(pl.BlockSpec((1, 2, HEAD_DIM, n), lambda i, q, j: (i, 0, 0, q)))
        args.append(s0_bd)
    out_specs = [fwd, bwd]
    out_shape = [jax.ShapeDtypeStruct((b, t, d), BF16), jax.ShapeDtypeStruct((b, t, d), BF16)]
    if want_state:
        out_specs.append(pl.BlockSpec((1, 2, n // HEAD_DIM, HEAD_DIM, HEAD_DIM),
                                      lambda i, q, j: (i, 0, q, 0, 0)))
        out_shape.append(jax.ShapeDtypeStruct((b, 2, d // HEAD_DIM, HEAD_DIM, HEAD_DIM), F32))
    return pl.pallas_call(
        functools.partial(_wkv_kernel, nc=nc, has_s0=has_s0, want_state=want_state),
        grid=(b, nq, nc),
        in_specs=in_specs,
        out_specs=out_specs,
        out_shape=out_shape,
        scratch_shapes=[pltpu.VMEM((2, HEAD_DIM, n), F32)],
        compiler_params=_params(("arbitrary", "arbitrary", "arbitrary")),
        name="wkv_scan",
    )(*args)


def _rwkv_post_kernel(x_ref, yf_ref, yb_ref, bonus_ref, gate_ref, mod_ref, vec_ref, wo_ref,
                      hsel_ref, hselt_ref, o_ref):
    hsel = hsel_ref[...]
    hselt = hselt_ref[...]
    vec = vec_ref[...]
    gn_g, gn_b = vec[0:1], vec[1:2]
    gt = mod_ref[0][2:3]
    y = yf_ref[0].astype(F32) + yb_ref[0].astype(F32)
    inv = 1.0 / HEAD_DIM
    mean = _head_sum(y, hsel, hselt) * inv
    yc = y - mean
    var = _head_sum(yc * yc, hsel, hselt) * inv
    yn = yc * lax.rsqrt(var + GN_EPS) * gn_g + gn_b + bonus_ref[0].astype(F32)
    out = _dot(_bf(yn * gate_ref[0].astype(F32)), wo_ref[...])
    o_ref[0] = x_ref[0] + gt * out


def _rwkv_post_call(x, yf, yb, bonus, gate, mod, per_batch, vec, wo, hsel, hselt):
    b, t, d = x.shape
    tm = min(512, t)
    mod_map = (lambda i, j: (i, 0, 0)) if per_batch else (lambda i, j: (0, 0, 0))
    tile = pl.BlockSpec((1, tm, d), lambda i, j: (i, j, 0))
    return pl.pallas_call(
        _rwkv_post_kernel,
        grid=(b, t // tm),
        in_specs=[tile, tile, tile, tile, tile, pl.BlockSpec((1, 6, d), mod_map),
                  _const_spec(vec.shape), _const_spec(wo.shape),
                  _const_spec(hsel.shape), _const_spec(hselt.shape)],
        out_specs=tile,
        out_shape=jax.ShapeDtypeStruct((b, t, d), F32),
        compiler_params=_params(("arbitrary", "arbitrary")),
        name="rwkv_post",
    )(x, yf, yb, bonus, gate, mod, vec, wo, hsel, hselt)


def _conv_kernel(x_ref, mod_ref, vec_ref, pw1_ref, pw1b_ref, dw_ref, pw2_ref, o_ref,
                 pad_ref, cv_ref, *, tm, seg):
    d = x_ref.shape[-1]
    nlb = d // LANE
    nseg = tm // seg
    mod = mod_ref[0]
    shift, scale, gt = mod[0:1], mod[1:2], mod[2:3]
    vec = vec_ref[...]
    g, dw_b, ln_g, ln_b, pw2_b = vec[0:1], vec[1:2], vec[2:3], vec[3:4], vec[4:5]
    x = x_ref[0]
    h = _rms_mod(x, g, shift, scale)
    u = _dot(_bf(h), pw1_ref[...]) + pw1b_ref[...]
    u = u[:, :d] * jax.nn.sigmoid(u[:, d:])

    zpad = jnp.zeros((nseg, CONV_PAD, LANE), F32)
    for lb in range(nlb):
        pad_ref[:, lb, 0:CONV_PAD, :] = zpad
        pad_ref[:, lb, CONV_PAD + seg:2 * CONV_PAD + seg, :] = zpad
        pad_ref[:, lb, CONV_PAD:CONV_PAD + seg, :] = u[:, lb * LANE:(lb + 1) * LANE].reshape(nseg, seg, LANE)

    base = CONV_PAD - CONV_WIDTH // 2

    def body(idx, carry):
        s = idx // nlb
        lb = idx % nlb
        acc = jnp.zeros((seg, LANE), F32)
        for j in range(CONV_WIDTH):
            acc = acc + pad_ref[s, lb, base + j:base + j + seg, :] * dw_ref[j, lb]
        cv_ref[s, lb] = acc
        return carry

    lax.fori_loop(0, nseg * nlb, body, 0)

    cv = jnp.concatenate([cv_ref[:, lb].reshape(tm, LANE) for lb in range(nlb)], axis=1) + dw_b
    mu = jnp.mean(cv, axis=-1, keepdims=True)
    cc = cv - mu
    var = jnp.mean(cc * cc, axis=-1, keepdims=True)
    z = cc * lax.rsqrt(var + LN_EPS) * ln_g + ln_b
    z = z * jax.nn.sigmoid(z)
    out = _dot(_bf(z), pw2_ref[...]) + pw2_b
    o_ref[0] = x + gt * out


def _conv_call(x, mod, per_batch, seg, vec, pw1, pw1b, dw, pw2):
    b, t, d = x.shape
    tm = min(512, t)
    seg = min(seg, tm)
    nlb = d // LANE
    mod_map = (lambda i, j: (i, 0, 0)) if per_batch else (lambda i, j: (0, 0, 0))
    tile = pl.BlockSpec((1, tm, d), lambda i, j: (i, j, 0))
    return pl.pallas_call(
        functools.partial(_conv_kernel, tm=tm, seg=seg),
        grid=(b, t // tm),
        in_specs=[tile, pl.BlockSpec((1, 6, d), mod_map), _const_spec(vec.shape),
                  _const_spec(pw1.shape), _const_spec(pw1b.shape), _const_spec(dw.shape),
                  _const_spec(pw2.shape)],
        out_specs=tile,
        out_shape=jax.ShapeDtypeStruct((b, t, d), F32),
        scratch_shapes=[pltpu.VMEM((tm // seg, nlb, seg + 2 * CONV_PAD, LANE), F32),
                        pltpu.VMEM((tm // seg, nlb, seg, LANE), F32)],
        compiler_params=_params(("arbitrary", "arbitrary")),
        name="conv_module",
    )(x, mod, vec, pw1, pw1b, dw, pw2)


def _router_gates_t(h, rwt_ref, rb_ref):
    h_hi, h_lo = _split(h)
    w_hi, w_lo = _split(rwt_ref[...])
    logits = _dot_nt(w_hi, h_hi) + _dot_nt(w_hi, h_lo) + _dot_nt(w_lo, h_hi) + rb_ref[...]
    m = jnp.max(logits, axis=0, keepdims=True)
    ex = jnp.exp(logits - m)
    p = ex / jnp.sum(ex, axis=0, keepdims=True)
    rows = [p[e:e + 1] for e in range(N_EXPERT_GROUPS * EXPERTS_PER_GROUP)]
    scores = []
    for gi in range(N_EXPERT_GROUPS):
        a, b, c, dd = rows[4 * gi:4 * gi + 4]
        hi1, lo1 = jnp.maximum(a, b), jnp.minimum(a, b)
        hi2, lo2 = jnp.maximum(c, dd), jnp.minimum(c, dd)
        scores.append(jnp.maximum(hi1, hi2) + jnp.maximum(jnp.minimum(hi1, hi2), jnp.maximum(lo1, lo2)))
    best = scores[0]
    sel = jnp.zeros(best.shape, jnp.int32)
    for gi in range(1, N_EXPERT_GROUPS):
        better = scores[gi] > best
        sel = jnp.where(better, gi, sel)
        best = jnp.where(better, scores[gi], best)
    qs = []
    for j in range(EXPERTS_PER_GROUP):
        qj = jnp.zeros(best.shape, F32)
        for gi in range(N_EXPERT_GROUPS):
            qj = jnp.where(sel == gi, rows[4 * gi + j], qj)
        qs.append(qj)

    def argmax4(vals):
        bv, bi = vals[0], jnp.zeros(best.shape, jnp.int32)
        for j in range(1, EXPERTS_PER_GROUP):
            better = vals[j] > bv
            bi = jnp.where(better, j, bi)
            bv = jnp.where(better, vals[j], bv)
        return bv, bi

    v1, i1 = argmax4(qs)
    v2, i2 = argmax4([jnp.where(i1 == j, -1.0, qs[j]) for j in range(EXPERTS_PER_GROUP)])
    den = v1 + v2
    w1, w2 = v1 / den, v2 / den
    gates = []
    for gi in range(N_EXPERT_GROUPS):
        rows_g = [jnp.where(sel == gi, jnp.where(i1 == j, w1, 0.0) + jnp.where(i2 == j, w2, 0.0), 0.0)
                  for j in range(EXPERTS_PER_GROUP)]
        gates.append(jnp.concatenate(rows_g, axis=0))
    return gates, sel


def _moe_kernel(x_ref, mod_ref, vec_ref, rwt_ref, rb_ref, wg_ref, wu_ref, wd_ref, o_ref,
                h_ref, gt_ref, rk_ref, rc_ref, early_ref, xg_ref, yg_ref, wr_ref, acc_ref, nblk_ref,
                *, n_exp, final):
    e = pl.program_id(2)
    tm = x_ref.shape[1]
    g = e // EXPERTS_PER_GROUP
    j = e % EXPERTS_PER_GROUP
    sub8 = 8

    @pl.when((pl.program_id(0) == 0) & (pl.program_id(1) == 0) & (e == 0))
    def _():
        early_ref[...] = _bf(jnp.where(lax.broadcasted_iota(jnp.int32, (tm, tm), 0)
                                       < lax.broadcasted_iota(jnp.int32, (tm, tm), 1), 1.0, 0.0))

    def block_rows(blk, size):
        return pl.ds(pl.multiple_of(blk * MOE_ROWS, MOE_ROWS), size)

    def block_base(blk):
        return (blk * MOE_ROWS).astype(F32)

    @pl.when(e == 0)
    def _():
        mod = mod_ref[0]
        h = _rms_mod(x_ref[0], vec_ref[0:1], mod[3:4], mod[4:5])
        h_ref[...] = _bf(h)
        gates, sel = _router_gates_t(h, rwt_ref, rb_ref)
        zero4 = jnp.zeros((sub8 - EXPERTS_PER_GROUP, tm), F32)
        for gi in range(N_EXPERT_GROUPS):
            gt_ref[gi] = jnp.concatenate([gates[gi], zero4], axis=0)
        chosen = [jnp.where(sel == gi, 1.0, 0.0) for gi in range(N_EXPERT_GROUPS)]
        selmat = _bf(jnp.concatenate(chosen + [jnp.zeros((sub8 - N_EXPERT_GROUPS, tm), F32)], axis=0))
        prefix = _dot(selmat, early_ref[...])
        rk = jnp.where(selmat > 0, prefix, -1.0)
        rk_ref[...] = rk
        rc_ref[...] = jnp.transpose(jnp.concatenate([rk, jnp.zeros((LANE - sub8, tm), F32)], axis=0))
        acc_ref[...] = jnp.zeros(acc_ref.shape, F32)

    @pl.when(j == 0)
    def _():
        sub_g = lax.broadcasted_iota(jnp.int32, (sub8, tm), 0)
        rank_row = jnp.sum(jnp.where(sub_g == g, rk_ref[...], 0.0), axis=0, keepdims=True)
        cnt = jnp.sum(jnp.where(rank_row >= 0.0, 1.0, 0.0)).astype(jnp.int32)
        nblk = (cnt + (MOE_ROWS - 1)) // MOE_ROWS
        nblk_ref[0] = nblk
        nblk_ref[1] = (cnt + (MOE_ROW_STEP - 1)) // MOE_ROW_STEP
        g_hi, g_lo = _split(gt_ref[g])
        slot = lax.broadcasted_iota(jnp.int32, (MOE_ROWS, tm), 0).astype(F32)

        def compact(blk, carry):
            rows = block_rows(blk, MOE_ROWS)
            pb = _bf(jnp.where(rank_row - block_base(blk) == slot, 1.0, 0.0))
            xg_ref[rows, :] = _bf(_dot(pb, h_ref[...]))
            wr_ref[rows, :] = _dot_nt(pb, g_hi) + _dot_nt(pb, g_lo)
            yg_ref[rows, :] = jnp.zeros((MOE_ROWS, yg_ref.shape[1]), F32)
            return carry

        lax.fori_loop(0, nblk, compact, 0)

    def expert(blk, size):
        rows = block_rows(blk, size)
        xb = xg_ref[rows, :]
        hg = _dot(xb, wg_ref[0, 0])
        hu = _dot(xb, wu_ref[0, 0])
        lane8 = lax.broadcasted_iota(jnp.int32, (size, sub8), 1)
        wcol = jnp.sum(jnp.where(lane8 == j, wr_ref[rows, :], 0.0), axis=1, keepdims=True)
        he = hg * jax.nn.sigmoid(hg) * hu * wcol
        yg_ref[rows, :] += _dot(_bf(he), wd_ref[0, 0])

    n_used = nblk_ref[0]
    n_steps = nblk_ref[1]

    assert xg_ref.shape[0] >= 2 * MOE_ROWS
    steps_per_block = MOE_ROWS // MOE_ROW_STEP
    trimmed = range(steps_per_block + 1, 2 * steps_per_block - 1)
    for ns in trimmed:
        @pl.when(n_steps == ns)
        def _(ns=ns):
            expert(0, ns * MOE_ROW_STEP)

    @pl.when((n_steps < trimmed.start) | (n_steps >= trimmed.stop))
    def _():
        n_pairs = lax.shift_right_logical(n_used, 1)

        def expert_pair(p, carry):
            expert(2 * p, 2 * MOE_ROWS)
            return carry

        lax.fori_loop(0, n_pairs, expert_pair, 0)

        @pl.when(n_used - 2 * n_pairs == 1)
        def _():
            expert(n_used - 1, MOE_ROWS)

    @pl.when(j == EXPERTS_PER_GROUP - 1)
    def _():
        lane_g = lax.broadcasted_iota(jnp.int32, (tm, LANE), 1)
        rank_col = jnp.sum(jnp.where(lane_g == g, rc_ref[...], 0.0), axis=1, keepdims=True)
        slot_l = lax.broadcasted_iota(jnp.int32, (tm, MOE_ROWS), 1).astype(F32)

        def spread(blk, carry):
            pt = _bf(jnp.where(rank_col - block_base(blk) == slot_l, 1.0, 0.0))
            acc_ref[...] += _dot(pt, _bf(yg_ref[block_rows(blk, MOE_ROWS), :]))
            return carry

        lax.fori_loop(0, n_used, spread, 0)

    @pl.when(e == n_exp - 1)
    def _():
        out = x_ref[0] + mod_ref[0][5:6] * acc_ref[...]
        if final:
            out = out * lax.rsqrt(jnp.mean(out * out, axis=-1, keepdims=True) + RMS_EPS) * vec_ref[1:2]
        o_ref[0] = out


def _moe_call(x, mod, per_batch, vec, rwt, rb, wg, wu, wd, layer, final):
    b, t, d = x.shape
    tm = min(1024, t)
    cap = max(-(-tm // MOE_ROWS), 2) * MOE_ROWS
    _, n_exp, _, de = wg.shape
    mod_map = (lambda i, j, e: (i, 0, 0)) if per_batch else (lambda i, j, e: (0, 0, 0))
    tile = pl.BlockSpec((1, tm, d), lambda i, j, e: (i, j, 0))
    cst = lambda shape: pl.BlockSpec(shape, lambda i, j, e: (0,) * len(shape))
    return pl.pallas_call(
        functools.partial(_moe_kernel, n_exp=n_exp, final=final),
        grid=(b, t // tm, n_exp),
        in_specs=[tile, pl.BlockSpec((1, 6, d), mod_map), cst(vec.shape), cst(rwt.shape), cst(rb.shape),
                  pl.BlockSpec((1, 1, d, de), lambda i, j, e: (layer, e, 0, 0)),
                  pl.BlockSpec((1, 1, d, de), lambda i, j, e: (layer, e, 0, 0)),
                  pl.BlockSpec((1, 1, de, d), lambda i, j, e: (layer, e, 0, 0))],
        out_specs=tile,
        out_shape=jax.ShapeDtypeStruct((b, t, d), F32),
        scratch_shapes=[
            pltpu.VMEM((tm, d), BF16),
            pltpu.VMEM((N_EXPERT_GROUPS, 8, tm), F32),
            pltpu.VMEM((8, tm), F32),
            pltpu.VMEM((tm, LANE), F32),
            pltpu.VMEM((tm, tm), BF16),
            pltpu.VMEM((cap, d), BF16),
            pltpu.VMEM((cap, d), F32),
            pltpu.VMEM((cap, 8), F32),
            pltpu.VMEM((tm, d), F32),
            pltpu.SMEM((2,), jnp.int32),
        ],
        compiler_params=_params(("arbitrary", "arbitrary", "arbitrary")),
        name="grouped_moe",
    )(x, mod, vec, rwt, rb, wg, wu, wd)


def _pad_rows(a, rows):
    return jnp.concatenate([a, jnp.zeros((rows - a.shape[0],) + a.shape[1:], a.dtype)], axis=0)


def _lane_stacked_state(s0):
    b, two, h, n, _ = s0.shape
    return jnp.swapaxes(s0, 2, 3).reshape(b, two, n, h * n)


def kernel(x_prompt, x_sample, state_rwkv, c, c_ctx, norm_g, ada_w, ada_b, final_g, rwkv_mu, rwkv_w_rkv, rwkv_w_o, rwkv_w0, rwkv_w1, rwkv_w2, rwkv_a0, rwkv_a1, rwkv_a2, rwkv_g1, rwkv_g2, rwkv_k_k, rwkv_k_a, rwkv_r_k, rwkv_gn_g, rwkv_gn_b, conv_pw1, conv_pw1_b, conv_dw, conv_dw_b, conv_ln_g, conv_ln_b, conv_pw2, conv_pw2_b, router_w, router_b, moe_w_gate, moe_w_up, moe_w_down):
    d = x_prompt.shape[-1]
    depth = ada_w.shape[0]
    n_heads = d // HEAD_DIM
    dec_b = c.shape[0]

    cond = _pad_rows(jnp.concatenate([c_ctx[None, :], c], axis=0), COND_ROWS)
    mod = _mod_call(cond, ada_w, ada_b).reshape(depth, COND_ROWS, 6, d)

    head_of_lane = jnp.arange(d) // HEAD_DIM
    hsel = (head_of_lane[:, None] == jnp.arange(LANE)[None, :]).astype(BF16)
    hselt = jnp.concatenate([hsel.T, hsel.T], axis=0)
    rwt = router_w.T
    rb = router_b[:, None]
    wg = _bf(moe_w_gate)
    wu = _bf(moe_w_up)
    wd = _bf(moe_w_down)

    def lora_pad(w2):
        z = jnp.zeros_like(w2[0])
        return _bf(jnp.stack([jnp.concatenate([w2[0], z], 0), jnp.concatenate([z, w2[1]], 0)]))

    groups = (
        dict(x=x_prompt, rows=slice(0, 1), per_batch=False, s0=None, want_state=True, seg=x_prompt.shape[1]),
        dict(x=x_sample, rows=slice(1, 1 + dec_b), per_batch=True, s0=state_rwkv, want_state=False, seg=GRID_W),
    )
    outs = []
    new_state = None
    for gr in groups:
        x = gr["x"]
        states = []
        for i in range(depth):
            m_i = mod[i, gr["rows"]]
            j = i // 2
            if i % 2 == 0:
                vec = jnp.stack([norm_g[i, 0], rwkv_k_k[j], rwkv_k_a[j], rwkv_r_k[j].reshape(d),
                                 rwkv_w0[j, 0], rwkv_w0[j, 1], rwkv_a0[j, 0], rwkv_a0[j, 1]])
                r, v, kkn, logd, kd, bd, gate, bonus = _rwkv_pre_call(
                    x, m_i, gr["per_batch"], vec, rwkv_mu[j], _bf(rwkv_w_rkv[j]), _bf(rwkv_g1[j]), _bf(rwkv_g2[j]),
                    _bf(jnp.concatenate([rwkv_w1[j, 0], rwkv_w1[j, 1]], axis=1)), lora_pad(rwkv_w2[j]),
                    _bf(jnp.concatenate([rwkv_a1[j, 0], rwkv_a1[j, 1]], axis=1)), lora_pad(rwkv_a2[j]),
                    hsel, hselt)
                s0_bd = None if gr["s0"] is None else _lane_stacked_state(gr["s0"][:, j])
                res = _wkv_call(r, v, kkn, logd, kd, bd, s0_bd, gr["want_state"])
                if gr["want_state"]:
                    states.append(res[2])
                vec = _pad_rows(jnp.stack([rwkv_gn_g[j], rwkv_gn_b[j]]), 8)
                x = _rwkv_post_call(x, res[0], res[1], bonus, gate, m_i, gr["per_batch"], vec,
                                    _bf(rwkv_w_o[j]), hsel, hselt)
            else:
                vec = _pad_rows(jnp.stack([norm_g[i, 0], conv_dw_b[j], conv_ln_g[j], conv_ln_b[j], conv_pw2_b[j]]), 8)
                dw = conv_dw[j].reshape(CONV_WIDTH, d // LANE, 1, LANE)
                x = _conv_call(x, m_i, gr["per_batch"], gr["seg"], vec, _bf(conv_pw1[j]),
                               conv_pw1_b[j][None, :], dw, _bf(conv_pw2[j]))
            vec = _pad_rows(jnp.stack([norm_g[i, 1], final_g]), 8)
            xm = x if gr["per_batch"] else x.reshape(1, -1, d)
            xm = _moe_call(xm, m_i, gr["per_batch"], vec, rwt, rb, wg, wu, wd, i, final=(i == depth - 1))
            x = xm.reshape(x.shape)
        outs.append(x)
        if gr["want_state"]:
            new_state = jnp.stack(states, axis=1)
    return (outs[0], outs[1], new_state)
```

```python
import functools

import jax
import jax.numpy as jnp
from jax import lax
from jax.experimental import pallas as pl
from jax.experimental.pallas import tpu as pltpu

F32 = jnp.float32
BF16 = jnp.bfloat16

HEAD_DIM = 64
HEADS_PER_BLOCK = 2
BLOCK_LANES = HEAD_DIM * HEADS_PER_BLOCK
CHUNK = 64
WKV_BLOCKS_PER_STEP = 8
GRID_W = 64
MOE_ROWS = 256
MOE_ROW_STEP = 64
CONV_WIDTH = 31
CONV_PAD = 16
N_EXPERT_GROUPS = 4
EXPERTS_PER_GROUP = 4
LANE = 128
COND_ROWS = 16
RMS_EPS = 1e-6
LN_EPS = 1e-5
GN_EPS = 64e-5
NEG_EXP_M_HALF = -0.6065306597126334
VMEM_LIMIT_BYTES = 56 * 1024 * 1024


def _bf(x):
    return x.astype(BF16)


def _dot(a, b):
    return jnp.dot(a, b, preferred_element_type=F32)


def _dot_nt(a, b):
    return lax.dot_general(a, b, (((1,), (1,)), ((), ())), preferred_element_type=F32)


def _split(x):
    hi = _bf(x)
    lo = _bf(x - hi.astype(F32))
    return hi, lo


def _rms_mod(x, g, shift, scale):
    y = x * lax.rsqrt(jnp.mean(x * x, axis=-1, keepdims=True) + RMS_EPS) * g
    return y * (1.0 + scale) + shift


def _head_sum(z, hsel, hselt2):
    hi, lo = _split(_dot(_bf(z), hsel))
    return _dot(jnp.concatenate([hi, lo], axis=1), hselt2)


def _params(sem):
    return pltpu.CompilerParams(dimension_semantics=sem, vmem_limit_bytes=VMEM_LIMIT_BYTES)


def _const_spec(shape):
    nd = len(shape)
    return pl.BlockSpec(shape, lambda *_: (0,) * nd, pipeline_mode=pl.Buffered(1))


def _mod_kernel(c_ref, w_ref, b_ref, o_ref):
    c = c_ref[...]
    s = c * jax.nn.sigmoid(c)
    o_ref[0] = jnp.dot(s, w_ref[0], preferred_element_type=F32,
                       precision=lax.Precision.HIGHEST) + b_ref[0]


def _mod_call(cond, ada_w, ada_b):
    depth, d, n = ada_w.shape
    tn = 1536
    return pl.pallas_call(
        _mod_kernel,
        grid=(depth, n // tn),
        in_specs=[
            pl.BlockSpec((COND_ROWS, d), lambda l, j: (0, 0)),
            pl.BlockSpec((1, d, tn), lambda l, j: (l, 0, j)),
            pl.BlockSpec((1, 1, tn), lambda l, j: (l, 0, j)),
        ],
        out_specs=pl.BlockSpec((1, COND_ROWS, tn), lambda l, j: (l, 0, j)),
        out_shape=jax.ShapeDtypeStruct((depth, COND_ROWS, n), F32),
        compiler_params=_params(("arbitrary", "arbitrary")),
        name="adaln_mod",
    )(cond, ada_w, ada_b.reshape(depth, 1, n))


def _rwkv_pre_kernel(x_ref, xp_ref, xn_ref, mod_ref, vec_ref, mu_ref, wrkv_ref, g1_ref, g2_ref,
                     w1_ref, w2_ref, a1_ref, a2_ref, hsel_ref, hselt_ref,
                     r_ref, v_ref, kkn_ref, logd_ref, kd_ref, bd_ref, gate_ref, bonus_ref,
                     *, tm, nt):
    t = pl.program_id(1)
    mod = mod_ref[0]
    shift, scale = mod[0:1], mod[1:2]
    vec = vec_ref[...]
    g, k_k, k_a, r_k = vec[0:1], vec[1:2], vec[2:3], vec[3:4]
    hsel = hsel_ref[...]
    hselt = hselt_ref[...]

    h = _rms_mod(x_ref[0], g, shift, scale)
    h_prev = _rms_mod(xp_ref[0], g, shift, scale)[7:8]
    h_next = _rms_mod(xn_ref[0], g, shift, scale)[0:1]
    h_prev = jnp.where(t == 0, 0.0, h_prev)
    h_next = jnp.where(t == nt - 1, 0.0, h_next)
    row = lax.broadcasted_iota(jnp.int32, (tm, 1), 0)
    prev = jnp.where(row == 0, h_prev, pltpu.roll(h, 1, 0))
    nxt = jnp.where(row == tm - 1, h_next, pltpu.roll(h, tm - 1, 0))
    xx = 0.5 * (prev + nxt) - h
    mu = mu_ref[...]

    def mix(i):
        return _bf(h + xx * mu[i:i + 1])

    r = _dot(mix(0), wrkv_ref[0])
    k = _dot(mix(2), wrkv_ref[1])
    v = _dot(mix(3), wrkv_ref[2])
    gate = _dot(_bf(jax.nn.sigmoid(_dot(mix(5), g1_ref[...]))), g2_ref[...])
    tw = _bf(jnp.tanh(_dot(mix(1), w1_ref[...])))
    ta = _bf(_dot(mix(4), a1_ref[...]))

    kk = k * k_k
    kkn = kk * lax.rsqrt(jnp.maximum(_head_sum(kk * kk, hsel, hselt), 1e-24))
    r_ref[0] = _bf(r)
    v_ref[0] = _bf(v)
    kkn_ref[0] = _bf(kkn)
    gate_ref[0] = _bf(gate)

    ksum = None
    for d in range(2):
        u = vec[4 + d:5 + d] + _dot(tw, w2_ref[d])
        logd_ref[d, 0] = NEG_EXP_M_HALF * jax.nn.sigmoid(u)
        a = jax.nn.sigmoid(vec[6 + d:7 + d] + _dot(ta, a2_ref[d]))
        kd = k * (1.0 + (a - 1.0) * k_a)
        kd_ref[d, 0] = _bf(kd)
        bd_ref[d, 0] = _bf(kkn * a)
        ksum = kd if ksum is None else ksum + kd
    bonus_ref[0] = _bf(_head_sum(r * r_k * ksum, hsel, hselt) * v)


def _rwkv_pre_call(x, mod, per_batch, vec, mu, wrkv, g1, g2, w1c, w2p, a1c, a2p, hsel, hselt):
    b, t, d = x.shape
    tm = min(512, t)
    nt = t // tm
    r8 = tm // 8
    nb8 = t // 8
    mod_map = (lambda i, j: (i, 0, 0)) if per_batch else (lambda i, j: (0, 0, 0))
    tile = pl.BlockSpec((1, tm, d), lambda i, j: (i, j, 0))
    tile2 = pl.BlockSpec((2, 1, tm, d), lambda i, j: (0, i, j, 0))
    one = jax.ShapeDtypeStruct((b, t, d), BF16)
    two = jax.ShapeDtypeStruct((2, b, t, d), BF16)
    two_f32 = jax.ShapeDtypeStruct((2, b, t, d), F32)
    return pl.pallas_call(
        functools.partial(_rwkv_pre_kernel, tm=tm, nt=nt),
        grid=(b, nt),
        in_specs=[
            tile,
            pl.BlockSpec((1, 8, d), lambda i, j: (i, jnp.maximum(j * r8 - 1, 0), 0)),
            pl.BlockSpec((1, 8, d), lambda i, j: (i, jnp.minimum((j + 1) * r8, nb8 - 1), 0)),
            pl.BlockSpec((1, 6, d), mod_map),
            _const_spec(vec.shape), _const_spec(mu.shape), _const_spec(wrkv.shape),
            _const_spec(g1.shape), _const_spec(g2.shape), _const_spec(w1c.shape),
            _const_spec(w2p.shape), _const_spec(a1c.shape), _const_spec(a2p.shape),
            _const_spec(hsel.shape), _const_spec(hselt.shape),
        ],
        out_specs=[tile, tile, tile, tile2, tile2, tile2, tile, tile],
        out_shape=[one, one, one, two_f32, two, two, one, one],
        compiler_params=_params(("arbitrary", "arbitrary")),
        name="rwkv_pre",
    )(x, x, x, mod, vec, mu, wrkv, g1, g2, w1c, w2p, a1c, a2p, hsel, hselt)


def _wkv_chains(chains, bd_mask):
    c = CHUNK
    n = BLOCK_LANES
    idx = range(len(chains))
    rev = [ch["reverse"] for ch in chains]
    row_t = lax.broadcasted_iota(jnp.int32, (c, n), 0)
    col_s = lax.broadcasted_iota(jnp.int32, (c, n), 1) % c
    strict = {False: row_t > col_s, True: row_t < col_s}
    incl = {False: row_t >= col_s, True: row_t <= col_s}
    zeros = jnp.zeros((c, n), F32)

    def blockdiag(z):
        return jnp.concatenate([_bf(z)] * HEADS_PER_BLOCK, axis=0) * bd_mask

    def chunk_cumsum(x, reverse):
        k = 1
        while k < c:
            if reverse:
                x = x + jnp.where(row_t < c - k, pltpu.roll(x, c - k, 0), 0.0)
            else:
                x = x + jnp.where(row_t >= k, pltpu.roll(x, k, 0), 0.0)
            k *= 2
        return x

    logd = [ch["logd"] for ch in chains]
    cum = [chunk_cumsum(logd[i], rev[i]) for i in idx]
    tot = [cum[i][0:1] if rev[i] else cum[i][c - 1:c] for i in idx]
    e_in = [jnp.exp(cum[i]) for i in idx]
    e_ex = [jnp.exp(cum[i] - logd[i]) for i in idx]
    e_inv = [jnp.exp(-cum[i]) for i in idx]
    e_bar = [jnp.exp(tot[i] - cum[i]) for i in idx]
    p_c = [jnp.exp(tot[i]) for i in idx]

    a_t = [-chains[i]["kkn"] * e_ex[i] for i in idx]
    r_t = [chains[i]["r"] * e_in[i] for i in idx]
    b_t = [chains[i]["bd"] * e_inv[i] for i in idx]
    k_t = [chains[i]["kd"] * e_inv[i] for i in idx]
    b_bar = [chains[i]["bd"] * e_bar[i] for i in idx]
    k_bar = [chains[i]["kd"] * e_bar[i] for i in idx]

    ycat = [jnp.concatenate([blockdiag(b_t[i]), blockdiag(k_t[i])], axis=0) for i in idx]
    g = [_dot_nt(_bf(jnp.concatenate([a_t[i], r_t[i]], axis=0)), ycat[i]) for i in idx]
    a_m = [jnp.where(strict[rev[i]], g[i][:c, :n], 0.0) for i in idx]
    ak_b = [_bf(jnp.where(strict[rev[i]], g[i][:c, n:], 0.0)) for i in idx]
    rb_b = [_bf(jnp.where(incl[rev[i]], g[i][c:, :n], 0.0)) for i in idx]
    rk_b = [_bf(jnp.where(incl[rev[i]], g[i][c:, n:], 0.0)) for i in idx]

    vst = [blockdiag(chains[i]["v"]) for i in idx]
    zk = [_dot(jnp.concatenate([ak_b[i], rk_b[i]], axis=0), vst[i]) for i in idx]
    z = [zk[i][:c] for i in idx]

    nm = a_m
    q = [_dot(_bf(a_m[i]), blockdiag(a_m[i])) for i in idx]
    levels = CHUNK.bit_length() - 2
    for j in range(levels):
        qb = [_bf(q[i]) for i in idx]
        if j + 1 < levels:
            both = [_dot(qb[i], jnp.concatenate([blockdiag(nm[i]), blockdiag(q[i])], axis=1)) for i in idx]
            nm = [nm[i] + q[i] + both[i][:, :n] for i in idx]
            q = [both[i][:, n:] for i in idx]
        else:
            nm = [nm[i] + q[i] + _dot(qb[i], blockdiag(nm[i])) for i in idx]

    t_az = [jnp.concatenate([a_t[i], z[i]], axis=1)
            + _dot(_bf(nm[i]), jnp.concatenate([blockdiag(a_t[i]), blockdiag(z[i])], axis=1)) for i in idx]
    a_hat = [t_az[i][:, :n] for i in idx]
    tz = [t_az[i][:, n:] for i in idx]
    rb_at = [_dot(rb_b[i], jnp.concatenate([blockdiag(a_hat[i]), blockdiag(tz[i])], axis=1)) for i in idx]
    r_hat = [r_t[i] + rb_at[i][:, :n] for i in idx]
    y0 = [rb_at[i][:, n:] + zk[i][c:] for i in idx]

    atz_t = [_bf(jnp.transpose(jnp.concatenate([a_hat[i], tz[i]], axis=0))) for i in idx]
    bb2 = [_bf(jnp.concatenate([jnp.concatenate([b_bar[i], zeros], axis=1),
                                jnp.concatenate([zeros, b_bar[i]], axis=1)], axis=0)) for i in idx]
    mn = [_dot(atz_t[i], bb2[i]) for i in idx]
    m_t = [_bf(mn[i][:, :n]) * bd_mask for i in idx]
    vk = []
    for p in range(0, len(chains), 2):
        assert chains[p]["v_first"] and not chains[p + 1]["v_first"]
        kk2 = _bf(jnp.concatenate([jnp.concatenate([k_bar[p], zeros], axis=1),
                                   jnp.concatenate([zeros, k_bar[p + 1]], axis=1)], axis=0))
        both = _dot(_bf(chains[p]["vt"]), kk2)
        vk += [both[:, :n], both[:, n:]]
    bd_mask_f = bd_mask.astype(F32)
    n0_bd = [(mn[i][:, n:] + vk[i]) * bd_mask_f for i in idx]
    n0 = [sum(x[hb * c:(hb + 1) * c] for hb in range(1, HEADS_PER_BLOCK)) + x[0:c] for x in n0_bd]

    st = [chains[i]["state"] for i in idx]
    y = [_dot_nt(_bf(r_hat[i]), blockdiag(st[i])) + y0[i] for i in idx]
    new_state = [st[i] * p_c[i] + _dot(_bf(st[i]), m_t[i]) + n0[i] for i in idx]
    return list(zip(y, new_state))


def _wkv_kernel(*refs, nc, has_s0, want_state):
    (rf, vf, af, ldf, kf, bf_, rb, vb, ab, ldb, kb, bb) = refs[:12]
    pos = 12
    s0_ref = None
    if has_s0:
        s0_ref = refs[pos]
        pos += 1
    yf_ref, yb_ref = refs[pos], refs[pos + 1]
    pos += 2
    so_ref = None
    if want_state:
        so_ref = refs[pos]
        pos += 1
    st_ref = refs[pos]
    ci = pl.program_id(2)
    n = BLOCK_LANES

    @pl.when(ci == 0)
    def _():
        if has_s0:
            st_ref[...] = s0_ref[0]
        else:
            st_ref[...] = jnp.zeros(st_ref.shape, F32)

    rowb = lax.broadcasted_iota(jnp.int32, (n, n), 0) // HEAD_DIM
    colb = lax.broadcasted_iota(jnp.int32, (n, n), 1) // HEAD_DIM
    bd_mask = _bf(jnp.where(rowb == colb, 1.0, 0.0))

    chains = []
    for qi in range(WKV_BLOCKS_PER_STEP):
        ln = slice(qi * n, (qi + 1) * n)
        f32 = lambda ref, *ix: ref[ix + (slice(None), ln)].astype(F32)
        v_f = f32(vf, 0)
        v_b = f32(vb, 0)
        vt = jnp.transpose(jnp.concatenate([v_f, v_b], axis=0))
        chains.append(dict(r=f32(rf, 0), v=v_f, kkn=f32(af, 0), logd=ldf[0, 0, :, ln], kd=f32(kf, 0, 0),
                           bd=f32(bf_, 0, 0), vt=vt, v_first=True, state=st_ref[0, :, ln], reverse=False))
        chains.append(dict(r=f32(rb, 0), v=v_b, kkn=f32(ab, 0), logd=ldb[0, 0, :, ln], kd=f32(kb, 0, 0),
                           bd=f32(bb, 0, 0), vt=vt, v_first=False, state=st_ref[1, :, ln], reverse=True))
    res = _wkv_chains(chains, bd_mask)
    finals = []
    for qi in range(WKV_BLOCKS_PER_STEP):
        ln = slice(qi * n, (qi + 1) * n)
        (y_f, s_f), (y_b, s_b) = res[2 * qi], res[2 * qi + 1]
        yf_ref[0, :, ln] = _bf(y_f)
        yb_ref[0, :, ln] = _bf(y_b)
        st_ref[0, :, ln] = s_f
        st_ref[1, :, ln] = s_b
        finals.append((s_f, s_b))

    if want_state:
        @pl.when(ci == nc - 1)
        def _():
            for qi, pair in enumerate(finals):
                for d, s in enumerate(pair):
                    for i in range(HEADS_PER_BLOCK):
                        so_ref[0, d, qi * HEADS_PER_BLOCK + i] = s[:, i * HEAD_DIM:(i + 1) * HEAD_DIM]


def _wkv_call(r, v, kkn, logd, kd, bd, s0_bd, want_state):
    b, t, d = r.shape
    c = CHUNK
    nc = t // c
    n = BLOCK_LANES * WKV_BLOCKS_PER_STEP
    nq = d // n
    fwd = pl.BlockSpec((1, c, n), lambda i, q, j: (i, j, q))
    bwd = pl.BlockSpec((1, c, n), lambda i, q, j: (i, nc - 1 - j, q))
    fwd2 = pl.BlockSpec((1, 1, c, n), lambda i, q, j: (0, i, j, q))
    bwd2 = pl.BlockSpec((1, 1, c, n), lambda i, q, j: (1, i, nc - 1 - j, q))
    in_specs = [fwd, fwd, fwd, fwd2, fwd2, fwd2, bwd, bwd, bwd, bwd2, bwd2, bwd2]
    args = [r, v, kkn, logd, kd, bd, r, v, kkn, logd, kd, bd]
    has_s0 = s0_bd is not None
    if has_s0:
        in_specs.append(pl.BlockSpec((1, 2, HEAD_DIM, n), lambda i, q, j: (i, 0, 0, q)))
        args.append(s0_bd)
    out_specs = [fwd, bwd]
    out_shape = [jax.ShapeDtypeStruct((b, t, d), BF16), jax.ShapeDtypeStruct((b, t, d), BF16)]
    if want_state:
        out_specs.append(pl.BlockSpec((1, 2, n // HEAD_DIM, HEAD_DIM, HEAD_DIM),
                                      lambda i, q, j: (i, 0, q, 0, 0)))
        out_shape.append(jax.ShapeDtypeStruct((b, 2, d // HEAD_DIM, HEAD_DIM, HEAD_DIM), F32))
    return pl.pallas_call(
        functools.partial(_wkv_kernel, nc=nc, has_s0=has_s0, want_state=want_state),
        grid=(b, nq, nc),
        in_specs=in_specs,
        out_specs=out_specs,
        out_shape=out_shape,
        scratch_shapes=[pltpu.VMEM((2, HEAD_DIM, n), F32)],
        compiler_params=_params(("arbitrary", "arbitrary", "arbitrary")),
        name="wkv_scan",
    )(*args)


def _rwkv_post_kernel(x_ref, yf_ref, yb_ref, bonus_ref, gate_ref, mod_ref, vec_ref, wo_ref,
                      hsel_ref, hselt_ref, o_ref):
    hsel = hsel_ref[...]
    hselt = hselt_ref[...]
    vec = vec_ref[...]
    gn_g, gn_b = vec[0:1], vec[1:2]
    gt = mod_ref[0][2:3]
    y = yf_ref[0].astype(F32) + yb_ref[0].astype(F32)
    inv = 1.0 / HEAD_DIM
    mean = _head_sum(y, hsel, hselt) * inv
    yc = y - mean
    var = _head_sum(yc * yc, hsel, hselt) * inv
    yn = yc * lax.rsqrt(var + GN_EPS) * gn_g + gn_b + bonus_ref[0].astype(F32)
    out = _dot(_bf(yn * gate_ref[0].astype(F32)), wo_ref[...])
    o_ref[0] = x_ref[0] + gt * out


def _rwkv_post_call(x, yf, yb, bonus, gate, mod, per_batch, vec, wo, hsel, hselt):
    b, t, d = x.shape
    tm = min(512, t)
    mod_map = (lambda i, j: (i, 0, 0)) if per_batch else (lambda i, j: (0, 0, 0))
    tile = pl.BlockSpec((1, tm, d), lambda i, j: (i, j, 0))
    return pl.pallas_call(
        _rwkv_post_kernel,
        grid=(b, t // tm),
        in_specs=[tile, tile, tile, tile, tile, pl.BlockSpec((1, 6, d), mod_map),
                  _const_spec(vec.shape), _const_spec(wo.shape),
                  _const_spec(hsel.shape), _const_spec(hselt.shape)],
        out_specs=tile,
        out_shape=jax.ShapeDtypeStruct((b, t, d), F32),
        compiler_params=_params(("arbitrary", "arbitrary")),
        name="rwkv_post",
    )(x, yf, yb, bonus, gate, mod, vec, wo, hsel, hselt)


def _conv_kernel(x_ref, mod_ref, vec_ref, pw1_ref, pw1b_ref, dw_ref, pw2_ref, o_ref,
                 pad_ref, cv_ref, *, tm, seg):
    d = x_ref.shape[-1]
    nlb = d // LANE
    nseg = tm // seg
    mod = mod_ref[0]
    shift, scale, gt = mod[0:1], mod[1:2], mod[2:3]
    vec = vec_ref[...]
    g, dw_b, ln_g, ln_b, pw2_b = vec[0:1], vec[1:2], vec[2:3], vec[3:4], vec[4:5]
    x = x_ref[0]
    h = _rms_mod(x, g, shift, scale)
    u = _dot(_bf(h), pw1_ref[...]) + pw1b_ref[...]
    u = u[:, :d] * jax.nn.sigmoid(u[:, d:])

    zpad = jnp.zeros((nseg, CONV_PAD, LANE), F32)
    for lb in range(nlb):
        pad_ref[:, lb, 0:CONV_PAD, :] = zpad
        pad_ref[:, lb, CONV_PAD + seg:2 * CONV_PAD + seg, :] = zpad
        pad_ref[:, lb, CONV_PAD:CONV_PAD + seg, :] = u[:, lb * LANE:(lb + 1) * LANE].reshape(nseg, seg, LANE)

    base = CONV_PAD - CONV_WIDTH // 2

    def body(idx, carry):
        s = idx // nlb
        lb = idx % nlb
        acc = jnp.zeros((seg, LANE), F32)
        for j in range(CONV_WIDTH):
            acc = acc + pad_ref[s, lb, base + j:base + j + seg, :] * dw_ref[j, lb]
        cv_ref[s, lb] = acc
        return carry

    lax.fori_loop(0, nseg * nlb, body, 0, unroll=4)

    cv = jnp.concatenate([cv_ref[:, lb].reshape(tm, LANE) for lb in range(nlb)], axis=1) + dw_b
    mu = jnp.mean(cv, axis=-1, keepdims=True)
    cc = cv - mu
    var = jnp.mean(cc * cc, axis=-1, keepdims=True)
    z = cc * lax.rsqrt(var + LN_EPS) * ln_g + ln_b
    z = z * jax.nn.sigmoid(z)
    out = _dot(_bf(z), pw2_ref[...]) + pw2_b
    o_ref[0] = x + gt * out


def _conv_call(x, mod, per_batch, seg, vec, pw1, pw1b, dw, pw2):
    b, t, d = x.shape
    tm = min(512, t)
    seg = min(seg, tm)
    nlb = d // LANE
    mod_map = (lambda i, j: (i, 0, 0)) if per_batch else (lambda i, j: (0, 0, 0))
    tile = pl.BlockSpec((1, tm, d), lambda i, j: (i, j, 0))
    return pl.pallas_call(
        functools.partial(_conv_kernel, tm=tm, seg=seg),
        grid=(b, t // tm),
        in_specs=[tile, pl.BlockSpec((1, 6, d), mod_map), _const_spec(vec.shape),
                  _const_spec(pw1.shape), _const_spec(pw1b.shape), _const_spec(dw.shape),
                  _const_spec(pw2.shape)],
        out_specs=tile,
        out_shape=jax.ShapeDtypeStruct((b, t, d), F32),
        scratch_shapes=[pltpu.VMEM((tm // seg, nlb, seg + 2 * CONV_PAD, LANE), F32),
                        pltpu.VMEM((tm // seg, nlb, seg, LANE), F32)],
        compiler_params=_params(("arbitrary", "arbitrary")),
        name="conv_module",
    )(x, mod, vec, pw1, pw1b, dw, pw2)


def _router_gates_t(h, rwt_ref, rb_ref):
    h_hi, h_lo = _split(h)
    w_hi, w_lo = _split(rwt_ref[...])
    logits = _dot_nt(w_hi, h_hi) + _dot_nt(w_hi, h_lo) + _dot_nt(w_lo, h_hi) + rb_ref[...]
    m = jnp.max(logits, axis=0, keepdims=True)
    ex = jnp.exp(logits - m)
    p = ex / jnp.sum(ex, axis=0, keepdims=True)
    rows = [p[e:e + 1] for e in range(N_EXPERT_GROUPS * EXPERTS_PER_GROUP)]
    scores = []
    for gi in range(N_EXPERT_GROUPS):
        a, b, c, dd = rows[4 * gi:4 * gi + 4]
        hi1, lo1 = jnp.maximum(a, b), jnp.minimum(a, b)
        hi2, lo2 = jnp.maximum(c, dd), jnp.minimum(c, dd)
        scores.append(jnp.maximum(hi1, hi2) + jnp.maximum(jnp.minimum(hi1, hi2), jnp.maximum(lo1, lo2)))
    best = scores[0]
    sel = jnp.zeros(best.shape, jnp.int32)
    for gi in range(1, N_EXPERT_GROUPS):
        better = scores[gi] > best
        sel = jnp.where(better, gi, sel)
        best = jnp.where(better, scores[gi], best)
    qs = []
    for j in range(EXPERTS_PER_GROUP):
        qj = jnp.zeros(best.shape, F32)
        for gi in range(N_EXPERT_GROUPS):
            qj = jnp.where(sel == gi, rows[4 * gi + j], qj)
        qs.append(qj)

    def argmax4(vals):
        bv, bi = vals[0], jnp.zeros(best.shape, jnp.int32)
        for j in range(1, EXPERTS_PER_GROUP):
            better = vals[j] > bv
            bi = jnp.where(better, j, bi)
            bv = jnp.where(better, vals[j], bv)
        return bv, bi

    v1, i1 = argmax4(qs)
    v2, i2 = argmax4([jnp.where(i1 == j, -1.0, qs[j]) for j in range(EXPERTS_PER_GROUP)])
    den = v1 + v2
    w1, w2 = v1 / den, v2 / den
    gates = []
    for gi in range(N_EXPERT_GROUPS):
        rows_g = [jnp.where(sel == gi, jnp.where(i1 == j, w1, 0.0) + jnp.where(i2 == j, w2, 0.0), 0.0)
                  for j in range(EXPERTS_PER_GROUP)]
        gates.append(jnp.concatenate(rows_g, axis=0))
    return gates, sel


def _moe_kernel(x_ref, mod_ref, vec_ref, rwt_ref, rb_ref, wg_ref, wu_ref, wd_ref, o_ref,
                h_ref, gt_ref, rk_ref, rc_ref, early_ref, xg_ref, yg_ref, wr_ref, acc_ref, nblk_ref,
                *, n_exp, final):
    e = pl.program_id(2)
    tm = x_ref.shape[1]
    g = e // EXPERTS_PER_GROUP
    j = e % EXPERTS_PER_GROUP
    sub8 = 8

    @pl.when((pl.program_id(0) == 0) & (pl.program_id(1) == 0) & (e == 0))
    def _():
        early_ref[...] = _bf(jnp.where(lax.broadcasted_iota(jnp.int32, (tm, tm), 0)
                                       < lax.broadcasted_iota(jnp.int32, (tm, tm), 1), 1.0, 0.0))

    def block_rows(blk, size):
        return pl.ds(pl.multiple_of(blk * MOE_ROWS, MOE_ROWS), size)

    def block_base(blk):
        return (blk * MOE_ROWS).astype(F32)

    @pl.when(e == 0)
    def _():
        mod = mod_ref[0]
        h = _rms_mod(x_ref[0], vec_ref[0:1], mod[3:4], mod[4:5])
        h_ref[...] = _bf(h)
        gates, sel = _router_gates_t(h, rwt_ref, rb_ref)
        zero4 = jnp.zeros((sub8 - EXPERTS_PER_GROUP, tm), F32)
        for gi in range(N_EXPERT_GROUPS):
            gt_ref[gi] = jnp.concatenate([gates[gi], zero4], axis=0)
        chosen = [jnp.where(sel == gi, 1.0, 0.0) for gi in range(N_EXPERT_GROUPS)]
        selmat = _bf(jnp.concatenate(chosen + [jnp.zeros((sub8 - N_EXPERT_GROUPS, tm), F32)], axis=0))
        prefix = _dot(selmat, early_ref[...])
        rk = jnp.where(selmat > 0, prefix, -1.0)
        rk_ref[...] = rk
        rc_ref[...] = jnp.transpose(jnp.concatenate([rk, jnp.zeros((LANE - sub8, tm), F32)], axis=0))
        acc_ref[...] = jnp.zeros(acc_ref.shape, F32)

    @pl.when(j == 0)
    def _():
        sub_g = lax.broadcasted_iota(jnp.int32, (sub8, tm), 0)
        rank_row = jnp.sum(jnp.where(sub_g == g, rk_ref[...], 0.0), axis=0, keepdims=True)
        cnt = jnp.sum(jnp.where(rank_row >= 0.0, 1.0, 0.0)).astype(jnp.int32)
        nblk = (cnt + (MOE_ROWS - 1)) // MOE_ROWS
        nblk_ref[0] = nblk
        nblk_ref[1] = (cnt + (MOE_ROW_STEP - 1)) // MOE_ROW_STEP
        g_hi, g_lo = _split(gt_ref[g])
        slot = lax.broadcasted_iota(jnp.int32, (MOE_ROWS, tm), 0).astype(F32)

        def compact(blk, carry):
            rows = block_rows(blk, MOE_ROWS)
            pb = _bf(jnp.where(rank_row - block_base(blk) == slot, 1.0, 0.0))
            xg_ref[rows, :] = _bf(_dot(pb, h_ref[...]))
            wr_ref[rows, :] = _dot_nt(pb, g_hi) + _dot_nt(pb, g_lo)
            yg_ref[rows, :] = jnp.zeros((MOE_ROWS, yg_ref.shape[1]), F32)
            return carry

        lax.fori_loop(0, nblk, compact, 0)

    def expert(blk, size):
        rows = block_rows(blk, size)
        xb = xg_ref[rows, :]
        hg = _dot(xb, wg_ref[0, 0])
        hu = _dot(xb, wu_ref[0, 0])
        lane8 = lax.broadcasted_iota(jnp.int32, (size, sub8), 1)
        wcol = jnp.sum(jnp.where(lane8 == j, wr_ref[rows, :], 0.0), axis=1, keepdims=True)
        he = hg * jax.nn.sigmoid(hg) * hu * wcol
        yg_ref[rows, :] += _dot(_bf(he), wd_ref[0, 0])

    n_used = nblk_ref[0]
    n_steps = nblk_ref[1]

    assert xg_ref.shape[0] >= 2 * MOE_ROWS
    steps_per_block = MOE_ROWS // MOE_ROW_STEP
    trimmed = range(steps_per_block + 1, 2 * steps_per_block - 1)
    for ns in trimmed:
        @pl.when(n_steps == ns)
        def _(ns=ns):
            expert(0, ns * MOE_ROW_STEP)

    @pl.when((n_steps < trimmed.start) | (n_steps >= trimmed.stop))
    def _():
        n_pairs = lax.shift_right_logical(n_used, 1)

        def expert_pair(p, carry):
            expert(2 * p, 2 * MOE_ROWS)
            return carry

        lax.fori_loop(0, n_pairs, expert_pair, 0)

        @pl.when(n_used - 2 * n_pairs == 1)
        def _():
            expert(n_used - 1, MOE_ROWS)

    @pl.when(j == EXPERTS_PER_GROUP - 1)
    def _():
        lane_g = lax.broadcasted_iota(jnp.int32, (tm, LANE), 1)
        rank_col = jnp.sum(jnp.where(lane_g == g, rc_ref[...], 0.0), axis=1, keepdims=True)
        slot_l = lax.broadcasted_iota(jnp.int32, (tm, MOE_ROWS), 1).astype(F32)

        def spread(blk, carry):
            pt = _bf(jnp.where(rank_col - block_base(blk) == slot_l, 1.0, 0.0))
            acc_ref[...] += _dot(pt, _bf(yg_ref[block_rows(blk, MOE_ROWS), :]))
            return carry

        lax.fori_loop(0, n_used, spread, 0)

    @pl.when(e == n_exp - 1)
    def _():
        out = x_ref[0] + mod_ref[0][5:6] * acc_ref[...]
        if final:
            out = out * lax.rsqrt(jnp.mean(out * out, axis=-1, keepdims=True) + RMS_EPS) * vec_ref[1:2]
        o_ref[0] = out


def _moe_call(x, mod, per_batch, vec, rwt, rb, wg, wu, wd, layer, final):
    b, t, d = x.shape
    tm = min(1024, t)
    cap = max(-(-tm // MOE_ROWS), 2) * MOE_ROWS
    _, n_exp, _, de = wg.shape
    mod_map = (lambda i, j, e: (i, 0, 0)) if per_batch else (lambda i, j, e: (0, 0, 0))
    tile = pl.BlockSpec((1, tm, d), lambda i, j, e: (i, j, 0))
    cst = lambda shape: pl.BlockSpec(shape, lambda i, j, e: (0,) * len(shape))
    return pl.pallas_call(
        functools.partial(_moe_kernel, n_exp=n_exp, final=final),
        grid=(b, t // tm, n_exp),
        in_specs=[tile, pl.BlockSpec((1, 6, d), mod_map), cst(vec.shape), cst(rwt.shape), cst(rb.shape),
                  pl.BlockSpec((1, 1, d, de), lambda i, j, e: (layer, e, 0, 0)),
                  pl.BlockSpec((1, 1, d, de), lambda i, j, e: (layer, e, 0, 0)),
                  pl.BlockSpec((1, 1, de, d), lambda i, j, e: (layer, e, 0, 0))],
        out_specs=tile,
        out_shape=jax.ShapeDtypeStruct((b, t, d), F32),
        scratch_shapes=[
            pltpu.VMEM((tm, d), BF16),
            pltpu.VMEM((N_EXPERT_GROUPS, 8, tm), F32),
            pltpu.VMEM((8, tm), F32),
            pltpu.VMEM((tm, LANE), F32),
            pltpu.VMEM((tm, tm), BF16),
            pltpu.VMEM((cap, d), BF16),
            pltpu.VMEM((cap, d), F32),
            pltpu.VMEM((cap, 8), F32),
            pltpu.VMEM((tm, d), F32),
            pltpu.SMEM((2,), jnp.int32),
        ],
        compiler_params=_params(("arbitrary", "arbitrary", "arbitrary")),
        name="grouped_moe",
    )(x, mod, vec, rwt, rb, wg, wu, wd)


def _pad_rows(a, rows):
    return jnp.concatenate([a, jnp.zeros((rows - a.shape[0],) + a.shape[1:], a.dtype)], axis=0)


def _lane_stacked_state(s0):
    b, two, h, n, _ = s0.shape
    return jnp.swapaxes(s0, 2, 3).reshape(b, two, n, h * n)


def kernel(x_prompt, x_sample, state_rwkv, c, c_ctx, norm_g, ada_w, ada_b, final_g, rwkv_mu, rwkv_w_rkv, rwkv_w_o, rwkv_w0, rwkv_w1, rwkv_w2, rwkv_a0, rwkv_a1, rwkv_a2, rwkv_g1, rwkv_g2, rwkv_k_k, rwkv_k_a, rwkv_r_k, rwkv_gn_g, rwkv_gn_b, conv_pw1, conv_pw1_b, conv_dw, conv_dw_b, conv_ln_g, conv_ln_b, conv_pw2, conv_pw2_b, router_w, router_b, moe_w_gate, moe_w_up, moe_w_down):
    d = x_prompt.shape[-1]
    depth = ada_w.shape[0]
    n_heads = d // HEAD_DIM
    dec_b = c.shape[0]

    cond = _pad_rows(jnp.concatenate([c_ctx[None, :], c], axis=0), COND_ROWS)
    mod = _mod_call(cond, ada_w, ada_b).reshape(depth, COND_ROWS, 6, d)

    head_of_lane = jnp.arange(d) // HEAD_DIM
    hsel = (head_of_lane[:, None] == jnp.arange(LANE)[None, :]).astype(BF16)
    hselt = jnp.concatenate([hsel.T, hsel.T], axis=0)
    rwt = router_w.T
    rb = router_b[:, None]
    wg = _bf(moe_w_gate)
    wu = _bf(moe_w_up)
    wd = _bf(moe_w_down)

    def lora_pad(w2):
        z = jnp.zeros_like(w2[0])
        return _bf(jnp.stack([jnp.concatenate([w2[0], z], 0), jnp.concatenate([z, w2[1]], 0)]))

    groups = (
        dict(x=x_prompt, rows=slice(0, 1), per_batch=False, s0=None, want_state=True, seg=x_prompt.shape[1]),
        dict(x=x_sample, rows=slice(1, 1 + dec_b), per_batch=True, s0=state_rwkv, want_state=False, seg=GRID_W),
    )
    outs = []
    new_state = None
    for gr in groups:
        x = gr["x"]
        states = []
        for i in range(depth):
            m_i = mod[i, gr["rows"]]
            j = i // 2
            if i % 2 == 0:
                vec = jnp.stack([norm_g[i, 0], rwkv_k_k[j], rwkv_k_a[j], rwkv_r_k[j].reshape(d),
                                 rwkv_w0[j, 0], rwkv_w0[j, 1], rwkv_a0[j, 0], rwkv_a0[j, 1]])
                r, v, kkn, logd, kd, bd, gate, bonus = _rwkv_pre_call(
                    x, m_i, gr["per_batch"], vec, rwkv_mu[j], _bf(rwkv_w_rkv[j]), _bf(rwkv_g1[j]), _bf(rwkv_g2[j]),
                    _bf(jnp.concatenate([rwkv_w1[j, 0], rwkv_w1[j, 1]], axis=1)), lora_pad(rwkv_w2[j]),
                    _bf(jnp.concatenate([rwkv_a1[j, 0], rwkv_a1[j, 1]], axis=1)), lora_pad(rwkv_a2[j]),
                    hsel, hselt)
                s0_bd = None if gr["s0"] is None else _lane_stacked_state(gr["s0"][:, j])
                res = _wkv_call(r, v, kkn, logd, kd, bd, s0_bd, gr["want_state"])
                if gr["want_state"]:
                    states.append(res[2])
                vec = _pad_rows(jnp.stack([rwkv_gn_g[j], rwkv_gn_b[j]]), 8)
                x = _rwkv_post_call(x, res[0], res[1], bonus, gate, m_i, gr["per_batch"], vec,
                                    _bf(rwkv_w_o[j]), hsel, hselt)
            else:
                vec = _pad_rows(jnp.stack([norm_g[i, 0], conv_dw_b[j], conv_ln_g[j], conv_ln_b[j], conv_pw2_b[j]]), 8)
                dw = conv_dw[j].reshape(CONV_WIDTH, d // LANE, 1, LANE)
                x = _conv_call(x, m_i, gr["per_batch"], gr["seg"], vec, _bf(conv_pw1[j]),
                               conv_pw1_b[j][None, :], dw, _bf(conv_pw2[j]))
            vec = _pad_rows(jnp.stack([norm_g[i, 1], final_g]), 8)
            xm = x if gr["per_batch"] else x.reshape(1, -1, d)
            xm = _moe_call(xm, m_i, gr["per_batch"], vec, rwt, rb, wg, wu, wd, i, final=(i == depth - 1))
            x = xm.reshape(x.shape)
        outs.append(x)
        if gr["want_state"]:
            new_state = jnp.stack(states, axis=1)
    return (outs[0], outs[1], new_state)
```

```python
import functools

import jax
import jax.numpy as jnp
from jax import lax
from jax.experimental import pallas as pl
from jax.experimental.pallas import tpu as pltpu

F32 = jnp.float32
BF16 = jnp.bfloat16

HEAD_DIM = 64
HEADS_PER_BLOCK = 2
BLOCK_LANES = HEAD_DIM * HEADS_PER_BLOCK
CHUNK = 64
WKV_BLOCKS_PER_STEP = 8
GRID_W = 64
MOE_ROWS = 256
MOE_ROW_STEP = 64
CONV_WIDTH = 31
CONV_PAD = 16
N_EXPERT_GROUPS = 4
EXPERTS_PER_GROUP = 4
LANE = 128
COND_ROWS = 16
RMS_EPS = 1e-6
LN_EPS = 1e-5
GN_EPS = 64e-5
NEG_EXP_M_HALF = -0.6065306597126334
VMEM_LIMIT_BYTES = 56 * 1024 * 1024


def _bf(x):
    return x.astype(BF16)


def _dot(a, b):
    return jnp.dot(a, b, preferred_element_type=F32)


def _dot_nt(a, b):
    return lax.dot_general(a, b, (((1,), (1,)), ((), ())), preferred_element_type=F32)


def _split(x):
    hi = _bf(x)
    lo = _bf(x - hi.astype(F32))
    return hi, lo


def _rms_mod(x, g, shift, scale):
    y = x * lax.rsqrt(jnp.mean(x * x, axis=-1, keepdims=True) + RMS_EPS) * g
    return y * (1.0 + scale) + shift


def _head_sum(z, hsel, hselt2):
    hi, lo = _split(_dot(_bf(z), hsel))
    return _dot(jnp.concatenate([hi, lo], axis=1), hselt2)


def _params(sem):
    return pltpu.CompilerParams(dimension_semantics=sem, vmem_limit_bytes=VMEM_LIMIT_BYTES)


def _const_spec(shape):
    nd = len(shape)
    return pl.BlockSpec(shape, lambda *_: (0,) * nd, pipeline_mode=pl.Buffered(1))


def _mod_kernel(c_ref, w_ref, b_ref, o_ref):
    c = c_ref[...]
    s = c * jax.nn.sigmoid(c)
    o_ref[0] = jnp.dot(s, w_ref[0], preferred_element_type=F32,
                       precision=lax.Precision.HIGHEST) + b_ref[0]


def _mod_call(cond, ada_w, ada_b):
    depth, d, n = ada_w.shape
    tn = 1536
    return pl.pallas_call(
        _mod_kernel,
        grid=(depth, n // tn),
        in_specs=[
            pl.BlockSpec((COND_ROWS, d), lambda l, j: (0, 0)),
            pl.BlockSpec((1, d, tn), lambda l, j: (l, 0, j)),
            pl.BlockSpec((1, 1, tn), lambda l, j: (l, 0, j)),
        ],
        out_specs=pl.BlockSpec((1, COND_ROWS, tn), lambda l, j: (l, 0, j)),
        out_shape=jax.ShapeDtypeStruct((depth, COND_ROWS, n), F32),
        compiler_params=_params(("arbitrary", "arbitrary")),
        name="adaln_mod",
    )(cond, ada_w, ada_b.reshape(depth, 1, n))


def _rwkv_pre_kernel(x_ref, xp_ref, xn_ref, mod_ref, vec_ref, mu_ref, wrkv_ref, g1_ref, g2_ref,
                     w1_ref, w2_ref, a1_ref, a2_ref, hsel_ref, hselt_ref,
                     r_ref, v_ref, kkn_ref, logd_ref, kd_ref, bd_ref, gate_ref, bonus_ref,
                     *, tm, nt):
    t = pl.program_id(1)
    mod = mod_ref[0]
    shift, scale = mod[0:1], mod[1:2]
    vec = vec_ref[...]
    g, k_k, k_a, r_k = vec[0:1], vec[1:2], vec[2:3], vec[3:4]
    hsel = hsel_ref[...]
    hselt = hselt_ref[...]

    h = _rms_mod(x_ref[0], g, shift, scale)
    h_prev = _rms_mod(xp_ref[0], g, shift, scale)[7:8]
    h_next = _rms_mod(xn_ref[0], g, shift, scale)[0:1]
    h_prev = jnp.where(t == 0, 0.0, h_prev)
    h_next = jnp.where(t == nt - 1, 0.0, h_next)
    row = lax.broadcasted_iota(jnp.int32, (tm, 1), 0)
    prev = jnp.where(row == 0, h_prev, pltpu.roll(h, 1, 0))
    nxt = jnp.where(row == tm - 1, h_next, pltpu.roll(h, tm - 1, 0))
    xx = 0.5 * (prev + nxt) - h
    mu = mu_ref[...]

    def mix(i):
        return _bf(h + xx * mu[i:i + 1])

    r = _dot(mix(0), wrkv_ref[0])
    k = _dot(mix(2), wrkv_ref[1])
    v = _dot(mix(3), wrkv_ref[2])
    gate = _dot(_bf(jax.nn.sigmoid(_dot(mix(5), g1_ref[...]))), g2_ref[...])
    tw = _bf(jnp.tanh(_dot(mix(1), w1_ref[...])))
    ta = _bf(_dot(mix(4), a1_ref[...]))

    kk = k * k_k
    kkn = kk * lax.rsqrt(jnp.maximum(_head_sum(kk * kk, hsel, hselt), 1e-24))
    r_ref[0] = _bf(r)
    v_ref[0] = _bf(v)
    kkn_ref[0] = _bf(kkn)
    gate_ref[0] = _bf(gate)

    ksum = None
    for d in range(2):
        u = vec[4 + d:5 + d] + _dot(tw, w2_ref[d])
        logd_ref[d, 0] = NEG_EXP_M_HALF * jax.nn.sigmoid(u)
        a = jax.nn.sigmoid(vec[6 + d:7 + d] + _dot(ta, a2_ref[d]))
        kd = k * (1.0 + (a - 1.0) * k_a)
        kd_ref[d, 0] = _bf(kd)
        bd_ref[d, 0] = _bf(kkn * a)
        ksum = kd if ksum is None else ksum + kd
    bonus_ref[0] = _bf(_head_sum(r * r_k * ksum, hsel, hselt) * v)


def _rwkv_pre_call(x, mod, per_batch, vec, mu, wrkv, g1, g2, w1c, w2p, a1c, a2p, hsel, hselt):
    b, t, d = x.shape
    tm = min(512, t)
    nt = t // tm
    r8 = tm // 8
    nb8 = t // 8
    mod_map = (lambda i, j: (i, 0, 0)) if per_batch else (lambda i, j: (0, 0, 0))
    tile = pl.BlockSpec((1, tm, d), lambda i, j: (i, j, 0))
    tile2 = pl.BlockSpec((2, 1, tm, d), lambda i, j: (0, i, j, 0))
    one = jax.ShapeDtypeStruct((b, t, d), BF16)
    two = jax.ShapeDtypeStruct((2, b, t, d), BF16)
    two_f32 = jax.ShapeDtypeStruct((2, b, t, d), F32)
    return pl.pallas_call(
        functools.partial(_rwkv_pre_kernel, tm=tm, nt=nt),
        grid=(b, nt),
        in_specs=[
            tile,
            pl.BlockSpec((1, 8, d), lambda i, j: (i, jnp.maximum(j * r8 - 1, 0), 0)),
            pl.BlockSpec((1, 8, d), lambda i, j: (i, jnp.minimum((j + 1) * r8, nb8 - 1), 0)),
            pl.BlockSpec((1, 6, d), mod_map),
            _const_spec(vec.shape), _const_spec(mu.shape), _const_spec(wrkv.shape),
            _const_spec(g1.shape), _const_spec(g2.shape), _const_spec(w1c.shape),
            _const_spec(w2p.shape), _const_spec(a1c.shape), _const_spec(a2p.shape),
            _const_spec(hsel.shape), _const_spec(hselt.shape),
        ],
        out_specs=[tile, tile, tile, tile2, tile2, tile2, tile, tile],
        out_shape=[one, one, one, two_f32, two, two, one, one],
        compiler_params=_params(("arbitrary", "arbitrary")),
        name="rwkv_pre",
    )(x, x, x, mod, vec, mu, wrkv, g1, g2, w1c, w2p, a1c, a2p, hsel, hselt)


def _wkv_chains(chains, bd_mask):
    c = CHUNK
    n = BLOCK_LANES
    idx = range(len(chains))
    rev = [ch["reverse"] for ch in chains]
    row_t = lax.broadcasted_iota(jnp.int32, (c, n), 0)
    col_s = lax.broadcasted_iota(jnp.int32, (c, n), 1) % c
    strict = {False: row_t > col_s, True: row_t < col_s}
    incl = {False: row_t >= col_s, True: row_t <= col_s}
    zeros = jnp.zeros((c, n), F32)

    def blockdiag(z):
        return jnp.concatenate([_bf(z)] * HEADS_PER_BLOCK, axis=0) * bd_mask

    def chunk_cumsum(x, reverse):
        k = 1
        while k < c:
            if reverse:
                x = x + jnp.where(row_t < c - k, pltpu.roll(x, c - k, 0), 0.0)
            else:
                x = x + jnp.where(row_t >= k, pltpu.roll(x, k, 0), 0.0)
            k *= 2
        return x

    logd = [ch["logd"] for ch in chains]
    cum = [chunk_cumsum(logd[i], rev[i]) for i in idx]
    tot = [cum[i][0:1] if rev[i] else cum[i][c - 1:c] for i in idx]
    e_in = [jnp.exp(cum[i]) for i in idx]
    e_ex = [jnp.exp(cum[i] - logd[i]) for i in idx]
    e_inv = [jnp.exp(-cum[i]) for i in idx]
    e_bar = [jnp.exp(tot[i] - cum[i]) for i in idx]
    p_c = [jnp.exp(tot[i]) for i in idx]

    a_t = [-chains[i]["kkn"] * e_ex[i] for i in idx]
    r_t = [chains[i]["r"] * e_in[i] for i in idx]
    b_t = [chains[i]["bd"] * e_inv[i] for i in idx]
    k_t = [chains[i]["kd"] * e_inv[i] for i in idx]
    b_bar = [chains[i]["bd"] * e_bar[i] for i in idx]
    k_bar = [chains[i]["kd"] * e_bar[i] for i in idx]

    ycat = [jnp.concatenate([blockdiag(b_t[i]), blockdiag(k_t[i])], axis=0) for i in idx]
    g = [_dot_nt(_bf(jnp.concatenate([a_t[i], r_t[i]], axis=0)), ycat[i]) for i in idx]
    a_m = [jnp.where(strict[rev[i]], g[i][:c, :n], 0.0) for i in idx]
    ak_b = [_bf(jnp.where(strict[rev[i]], g[i][:c, n:], 0.0)) for i in idx]
    rb_b = [_bf(jnp.where(incl[rev[i]], g[i][c:, :n], 0.0)) for i in idx]
    rk_b = [_bf(jnp.where(incl[rev[i]], g[i][c:, n:], 0.0)) for i in idx]

    vst = [blockdiag(chains[i]["v"]) for i in idx]
    zk = [_dot(jnp.concatenate([ak_b[i], rk_b[i]], axis=0), vst[i]) for i in idx]
    z = [zk[i][:c] for i in idx]

    nm = a_m
    q = [_dot(_bf(a_m[i]), blockdiag(a_m[i])) for i in idx]
    levels = CHUNK.bit_length() - 2
    for j in range(levels):
        qb = [_bf(q[i]) for i in idx]
        if j + 1 < levels:
            both = [_dot(qb[i], jnp.concatenate([blockdiag(nm[i]), blockdiag(q[i])], axis=1)) for i in idx]
            nm = [nm[i] + q[i] + both[i][:, :n] for i in idx]
            q = [both[i][:, n:] for i in idx]
        else:
            nm = [nm[i] + q[i] + _dot(qb[i], blockdiag(nm[i])) for i in idx]

    t_az = [jnp.concatenate([a_t[i], z[i]], axis=1)
            + _dot(_bf(nm[i]), jnp.concatenate([blockdiag(a_t[i]), blockdiag(z[i])], axis=1)) for i in idx]
    a_hat = [t_az[i][:, :n] for i in idx]
    tz = [t_az[i][:, n:] for i in idx]
    rb_at = [_dot(rb_b[i], jnp.concatenate([blockdiag(a_hat[i]), blockdiag(tz[i])], axis=1)) for i in idx]
    r_hat = [r_t[i] + rb_at[i][:, :n] for i in idx]
    y0 = [rb_at[i][:, n:] + zk[i][c:] for i in idx]

    atz_t = [_bf(jnp.transpose(jnp.concatenate([a_hat[i], tz[i]], axis=0))) for i in idx]
    bb2 = [_bf(jnp.concatenate([jnp.concatenate([b_bar[i], zeros], axis=1),
                                jnp.concatenate([zeros, b_bar[i]], axis=1)], axis=0)) for i in idx]
    mn = [_dot(atz_t[i], bb2[i]) for i in idx]
    m_t = [_bf(mn[i][:, :n]) * bd_mask for i in idx]
    vk = []
    for p in range(0, len(chains), 2):
        assert chains[p]["v_first"] and not chains[p + 1]["v_first"]
        kk2 = _bf(jnp.concatenate([jnp.concatenate([k_bar[p], zeros], axis=1),
                                   jnp.concatenate([zeros, k_bar[p + 1]], axis=1)], axis=0))
        both = _dot(_bf(chains[p]["vt"]), kk2)
        vk += [both[:, :n], both[:, n:]]
    bd_mask_f = bd_mask.astype(F32)
    n0_bd = [(mn[i][:, n:] + vk[i]) * bd_mask_f for i in idx]
    n0 = [sum(x[hb * c:(hb + 1) * c] for hb in range(1, HEADS_PER_BLOCK)) + x[0:c] for x in n0_bd]

    st = [chains[i]["state"] for i in idx]
    y = [_dot_nt(_bf(r_hat[i]), blockdiag(st[i])) + y0[i] for i in idx]
    new_state = [st[i] * p_c[i] + _dot(_bf(st[i]), m_t[i]) + n0[i] for i in idx]
    return list(zip(y, new_state))


def _wkv_kernel(*refs, nc, has_s0, want_state):
    (rf, vf, af, ldf, kf, bf_, rb, vb, ab, ldb, kb, bb) = refs[:12]
    pos = 12
    s0_ref = None
    if has_s0:
        s0_ref = refs[pos]
        pos += 1
    yf_ref, yb_ref = refs[pos], refs[pos + 1]
    pos += 2
    so_ref = None
    if want_state:
        so_ref = refs[pos]
        pos += 1
    st_ref = refs[pos]
    ci = pl.program_id(2)
    n = BLOCK_LANES

    @pl.when(ci == 0)
    def _():
        if has_s0:
            st_ref[...] = s0_ref[0]
        else:
            st_ref[...] = jnp.zeros(st_ref.shape, F32)

    rowb = lax.broadcasted_iota(jnp.int32, (n, n), 0) // HEAD_DIM
    colb = lax.broadcasted_iota(jnp.int32, (n, n), 1) // HEAD_DIM
    bd_mask = _bf(jnp.where(rowb == colb, 1.0, 0.0))

    chains = []
    for qi in range(WKV_BLOCKS_PER_STEP):
        ln = slice(qi * n, (qi + 1) * n)
        f32 = lambda ref, *ix: ref[ix + (slice(None), ln)].astype(F32)
        v_f = f32(vf, 0)
        v_b = f32(vb, 0)
        vt = jnp.transpose(jnp.concatenate([v_f, v_b], axis=0))
        chains.append(dict(r=f32(rf, 0), v=v_f, kkn=f32(af, 0), logd=ldf[0, 0, :, ln], kd=f32(kf, 0, 0),
                           bd=f32(bf_, 0, 0), vt=vt, v_first=True, state=st_ref[0, :, ln], reverse=False))
        chains.append(dict(r=f32(rb, 0), v=v_b, kkn=f32(ab, 0), logd=ldb[0, 0, :, ln], kd=f32(kb, 0, 0),
                           bd=f32(bb, 0, 0), vt=vt, v_first=False, state=st_ref[1, :, ln], reverse=True))
    res = _wkv_chains(chains, bd_mask)
    finals = []
    for qi in range(WKV_BLOCKS_PER_STEP):
        ln = slice(qi * n, (qi + 1) * n)
        (y_f, s_f), (y_b, s_b) = res[2 * qi], res[2 * qi + 1]
        yf_ref[0, :, ln] = _bf(y_f)
        yb_ref[0, :, ln] = _bf(y_b)
        st_ref[0, :, ln] = s_f
        st_ref[1, :, ln] = s_b
        finals.append((s_f, s_b))

    if want_state:
        @pl.when(ci == nc - 1)
        def _():
            for qi, pair in enumerate(finals):
                for d, s in enumerate(pair):
                    for i in range(HEADS_PER_BLOCK):
                        so_ref[0, d, qi * HEADS_PER_BLOCK + i] = s[:, i * HEAD_DIM:(i + 1) * HEAD_DIM]


def _wkv_call(r, v, kkn, logd, kd, bd, s0_bd, want_state):
    b, t, d = r.shape
    c = CHUNK
    nc = t // c
    n = BLOCK_LANES * WKV_BLOCKS_PER_STEP
    nq = d // n
    fwd = pl.BlockSpec((1, c, n), lambda i, q, j: (i, j, q))
    bwd = pl.BlockSpec((1, c, n), lambda i, q, j: (i, nc - 1 - j, q))
    fwd2 = pl.BlockSpec((1, 1, c, n), lambda i, q, j: (0, i, j, q))
    bwd2 = pl.BlockSpec((1, 1, c, n), lambda i, q, j: (1, i, nc - 1 - j, q))
    in_specs = [fwd, fwd, fwd, fwd2, fwd2, fwd2, bwd, bwd, bwd, bwd2, bwd2, bwd2]
    args = [r, v, kkn, logd, kd, bd, r, v, kkn, logd, kd, bd]
    has_s0 = s0_bd is not None
    if has_s0:
        in_specs.append(pl.BlockSpec((1, 2, HEAD_DIM, n), lambda i, q, j: (i, 0, 0, q)))
        args.append(s0_bd)
    out_specs = [fwd, bwd]
    out_shape = [jax.ShapeDtypeStruct((b, t, d), BF16), jax.ShapeDtypeStruct((b, t, d), BF16)]
    if want_state:
        out_specs.append(pl.BlockSpec((1, 2, n // HEAD_DIM, HEAD_DIM, HEAD_DIM),
                                      lambda i, q, j: (i, 0, q, 0, 0)))
        out_shape.append(jax.ShapeDtypeStruct((b, 2, d // HEAD_DIM, HEAD_DIM, HEAD_DIM), F32))
    return pl.pallas_call(
        functools.partial(_wkv_kernel, nc=nc, has_s0=has_s0, want_state=want_state),
        grid=(b, nq, nc),
        in_specs=in_specs,
        out_specs=out_specs,
        out_shape=out_shape,
        scratch_shapes=[pltpu.VMEM((2, HEAD_DIM, n), F32)],
        compiler_params=_params(("arbitrary", "arbitrary", "arbitrary")),
        name="wkv_scan",
    )(*args)


def _rwkv_post_kernel(x_ref, yf_ref, yb_ref, bonus_ref, gate_ref, mod_ref, vec_ref, wo_ref,
                      hsel_ref, hselt_ref, o_ref):
    hsel = hsel_ref[...]
    hselt = hselt_ref[...]
    vec = vec_ref[...]
    gn_g, gn_b = vec[0:1], vec[1:2]
    gt = mod_ref[0][2:3]
    y = yf_ref[0].astype(F32) + yb_ref[0].astype(F32)
    inv = 1.0 / HEAD_DIM
    mean = _head_sum(y, hsel, hselt) * inv
    yc = y - mean
    var = _head_sum(yc * yc, hsel, hselt) * inv
    yn = yc * lax.rsqrt(var + GN_EPS) * gn_g + gn_b + bonus_ref[0].astype(F32)
    out = _dot(_bf(yn * gate_ref[0].astype(F32)), wo_ref[...])
    o_ref[0] = x_ref[0] + gt * out


def _rwkv_post_call(x, yf, yb, bonus, gate, mod, per_batch, vec, wo, hsel, hselt):
    b, t, d = x.shape
    tm = min(512, t)
    mod_map = (lambda i, j: (i, 0, 0)) if per_batch else (lambda i, j: (0, 0, 0))
    tile = pl.BlockSpec((1, tm, d), lambda i, j: (i, j, 0))
    return pl.pallas_call(
        _rwkv_post_kernel,
        grid=(b, t // tm),
        in_specs=[tile, tile, tile, tile, tile, pl.BlockSpec((1, 6, d), mod_map),
                  _const_spec(vec.shape), _const_spec(wo.shape),
                  _const_spec(hsel.shape), _const_spec(hselt.shape)],
        out_specs=tile,
        out_shape=jax.ShapeDtypeStruct((b, t, d), F32),
        compiler_params=_params(("arbitrary", "arbitrary")),
        name="rwkv_post",
    )(x, yf, yb, bonus, gate, mod, vec, wo, hsel, hselt)


def _conv_kernel(x_ref, mod_ref, vec_ref, pw1_ref, pw1b_ref, dw_ref, pw2_ref, o_ref,
                 pad_ref, cv_ref, *, tm, seg):
    d = x_ref.shape[-1]
    nlb = d // LANE
    nseg = tm // seg
    mod = mod_ref[0]
    shift, scale, gt = mod[0:1], mod[1:2], mod[2:3]
    vec = vec_ref[...]
    g, dw_b, ln_g, ln_b, pw2_b = vec[0:1], vec[1:2], vec[2:3], vec[3:4], vec[4:5]
    x = x_ref[0]
    h = _rms_mod(x, g, shift, scale)
    u = _dot(_bf(h), pw1_ref[...]) + pw1b_ref[...]
    u = u[:, :d] * jax.nn.sigmoid(u[:, d:])

    zpad = jnp.zeros((nseg, CONV_PAD, LANE), F32)
    for lb in range(nlb):
        pad_ref[:, lb, 0:CONV_PAD, :] = zpad
        pad_ref[:, lb, CONV_PAD + seg:2 * CONV_PAD + seg, :] = zpad
        pad_ref[:, lb, CONV_PAD:CONV_PAD + seg, :] = u[:, lb * LANE:(lb + 1) * LANE].reshape(nseg, seg, LANE)

    base = CONV_PAD - CONV_WIDTH // 2

    def body(idx, carry):
        s = idx // nlb
        lb = idx % nlb
        acc = jnp.zeros((seg, LANE), F32)
        for j in range(CONV_WIDTH):
            acc = acc + pad_ref[s, lb, base + j:base + j + seg, :] * dw_ref[j, lb]
        cv_ref[s, lb] = acc
        return carry

    lax.fori_loop(0, nseg * nlb, body, 0, unroll=4)

    cv = jnp.concatenate([cv_ref[:, lb].reshape(tm, LANE) for lb in range(nlb)], axis=1) + dw_b
    mu = jnp.mean(cv, axis=-1, keepdims=True)
    cc = cv - mu
    var = jnp.mean(cc * cc, axis=-1, keepdims=True)
    z = cc * lax.rsqrt(var + LN_EPS) * ln_g + ln_b
    z = z * jax.nn.sigmoid(z)
    out = _dot(_bf(z), pw2_ref[...]) + pw2_b
    o_ref[0] = x + gt * out


def _conv_call(x, mod, per_batch, seg, vec, pw1, pw1b, dw, pw2):
    b, t, d = x.shape
    tm = min(512, t)
    seg = min(seg, tm)
    nlb = d // LANE
    mod_map = (lambda i, j: (i, 0, 0)) if per_batch else (lambda i, j: (0, 0, 0))
    tile = pl.BlockSpec((1, tm, d), lambda i, j: (i, j, 0))
    return pl.pallas_call(
        functools.partial(_conv_kernel, tm=tm, seg=seg),
        grid=(b, t // tm),
        in_specs=[tile, pl.BlockSpec((1, 6, d), mod_map), _const_spec(vec.shape),
                  _const_spec(pw1.shape), _const_spec(pw1b.shape), _const_spec(dw.shape),
                  _const_spec(pw2.shape)],
        out_specs=tile,
        out_shape=jax.ShapeDtypeStruct((b, t, d), F32),
        scratch_shapes=[pltpu.VMEM((tm // seg, nlb, seg + 2 * CONV_PAD, LANE), F32),
                        pltpu.VMEM((tm // seg, nlb, seg, LANE), F32)],
        compiler_params=_params(("arbitrary", "arbitrary")),
        name="conv_module",
    )(x, mod, vec, pw1, pw1b, dw, pw2)


def _router_gates_t(h, rwt_ref, rb_ref):
    h_hi, h_lo = _split(h)
    w_hi, w_lo = _split(rwt_ref[...])
    n_e = w_hi.shape[0]
    hh = _dot_nt(jnp.concatenate([w_hi, w_lo], axis=0), h_hi)
    logits = hh[:n_e] + hh[n_e:] + _dot_nt(w_hi, h_lo) + rb_ref[...]
    m = jnp.max(logits, axis=0, keepdims=True)
    ex = jnp.exp(logits - m)
    p = ex / jnp.sum(ex, axis=0, keepdims=True)
    rows = [p[e:e + 1] for e in range(N_EXPERT_GROUPS * EXPERTS_PER_GROUP)]
    scores = []
    for gi in range(N_EXPERT_GROUPS):
        a, b, c, dd = rows[4 * gi:4 * gi + 4]
        hi1, lo1 = jnp.maximum(a, b), jnp.minimum(a, b)
        hi2, lo2 = jnp.maximum(c, dd), jnp.minimum(c, dd)
        scores.append(jnp.maximum(hi1, hi2) + jnp.maximum(jnp.minimum(hi1, hi2), jnp.maximum(lo1, lo2)))
    best = scores[0]
    sel = jnp.zeros(best.shape, jnp.int32)
    for gi in range(1, N_EXPERT_GROUPS):
        better = scores[gi] > best
        sel = jnp.where(better, gi, sel)
        best = jnp.where(better, scores[gi], best)
    qs = []
    for j in range(EXPERTS_PER_GROUP):
        qj = jnp.zeros(best.shape, F32)
        for gi in range(N_EXPERT_GROUPS):
            qj = jnp.where(sel == gi, rows[4 * gi + j], qj)
        qs.append(qj)

    def argmax4(vals):
        bv, bi = vals[0], jnp.zeros(best.shape, jnp.int32)
        for j in range(1, EXPERTS_PER_GROUP):
            better = vals[j] > bv
            bi = jnp.where(better, j, bi)
            bv = jnp.where(better, vals[j], bv)
        return bv, bi

    v1, i1 = argmax4(qs)
    v2, i2 = argmax4([jnp.where(i1 == j, -1.0, qs[j]) for j in range(EXPERTS_PER_GROUP)])
    den = v1 + v2
    w1, w2 = v1 / den, v2 / den
    gates = []
    for gi in range(N_EXPERT_GROUPS):
        rows_g = [jnp.where(sel == gi, jnp.where(i1 == j, w1, 0.0) + jnp.where(i2 == j, w2, 0.0), 0.0)
                  for j in range(EXPERTS_PER_GROUP)]
        gates.append(jnp.concatenate(rows_g, axis=0))
    return gates, sel


def _moe_kernel(x_ref, mod_ref, vec_ref, rwt_ref, rb_ref, wg_ref, wu_ref, wd_ref, o_ref,
                h_ref, gt_ref, rk_ref, rc_ref, early_ref, xg_ref, yg_ref, wr_ref, acc_ref, nblk_ref,
                *, n_exp, final):
    e = pl.program_id(2)
    tm = x_ref.shape[1]
    g = e // EXPERTS_PER_GROUP
    j = e % EXPERTS_PER_GROUP
    sub8 = 8

    @pl.when((pl.program_id(0) == 0) & (pl.program_id(1) == 0) & (e == 0))
    def _():
        early_ref[...] = _bf(jnp.where(lax.broadcasted_iota(jnp.int32, (tm, tm), 0)
                                       < lax.broadcasted_iota(jnp.int32, (tm, tm), 1), 1.0, 0.0))

    def block_rows(blk, size):
        return pl.ds(pl.multiple_of(blk * MOE_ROWS, MOE_ROWS), size)

    def block_base(blk):
        return (blk * MOE_ROWS).astype(F32)

    @pl.when(e == 0)
    def _():
        mod = mod_ref[0]
        h = _rms_mod(x_ref[0], vec_ref[0:1], mod[3:4], mod[4:5])
        h_ref[...] = _bf(h)
        gates, sel = _router_gates_t(h, rwt_ref, rb_ref)
        zero4 = jnp.zeros((sub8 - EXPERTS_PER_GROUP, tm), F32)
        for gi in range(N_EXPERT_GROUPS):
            gt_ref[gi] = jnp.concatenate([gates[gi], zero4], axis=0)
        chosen = [jnp.where(sel == gi, 1.0, 0.0) for gi in range(N_EXPERT_GROUPS)]
        selmat = _bf(jnp.concatenate(chosen + [jnp.zeros((sub8 - N_EXPERT_GROUPS, tm), F32)], axis=0))
        prefix = _dot(selmat, early_ref[...])
        rk = jnp.where(selmat > 0, prefix, -1.0)
        rk_ref[...] = rk
        rc_ref[...] = jnp.transpose(jnp.concatenate([rk, jnp.zeros((LANE - sub8, tm), F32)], axis=0))
        acc_ref[...] = jnp.zeros(acc_ref.shape, F32)

    @pl.when(j == 0)
    def _():
        sub_g = lax.broadcasted_iota(jnp.int32, (sub8, tm), 0)
        rank_row = jnp.sum(jnp.where(sub_g == g, rk_ref[...], 0.0), axis=0, keepdims=True)
        cnt = jnp.sum(jnp.where(rank_row >= 0.0, 1.0, 0.0)).astype(jnp.int32)
        nblk = (cnt + (MOE_ROWS - 1)) // MOE_ROWS
        nblk_ref[0] = nblk
        nblk_ref[1] = (cnt + (MOE_ROW_STEP - 1)) // MOE_ROW_STEP
        g_hi, g_lo = _split(gt_ref[g])
        slot = lax.broadcasted_iota(jnp.int32, (MOE_ROWS, tm), 0).astype(F32)

        def compact(blk, carry):
            rows = block_rows(blk, MOE_ROWS)
            pb = _bf(jnp.where(rank_row - block_base(blk) == slot, 1.0, 0.0))
            xg_ref[rows, :] = _bf(_dot(pb, h_ref[...]))
            wr_ref[rows, :] = _dot_nt(pb, g_hi) + _dot_nt(pb, g_lo)
            yg_ref[rows, :] = jnp.zeros((MOE_ROWS, yg_ref.shape[1]), F32)
            return carry

        lax.fori_loop(0, nblk, compact, 0)

    def expert(blk, size):
        rows = block_rows(blk, size)
        xb = xg_ref[rows, :]
        hg = _dot(xb, wg_ref[0, 0])
        hu = _dot(xb, wu_ref[0, 0])
        lane8 = lax.broadcasted_iota(jnp.int32, (size, sub8), 1)
        wcol = jnp.sum(jnp.where(lane8 == j, wr_ref[rows, :], 0.0), axis=1, keepdims=True)
        he = hg * jax.nn.sigmoid(hg) * hu * wcol
        yg_ref[rows, :] += _dot(_bf(he), wd_ref[0, 0])

    n_used = nblk_ref[0]
    n_steps = nblk_ref[1]

    assert xg_ref.shape[0] >= 2 * MOE_ROWS
    steps_per_block = MOE_ROWS // MOE_ROW_STEP
    trimmed = range(steps_per_block + 1, 2 * steps_per_block - 1)
    for ns in trimmed:
        @pl.when(n_steps == ns)
        def _(ns=ns):
            expert(0, ns * MOE_ROW_STEP)

    @pl.when((n_steps < trimmed.start) | (n_steps >= trimmed.stop))
    def _():
        n_pairs = lax.shift_right_logical(n_used, 1)

        def expert_pair(p, carry):
            expert(2 * p, 2 * MOE_ROWS)
            return carry

        lax.fori_loop(0, n_pairs, expert_pair, 0)

        @pl.when(n_used - 2 * n_pairs == 1)
        def _():
            expert(n_used - 1, MOE_ROWS)

    @pl.when(j == EXPERTS_PER_GROUP - 1)
    def _():
        lane_g = lax.broadcasted_iota(jnp.int32, (tm, LANE), 1)
        rank_col = jnp.sum(jnp.where(lane_g == g, rc_ref[...], 0.0), axis=1, keepdims=True)
        slot_l = lax.broadcasted_iota(jnp.int32, (tm, MOE_ROWS), 1).astype(F32)

        def spread(blk, carry):
            pt = _bf(jnp.where(rank_col - block_base(blk) == slot_l, 1.0, 0.0))
            acc_ref[...] += _dot(pt, _bf(yg_ref[block_rows(blk, MOE_ROWS), :]))
            return carry

        slot2 = lax.broadcasted_iota(jnp.int32, (tm, 2 * MOE_ROWS), 1).astype(F32)

        @pl.when(n_used == 2)
        def _():
            pt2 = _bf(jnp.where(rank_col == slot2, 1.0, 0.0))
            acc_ref[...] += _dot(pt2, _bf(yg_ref[0:2 * MOE_ROWS, :]))

        @pl.when(n_used != 2)
        def _():
            lax.fori_loop(0, n_used, spread, 0)

    @pl.when(e == n_exp - 1)
    def _():
        out = x_ref[0] + mod_ref[0][5:6] * acc_ref[...]
        if final:
            out = out * lax.rsqrt(jnp.mean(out * out, axis=-1, keepdims=True) + RMS_EPS) * vec_ref[1:2]
        o_ref[0] = out


def _moe_call(x, mod, per_batch, vec, rwt, rb, wg, wu, wd, layer, final):
    b, t, d = x.shape
    tm = min(1024, t)
    cap = max(-(-tm // MOE_ROWS), 2) * MOE_ROWS
    _, n_exp, _, de = wg.shape
    mod_map = (lambda i, j, e: (i, 0, 0)) if per_batch else (lambda i, j, e: (0, 0, 0))
    tile = pl.BlockSpec((1, tm, d), lambda i, j, e: (i, j, 0))
    cst = lambda shape: pl.BlockSpec(shape, lambda i, j, e: (0,) * len(shape))
    return pl.pallas_call(
        functools.partial(_moe_kernel, n_exp=n_exp, final=final),
        grid=(b, t // tm, n_exp),
        in_specs=[tile, pl.BlockSpec((1, 6, d), mod_map), cst(vec.shape), cst(rwt.shape), cst(rb.shape),
                  pl.BlockSpec((1, 1, d, de), lambda i, j, e: (layer, e, 0, 0)),
                  pl.BlockSpec((1, 1, d, de), lambda i, j, e: (layer, e, 0, 0)),
                  pl.BlockSpec((1, 1, de, d), lambda i, j, e: (layer, e, 0, 0))],
        out_specs=tile,
        out_shape=jax.ShapeDtypeStruct((b, t, d), F32),
        scratch_shapes=[
            pltpu.VMEM((tm, d), BF16),
            pltpu.VMEM((N_EXPERT_GROUPS, 8, tm), F32),
            pltpu.VMEM((8, tm), F32),
            pltpu.VMEM((tm, LANE), F32),
            pltpu.VMEM((tm, tm), BF16),
            pltpu.VMEM((cap, d), BF16),
            pltpu.VMEM((cap, d), F32),
            pltpu.VMEM((cap, 8), F32),
            pltpu.VMEM((tm, d), F32),
            pltpu.SMEM((2,), jnp.int32),
        ],
        compiler_params=_params(("arbitrary", "arbitrary", "arbitrary")),
        name="grouped_moe",
    )(x, mod, vec, rwt, rb, wg, wu, wd)


def _pad_rows(a, rows):
    return jnp.concatenate([a, jnp.zeros((rows - a.shape[0],) + a.shape[1:], a.dtype)], axis=0)


def _lane_stacked_state(s0):
    b, two, h, n, _ = s0.shape
    return jnp.swapaxes(s0, 2, 3).reshape(b, two, n, h * n)


def kernel(x_prompt, x_sample, state_rwkv, c, c_ctx, norm_g, ada_w, ada_b, final_g, rwkv_mu, rwkv_w_rkv, rwkv_w_o, rwkv_w0, rwkv_w1, rwkv_w2, rwkv_a0, rwkv_a1, rwkv_a2, rwkv_g1, rwkv_g2, rwkv_k_k, rwkv_k_a, rwkv_r_k, rwkv_gn_g, rwkv_gn_b, conv_pw1, conv_pw1_b, conv_dw, conv_dw_b, conv_ln_g, conv_ln_b, conv_pw2, conv_pw2_b, router_w, router_b, moe_w_gate, moe_w_up, moe_w_down):
    d = x_prompt.shape[-1]
    depth = ada_w.shape[0]
    n_heads = d // HEAD_DIM
    dec_b = c.shape[0]

    cond = _pad_rows(jnp.concatenate([c_ctx[None, :], c], axis=0), COND_ROWS)
    mod = _mod_call(cond, ada_w, ada_b).reshape(depth, COND_ROWS, 6, d)

    head_of_lane = jnp.arange(d) // HEAD_DIM
    hsel = (head_of_lane[:, None] == jnp.arange(LANE)[None, :]).astype(BF16)
    hselt = jnp.concatenate([hsel.T, hsel.T], axis=0)
    rwt = router_w.T
    rb = router_b[:, None]
    wg = _bf(moe_w_gate)
    wu = _bf(moe_w_up)
    wd = _bf(moe_w_down)

    def lora_pad(w2):
        z = jnp.zeros_like(w2[0])
        return _bf(jnp.stack([jnp.concatenate([w2[0], z], 0), jnp.concatenate([z, w2[1]], 0)]))

    groups = (
        dict(x=x_prompt, rows=slice(0, 1), per_batch=False, s0=None, want_state=True, seg=x_prompt.shape[1]),
        dict(x=x_sample, rows=slice(1, 1 + dec_b), per_batch=True, s0=state_rwkv, want_state=False, seg=GRID_W),
    )
    outs = []
    new_state = None
    for gr in groups:
        x = gr["x"]
        states = []
        for i in range(depth):
            m_i = mod[i, gr["rows"]]
            j = i // 2
            if i % 2 == 0:
                vec = jnp.stack([norm_g[i, 0], rwkv_k_k[j], rwkv_k_a[j], rwkv_r_k[j].reshape(d),
                                 rwkv_w0[j, 0], rwkv_w0[j, 1], rwkv_a0[j, 0], rwkv_a0[j, 1]])
                r, v, kkn, logd, kd, bd, gate, bonus = _rwkv_pre_call(
                    x, m_i, gr["per_batch"], vec, rwkv_mu[j], _bf(rwkv_w_rkv[j]), _bf(rwkv_g1[j]), _bf(rwkv_g2[j]),
                    _bf(jnp.concatenate([rwkv_w1[j, 0], rwkv_w1[j, 1]], axis=1)), lora_pad(rwkv_w2[j]),
                    _bf(jnp.concatenate([rwkv_a1[j, 0], rwkv_a1[j, 1]], axis=1)), lora_pad(rwkv_a2[j]),
                    hsel, hselt)
                s0_bd = None if gr["s0"] is None else _lane_stacked_state(gr["s0"][:, j])
                res = _wkv_call(r, v, kkn, logd, kd, bd, s0_bd, gr["want_state"])
                if gr["want_state"]:
                    states.append(res[2])
                vec = _pad_rows(jnp.stack([rwkv_gn_g[j], rwkv_gn_b[j]]), 8)
                x = _rwkv_post_call(x, res[0], res[1], bonus, gate, m_i, gr["per_batch"], vec,
                                    _bf(rwkv_w_o[j]), hsel, hselt)
            else:
                vec = _pad_rows(jnp.stack([norm_g[i, 0], conv_dw_b[j], conv_ln_g[j], conv_ln_b[j], conv_pw2_b[j]]), 8)
                dw = conv_dw[j].reshape(CONV_WIDTH, d // LANE, 1, LANE)
                x = _conv_call(x, m_i, gr["per_batch"], gr["seg"], vec, _bf(conv_pw1[j]),
                               conv_pw1_b[j][None, :], dw, _bf(conv_pw2[j]))
            vec = _pad_rows(jnp.stack([norm_g[i, 1], final_g]), 8)
            xm = x if gr["per_batch"] else x.reshape(1, -1, d)
            xm = _moe_call(xm, m_i, gr["per_batch"], vec, rwt, rb, wg, wu, wd, i, final=(i == depth - 1))
            x = xm.reshape(x.shape)
        outs.append(x)
        if gr["want_state"]:
            new_state = jnp.stack(states, axis=1)
    return (outs[0], outs[1], new_state)
```

```python
import functools

import jax
import jax.numpy as jnp
from jax import lax
from jax.experimental import pallas as pl
from jax.experimental.pallas import tpu as pltpu

F32 = jnp.float32
BF16 = jnp.bfloat16

HEAD_DIM = 64
HEADS_PER_BLOCK = 2
BLOCK_LANES = HEAD_DIM * HEADS_PER_BLOCK
CHUNK = 64
WKV_BLOCKS_PER_STEP = 8
GRID_W = 64
MOE_ROWS = 256
MOE_ROW_STEP = 64
CONV_WIDTH = 31
CONV_PAD = 16
N_EXPERT_GROUPS = 4
EXPERTS_PER_GROUP = 4
LANE = 128
COND_ROWS = 16
RMS_EPS = 1e-6
LN_EPS = 1e-5
GN_EPS = 64e-5
NEG_EXP_M_HALF = -0.6065306597126334
VMEM_LIMIT_BYTES = 56 * 1024 * 1024


def _bf(x):
    return x.astype(BF16)


def _dot(a, b):
    return jnp.dot(a, b, preferred_element_type=F32)


def _dot_nt(a, b):
    return lax.dot_general(a, b, (((1,), (1,)), ((), ())), preferred_element_type=F32)


def _split(x):
    hi = _bf(x)
    lo = _bf(x - hi.astype(F32))
    return hi, lo


def _rms_mod(x, g, shift, scale):
    y = x * lax.rsqrt(jnp.mean(x * x, axis=-1, keepdims=True) + RMS_EPS) * g
    return y * (1.0 + scale) + shift


def _head_sum(z, hsel, hselt2):
    hi, lo = _split(_dot(_bf(z), hsel))
    return _dot(jnp.concatenate([hi, lo], axis=1), hselt2)


def _params(sem):
    return pltpu.CompilerParams(dimension_semantics=sem, vmem_limit_bytes=VMEM_LIMIT_BYTES)


def _const_spec(shape):
    nd = len(shape)
    return pl.BlockSpec(shape, lambda *_: (0,) * nd, pipeline_mode=pl.Buffered(1))


def _mod_kernel(c_ref, w_ref, b_ref, o_ref):
    c = c_ref[...]
    s = c * jax.nn.sigmoid(c)
    o_ref[0] = jnp.dot(s, w_ref[0], preferred_element_type=F32,
                       precision=lax.Precision.HIGHEST) + b_ref[0]


def _mod_call(cond, ada_w, ada_b):
    depth, d, n = ada_w.shape
    tn = 1536
    return pl.pallas_call(
        _mod_kernel,
        grid=(depth, n // tn),
        in_specs=[
            pl.BlockSpec((COND_ROWS, d), lambda l, j: (0, 0)),
            pl.BlockSpec((1, d, tn), lambda l, j: (l, 0, j)),
            pl.BlockSpec((1, 1, tn), lambda l, j: (l, 0, j)),
        ],
        out_specs=pl.BlockSpec((1, COND_ROWS, tn), lambda l, j: (l, 0, j)),
        out_shape=jax.ShapeDtypeStruct((depth, COND_ROWS, n), F32),
        compiler_params=_params(("arbitrary", "arbitrary")),
        name="adaln_mod",
    )(cond, ada_w, ada_b.reshape(depth, 1, n))


def _rwkv_pre_kernel(x_ref, xp_ref, xn_ref, mod_ref, vec_ref, mu_ref, wrkv_ref, g1_ref, g2_ref,
                     w1_ref, w2_ref, a1_ref, a2_ref, hsel_ref, hselt_ref,
                     r_ref, v_ref, kkn_ref, logd_ref, kd_ref, bd_ref, gate_ref, bonus_ref,
                     *, tm, nt):
    t = pl.program_id(1)
    mod = mod_ref[0]
    shift, scale = mod[0:1], mod[1:2]
    vec = vec_ref[...]
    g, k_k, k_a, r_k = vec[0:1], vec[1:2], vec[2:3], vec[3:4]
    hsel = hsel_ref[...]
    hselt = hselt_ref[...]

    h = _rms_mod(x_ref[0], g, shift, scale)
    h_prev = _rms_mod(xp_ref[0], g, shift, scale)[7:8]
    h_next = _rms_mod(xn_ref[0], g, shift, scale)[0:1]
    h_prev = jnp.where(t == 0, 0.0, h_prev)
    h_next = jnp.where(t == nt - 1, 0.0, h_next)
    row = lax.broadcasted_iota(jnp.int32, (tm, 1), 0)
    prev = jnp.where(row == 0, h_prev, pltpu.roll(h, 1, 0))
    nxt = jnp.where(row == tm - 1, h_next, pltpu.roll(h, tm - 1, 0))
    xx = 0.5 * (prev + nxt) - h
    mu = mu_ref[...]

    def mix(i):
        return _bf(h + xx * mu[i:i + 1])

    r = _dot(mix(0), wrkv_ref[0])
    k = _dot(mix(2), wrkv_ref[1])
    v = _dot(mix(3), wrkv_ref[2])
    gate = _dot(_bf(jax.nn.sigmoid(_dot(mix(5), g1_ref[...]))), g2_ref[...])
    tw = _bf(jnp.tanh(_dot(mix(1), w1_ref[...])))
    ta = _bf(_dot(mix(4), a1_ref[...]))

    kk = k * k_k
    kkn = kk * lax.rsqrt(jnp.maximum(_head_sum(kk * kk, hsel, hselt), 1e-24))
    r_ref[0] = _bf(r)
    v_ref[0] = _bf(v)
    kkn_ref[0] = _bf(kkn)
    gate_ref[0] = _bf(gate)

    ksum = None
    for d in range(2):
        u = vec[4 + d:5 + d] + _dot(tw, w2_ref[d])
        logd_ref[d, 0] = NEG_EXP_M_HALF * jax.nn.sigmoid(u)
        a = jax.nn.sigmoid(vec[6 + d:7 + d] + _dot(ta, a2_ref[d]))
        kd = k * (1.0 + (a - 1.0) * k_a)
        kd_ref[d, 0] = _bf(kd)
        bd_ref[d, 0] = _bf(kkn * a)
        ksum = kd if ksum is None else ksum + kd
    bonus_ref[0] = _bf(_head_sum(r * r_k * ksum, hsel, hselt) * v)


def _rwkv_pre_call(x, mod, per_batch, vec, mu, wrkv, g1, g2, w1c, w2p, a1c, a2p, hsel, hselt):
    b, t, d = x.shape
    tm = min(512, t)
    nt = t // tm
    r8 = tm // 8
    nb8 = t // 8
    mod_map = (lambda i, j: (i, 0, 0)) if per_batch else (lambda i, j: (0, 0, 0))
    tile = pl.BlockSpec((1, tm, d), lambda i, j: (i, j, 0))
    tile2 = pl.BlockSpec((2, 1, tm, d), lambda i, j: (0, i, j, 0))
    one = jax.ShapeDtypeStruct((b, t, d), BF16)
    two = jax.ShapeDtypeStruct((2, b, t, d), BF16)
    two_f32 = jax.ShapeDtypeStruct((2, b, t, d), F32)
    return pl.pallas_call(
        functools.partial(_rwkv_pre_kernel, tm=tm, nt=nt),
        grid=(b, nt),
        in_specs=[
            tile,
            pl.BlockSpec((1, 8, d), lambda i, j: (i, jnp.maximum(j * r8 - 1, 0), 0)),
            pl.BlockSpec((1, 8, d), lambda i, j: (i, jnp.minimum((j + 1) * r8, nb8 - 1), 0)),
            pl.BlockSpec((1, 6, d), mod_map),
            _const_spec(vec.shape), _const_spec(mu.shape), _const_spec(wrkv.shape),
            _const_spec(g1.shape), _const_spec(g2.shape), _const_spec(w1c.shape),
            _const_spec(w2p.shape), _const_spec(a1c.shape), _const_spec(a2p.shape),
            _const_spec(hsel.shape), _const_spec(hselt.shape),
        ],
        out_specs=[tile, tile, tile, tile2, tile2, tile2, tile, tile],
        out_shape=[one, one, one, two_f32, two, two, one, one],
        compiler_params=_params(("arbitrary", "arbitrary")),
        name="rwkv_pre",
    )(x, x, x, mod, vec, mu, wrkv, g1, g2, w1c, w2p, a1c, a2p, hsel, hselt)


def _wkv_chains(chains, bd_mask):
    c = CHUNK
    n = BLOCK_LANES
    idx = range(len(chains))
    rev = [ch["reverse"] for ch in chains]
    row_t = lax.broadcasted_iota(jnp.int32, (c, n), 0)
    col_s = lax.broadcasted_iota(jnp.int32, (c, n), 1) % c
    strict = {False: row_t > col_s, True: row_t < col_s}
    incl = {False: row_t >= col_s, True: row_t <= col_s}
    zeros = jnp.zeros((c, n), F32)

    def blockdiag(z):
        return jnp.concatenate([_bf(z)] * HEADS_PER_BLOCK, axis=0) * bd_mask

    def chunk_cumsum(x, reverse):
        k = 1
        while k < c:
            if reverse:
                x = x + jnp.where(row_t < c - k, pltpu.roll(x, c - k, 0), 0.0)
            else:
                x = x + jnp.where(row_t >= k, pltpu.roll(x, k, 0), 0.0)
            k *= 2
        return x

    logd = [ch["logd"] for ch in chains]
    cum = [chunk_cumsum(logd[i], rev[i]) for i in idx]
    tot = [cum[i][0:1] if rev[i] else cum[i][c - 1:c] for i in idx]
    e_in = [jnp.exp(cum[i]) for i in idx]
    e_ex = [jnp.exp(cum[i] - logd[i]) for i in idx]
    e_inv = [jnp.exp(-cum[i]) for i in idx]
    e_bar = [jnp.exp(tot[i] - cum[i]) for i in idx]
    p_c = [jnp.exp(tot[i]) for i in idx]

    a_t = [-chains[i]["kkn"] * e_ex[i] for i in idx]
    r_t = [chains[i]["r"] * e_in[i] for i in idx]
    b_t = [chains[i]["bd"] * e_inv[i] for i in idx]
    k_t = [chains[i]["kd"] * e_inv[i] for i in idx]
    b_bar = [chains[i]["bd"] * e_bar[i] for i in idx]
    k_bar = [chains[i]["kd"] * e_bar[i] for i in idx]

    ycat = [jnp.concatenate([blockdiag(b_t[i]), blockdiag(k_t[i])], axis=0) for i in idx]
    g = [_dot_nt(_bf(jnp.concatenate([a_t[i], r_t[i]], axis=0)), ycat[i]) for i in idx]
    a_m = [jnp.where(strict[rev[i]], g[i][:c, :n], 0.0) for i in idx]
    ak_b = [_bf(jnp.where(strict[rev[i]], g[i][:c, n:], 0.0)) for i in idx]
    rb_b = [_bf(jnp.where(incl[rev[i]], g[i][c:, :n], 0.0)) for i in idx]
    rk_b = [_bf(jnp.where(incl[rev[i]], g[i][c:, n:], 0.0)) for i in idx]

    vst = [blockdiag(chains[i]["v"]) for i in idx]
    zk = [_dot(jnp.concatenate([ak_b[i], rk_b[i]], axis=0), vst[i]) for i in idx]
    z = [zk[i][:c] for i in idx]

    nm = a_m
    q = [_dot(_bf(a_m[i]), blockdiag(a_m[i])) for i in idx]
    levels = CHUNK.bit_length() - 2
    for j in range(levels):
        qb = [_bf(q[i]) for i in idx]
        if j + 1 < levels:
            both = [_dot(qb[i], jnp.concatenate([blockdiag(nm[i]), blockdiag(q[i])], axis=1)) for i in idx]
            nm = [nm[i] + q[i] + both[i][:, :n] for i in idx]
            q = [both[i][:, n:] for i in idx]
        else:
            nm = [nm[i] + q[i] + _dot(qb[i], blockdiag(nm[i])) for i in idx]

    t_az = [jnp.concatenate([a_t[i], z[i]], axis=1)
            + _dot(_bf(nm[i]), jnp.concatenate([blockdiag(a_t[i]), blockdiag(z[i])], axis=1)) for i in idx]
    a_hat = [t_az[i][:, :n] for i in idx]
    tz = [t_az[i][:, n:] for i in idx]
    rb_at = [_dot(rb_b[i], jnp.concatenate([blockdiag(a_hat[i]), blockdiag(tz[i])], axis=1)) for i in idx]
    r_hat = [r_t[i] + rb_at[i][:, :n] for i in idx]
    y0 = [rb_at[i][:, n:] + zk[i][c:] for i in idx]

    atz_t = [_bf(jnp.transpose(jnp.concatenate([a_hat[i], tz[i]], axis=0))) for i in idx]
    bb2 = [_bf(jnp.concatenate([jnp.concatenate([b_bar[i], zeros], axis=1),
                                jnp.concatenate([zeros, b_bar[i]], axis=1)], axis=0)) for i in idx]
    mn = [_dot(atz_t[i], bb2[i]) for i in idx]
    m_t = [_bf(mn[i][:, :n]) * bd_mask for i in idx]
    vk = []
    for p in range(0, len(chains), 2):
        assert chains[p]["v_first"] and not chains[p + 1]["v_first"]
        kk2 = _bf(jnp.concatenate([jnp.concatenate([k_bar[p], zeros], axis=1),
                                   jnp.concatenate([zeros, k_bar[p + 1]], axis=1)], axis=0))
        both = _dot(_bf(chains[p]["vt"]), kk2)
        vk += [both[:, :n], both[:, n:]]
    bd_mask_f = bd_mask.astype(F32)
    n0_bd = [(mn[i][:, n:] + vk[i]) * bd_mask_f for i in idx]
    n0 = [sum(x[hb * c:(hb + 1) * c] for hb in range(1, HEADS_PER_BLOCK)) + x[0:c] for x in n0_bd]

    st = [chains[i]["state"] for i in idx]
    y = [_dot_nt(_bf(r_hat[i]), blockdiag(st[i])) + y0[i] for i in idx]
    new_state = [st[i] * p_c[i] + _dot(_bf(st[i]), m_t[i]) + n0[i] for i in idx]
    return list(zip(y, new_state))


def _wkv_kernel(*refs, nc, has_s0, want_state):
    (rf, vf, af, ldf, kf, bf_, rb, vb, ab, ldb, kb, bb) = refs[:12]
    pos = 12
    s0_ref = None
    if has_s0:
        s0_ref = refs[pos]
        pos += 1
    yf_ref, yb_ref = refs[pos], refs[pos + 1]
    pos += 2
    so_ref = None
    if want_state:
        so_ref = refs[pos]
        pos += 1
    st_ref = refs[pos]
    ci = pl.program_id(2)
    n = BLOCK_LANES

    @pl.when(ci == 0)
    def _():
        if has_s0:
            st_ref[...] = s0_ref[0]
        else:
            st_ref[...] = jnp.zeros(st_ref.shape, F32)

    rowb = lax.broadcasted_iota(jnp.int32, (n, n), 0) // HEAD_DIM
    colb = lax.broadcasted_iota(jnp.int32, (n, n), 1) // HEAD_DIM
    bd_mask = _bf(jnp.where(rowb == colb, 1.0, 0.0))

    chains = []
    for qi in range(WKV_BLOCKS_PER_STEP):
        ln = slice(qi * n, (qi + 1) * n)
        f32 = lambda ref, *ix: ref[ix + (slice(None), ln)].astype(F32)
        v_f = f32(vf, 0)
        v_b = f32(vb, 0)
        vt = jnp.transpose(jnp.concatenate([v_f, v_b], axis=0))
        chains.append(dict(r=f32(rf, 0), v=v_f, kkn=f32(af, 0), logd=ldf[0, 0, :, ln], kd=f32(kf, 0, 0),
                           bd=f32(bf_, 0, 0), vt=vt, v_first=True, state=st_ref[0, :, ln], reverse=False))
        chains.append(dict(r=f32(rb, 0), v=v_b, kkn=f32(ab, 0), logd=ldb[0, 0, :, ln], kd=f32(kb, 0, 0),
                           bd=f32(bb, 0, 0), vt=vt, v_first=False, state=st_ref[1, :, ln], reverse=True))
    res = _wkv_chains(chains, bd_mask)
    finals = []
    for qi in range(WKV_BLOCKS_PER_STEP):
        ln = slice(qi * n, (qi + 1) * n)
        (y_f, s_f), (y_b, s_b) = res[2 * qi], res[2 * qi + 1]
        yf_ref[0, :, ln] = _bf(y_f)
        yb_ref[0, :, ln] = _bf(y_b)
        st_ref[0, :, ln] = s_f
        st_ref[1, :, ln] = s_b
        finals.append((s_f, s_b))

    if want_state:
        @pl.when(ci == nc - 1)
        def _():
            for qi, pair in enumerate(finals):
                for d, s in enumerate(pair):
                    for i in range(HEADS_PER_BLOCK):
                        so_ref[0, d, qi * HEADS_PER_BLOCK + i] = s[:, i * HEAD_DIM:(i + 1) * HEAD_DIM]


def _wkv_call(r, v, kkn, logd, kd, bd, s0_bd, want_state):
    b, t, d = r.shape
    c = CHUNK
    nc = t // c
    n = BLOCK_LANES * WKV_BLOCKS_PER_STEP
    nq = d // n
    fwd = pl.BlockSpec((1, c, n), lambda i, q, j: (i, j, q))
    bwd = pl.BlockSpec((1, c, n), lambda i, q, j: (i, nc - 1 - j, q))
    fwd2 = pl.BlockSpec((1, 1, c, n), lambda i, q, j: (0, i, j, q))
    bwd2 = pl.BlockSpec((1, 1, c, n), lambda i, q, j: (1, i, nc - 1 - j, q))
    in_specs = [fwd, fwd, fwd, fwd2, fwd2, fwd2, bwd, bwd, bwd, bwd2, bwd2, bwd2]
    args = [r, v, kkn, logd, kd, bd, r, v, kkn, logd, kd, bd]
    has_s0 = s0_bd is not None
    if has_s0:
        in_specs.append(pl.BlockSpec((1, 2, HEAD_DIM, n), lambda i, q, j: (i, 0, 0, q)))
        args.append(s0_bd)
    out_specs = [fwd, bwd]
    out_shape = [jax.ShapeDtypeStruct((b, t, d), BF16), jax.ShapeDtypeStruct((b, t, d), BF16)]
    if want_state:
        out_specs.append(pl.BlockSpec((1, 2, n // HEAD_DIM, HEAD_DIM, HEAD_DIM),
                                      lambda i, q, j: (i, 0, q, 0, 0)))
        out_shape.append(jax.ShapeDtypeStruct((b, 2, d // HEAD_DIM, HEAD_DIM, HEAD_DIM), F32))
    return pl.pallas_call(
        functools.partial(_wkv_kernel, nc=nc, has_s0=has_s0, want_state=want_state),
        grid=(b, nq, nc),
        in_specs=in_specs,
        out_specs=out_specs,
        out_shape=out_shape,
        scratch_shapes=[pltpu.VMEM((2, HEAD_DIM, n), F32)],
        compiler_params=_params(("arbitrary", "arbitrary", "arbitrary")),
        name="wkv_scan",
    )(*args)


def _rwkv_post_kernel(x_ref, yf_ref, yb_ref, bonus_ref, gate_ref, mod_ref, vec_ref, wo_ref,
                      hsel_ref, hselt_ref, o_ref):
    hsel = hsel_ref[...]
    hselt = hselt_ref[...]
    vec = vec_ref[...]
    gn_g, gn_b = vec[0:1], vec[1:2]
    gt = mod_ref[0][2:3]
    y = yf_ref[0].astype(F32) + yb_ref[0].astype(F32)
    inv = 1.0 / HEAD_DIM
    mean = _head_sum(y, hsel, hselt) * inv
    yc = y - mean
    var = _head_sum(yc * yc, hsel, hselt) * inv
    yn = yc * lax.rsqrt(var + GN_EPS) * gn_g + gn_b + bonus_ref[0].astype(F32)
    out = _dot(_bf(yn * gate_ref[0].astype(F32)), wo_ref[...])
    o_ref[0] = x_ref[0] + gt * out


def _rwkv_post_call(x, yf, yb, bonus, gate, mod, per_batch, vec, wo, hsel, hselt):
    b, t, d = x.shape
    tm = min(512, t)
    mod_map = (lambda i, j: (i, 0, 0)) if per_batch else (lambda i, j: (0, 0, 0))
    tile = pl.BlockSpec((1, tm, d), lambda i, j: (i, j, 0))
    return pl.pallas_call(
        _rwkv_post_kernel,
        grid=(b, t // tm),
        in_specs=[tile, tile, tile, tile, tile, pl.BlockSpec((1, 6, d), mod_map),
                  _const_spec(vec.shape), _const_spec(wo.shape),
                  _const_spec(hsel.shape), _const_spec(hselt.shape)],
        out_specs=tile,
        out_shape=jax.ShapeDtypeStruct((b, t, d), F32),
        compiler_params=_params(("arbitrary", "arbitrary")),
        name="rwkv_post",
    )(x, yf, yb, bonus, gate, mod, vec, wo, hsel, hselt)


def _conv_kernel(x_ref, mod_ref, vec_ref, pw1_ref, pw1b_ref, dw_ref, pw2_ref, o_ref,
                 pad_ref, cv_ref, *, tm, seg):
    d = x_ref.shape[-1]
    nlb = d // LANE
    nseg = tm // seg
    mod = mod_ref[0]
    shift, scale, gt = mod[0:1], mod[1:2], mod[2:3]
    vec = vec_ref[...]
    g, dw_b, ln_g, ln_b, pw2_b = vec[0:1], vec[1:2], vec[2:3], vec[3:4], vec[4:5]
    x = x_ref[0]
    h = _rms_mod(x, g, shift, scale)
    u = _dot(_bf(h), pw1_ref[...]) + pw1b_ref[...]
    u = u[:, :d] * jax.nn.sigmoid(u[:, d:])

    zpad = jnp.zeros((nseg, CONV_PAD, LANE), F32)
    for lb in range(nlb):
        pad_ref[:, lb, 0:CONV_PAD, :] = zpad
        pad_ref[:, lb, CONV_PAD + seg:2 * CONV_PAD + seg, :] = zpad
        pad_ref[:, lb, CONV_PAD:CONV_PAD + seg, :] = u[:, lb * LANE:(lb + 1) * LANE].reshape(nseg, seg, LANE)

    base = CONV_PAD - CONV_WIDTH // 2

    def body(idx, carry):
        s = idx // nlb
        lb = idx % nlb
        acc = jnp.zeros((seg, LANE), F32)
        for j in range(CONV_WIDTH):
            acc = acc + pad_ref[s, lb, base + j:base + j + seg, :] * dw_ref[j, lb]
        cv_ref[s, lb] = acc
        return carry

    lax.fori_loop(0, nseg * nlb, body, 0, unroll=4)

    cv = jnp.concatenate([cv_ref[:, lb].reshape(tm, LANE) for lb in range(nlb)], axis=1) + dw_b
    mu = jnp.mean(cv, axis=-1, keepdims=True)
    cc = cv - mu
    var = jnp.mean(cc * cc, axis=-1, keepdims=True)
    z = cc * lax.rsqrt(var + LN_EPS) * ln_g + ln_b
    z = z * jax.nn.sigmoid(z)
    out = _dot(_bf(z), pw2_ref[...]) + pw2_b
    o_ref[0] = x + gt * out


def _conv_call(x, mod, per_batch, seg, vec, pw1, pw1b, dw, pw2):
    b, t, d = x.shape
    tm = min(512, t)
    seg = min(seg, tm)
    nlb = d // LANE
    mod_map = (lambda i, j: (i, 0, 0)) if per_batch else (lambda i, j: (0, 0, 0))
    tile = pl.BlockSpec((1, tm, d), lambda i, j: (i, j, 0))
    return pl.pallas_call(
        functools.partial(_conv_kernel, tm=tm, seg=seg),
        grid=(b, t // tm),
        in_specs=[tile, pl.BlockSpec((1, 6, d), mod_map), _const_spec(vec.shape),
                  _const_spec(pw1.shape), _const_spec(pw1b.shape), _const_spec(dw.shape),
                  _const_spec(pw2.shape)],
        out_specs=tile,
        out_shape=jax.ShapeDtypeStruct((b, t, d), F32),
        scratch_shapes=[pltpu.VMEM((tm // seg, nlb, seg + 2 * CONV_PAD, LANE), F32),
                        pltpu.VMEM((tm // seg, nlb, seg, LANE), F32)],
        compiler_params=_params(("arbitrary", "arbitrary")),
        name="conv_module",
    )(x, mod, vec, pw1, pw1b, dw, pw2)


def _router_gates_t(h, rwt_ref, rb_ref):
    h_hi, h_lo = _split(h)
    w_hi, w_lo = _split(rwt_ref[...])
    n_e = w_hi.shape[0]
    hh = _dot_nt(jnp.concatenate([w_hi, w_lo], axis=0), h_hi)
    logits = hh[:n_e] + hh[n_e:] + _dot_nt(w_hi, h_lo) + rb_ref[...]
    m = jnp.max(logits, axis=0, keepdims=True)
    ex = jnp.exp(logits - m)
    p = ex / jnp.sum(ex, axis=0, keepdims=True)
    rows = [p[e:e + 1] for e in range(N_EXPERT_GROUPS * EXPERTS_PER_GROUP)]
    scores = []
    for gi in range(N_EXPERT_GROUPS):
        a, b, c, dd = rows[4 * gi:4 * gi + 4]
        hi1, lo1 = jnp.maximum(a, b), jnp.minimum(a, b)
        hi2, lo2 = jnp.maximum(c, dd), jnp.minimum(c, dd)
        scores.append(jnp.maximum(hi1, hi2) + jnp.maximum(jnp.minimum(hi1, hi2), jnp.maximum(lo1, lo2)))
    best = scores[0]
    sel = jnp.zeros(best.shape, jnp.int32)
    for gi in range(1, N_EXPERT_GROUPS):
        better = scores[gi] > best
        sel = jnp.where(better, gi, sel)
        best = jnp.where(better, scores[gi], best)
    qs = []
    for j in range(EXPERTS_PER_GROUP):
        qj = jnp.zeros(best.shape, F32)
        for gi in range(N_EXPERT_GROUPS):
            qj = jnp.where(sel == gi, rows[4 * gi + j], qj)
        qs.append(qj)

    def argmax4(vals):
        bv, bi = vals[0], jnp.zeros(best.shape, jnp.int32)
        for j in range(1, EXPERTS_PER_GROUP):
            better = vals[j] > bv
            bi = jnp.where(better, j, bi)
            bv = jnp.where(better, vals[j], bv)
        return bv, bi

    v1, i1 = argmax4(qs)
    v2, i2 = argmax4([jnp.where(i1 == j, -1.0, qs[j]) for j in range(EXPERTS_PER_GROUP)])
    den = v1 + v2
    w1, w2 = v1 / den, v2 / den
    gates = []
    for gi in range(N_EXPERT_GROUPS):
        rows_g = [jnp.where(sel == gi, jnp.where(i1 == j, w1, 0.0) + jnp.where(i2 == j, w2, 0.0), 0.0)
                  for j in range(EXPERTS_PER_GROUP)]
        gates.append(jnp.concatenate(rows_g, axis=0))
    return gates, sel


def _moe_kernel(x_ref, mod_ref, vec_ref, rwt_ref, rb_ref, wg_ref, wu_ref, wd_ref, o_ref,
                h_ref, gt_ref, rk_ref, rc_ref, early_ref, xg_ref, yg_ref, wr_ref, acc_ref, nblk_ref,
                *, n_exp, final):
    e = pl.program_id(2)
    tm = x_ref.shape[1]
    g = e // EXPERTS_PER_GROUP
    j = e % EXPERTS_PER_GROUP
    sub8 = 8

    @pl.when((pl.program_id(0) == 0) & (pl.program_id(1) == 0) & (e == 0))
    def _():
        early_ref[...] = _bf(jnp.where(lax.broadcasted_iota(jnp.int32, (tm, tm), 0)
                                       < lax.broadcasted_iota(jnp.int32, (tm, tm), 1), 1.0, 0.0))

    def block_rows(blk, size):
        return pl.ds(pl.multiple_of(blk * MOE_ROWS, MOE_ROWS), size)

    def block_base(blk):
        return (blk * MOE_ROWS).astype(F32)

    @pl.when(e == 0)
    def _():
        mod = mod_ref[0]
        h = _rms_mod(x_ref[0], vec_ref[0:1], mod[3:4], mod[4:5])
        h_ref[...] = _bf(h)
        gates, sel = _router_gates_t(h, rwt_ref, rb_ref)
        zero4 = jnp.zeros((sub8 - EXPERTS_PER_GROUP, tm), F32)
        for gi in range(N_EXPERT_GROUPS):
            gt_ref[gi] = jnp.concatenate([gates[gi], zero4], axis=0)
        chosen = [jnp.where(sel == gi, 1.0, 0.0) for gi in range(N_EXPERT_GROUPS)]
        selmat = _bf(jnp.concatenate(chosen + [jnp.zeros((sub8 - N_EXPERT_GROUPS, tm), F32)], axis=0))
        prefix = _dot(selmat, early_ref[...])
        rk = jnp.where(selmat > 0, prefix, -1.0)
        rk_ref[...] = rk
        rc_ref[...] = jnp.transpose(jnp.concatenate([rk, jnp.zeros((LANE - sub8, tm), F32)], axis=0))
        acc_ref[...] = jnp.zeros(acc_ref.shape, F32)

    @pl.when(j == 0)
    def _():
        sub_g = lax.broadcasted_iota(jnp.int32, (sub8, tm), 0)
        rank_row = jnp.sum(jnp.where(sub_g == g, rk_ref[...], 0.0), axis=0, keepdims=True)
        cnt = jnp.sum(jnp.where(rank_row >= 0.0, 1.0, 0.0)).astype(jnp.int32)
        nblk = (cnt + (MOE_ROWS - 1)) // MOE_ROWS
        nblk_ref[0] = nblk
        nblk_ref[1] = (cnt + (MOE_ROW_STEP - 1)) // MOE_ROW_STEP
        g_hi, g_lo = _split(gt_ref[g])
        slot = lax.broadcasted_iota(jnp.int32, (MOE_ROWS, tm), 0).astype(F32)

        def compact(blk, carry):
            rows = block_rows(blk, MOE_ROWS)
            pb = _bf(jnp.where(rank_row - block_base(blk) == slot, 1.0, 0.0))
            xg_ref[rows, :] = _bf(_dot(pb, h_ref[...]))
            wr_ref[rows, :] = _dot_nt(pb, g_hi) + _dot_nt(pb, g_lo)
            yg_ref[rows, :] = jnp.zeros((MOE_ROWS, yg_ref.shape[1]), F32)
            return carry

        steps = nblk_ref[1]
        spb = MOE_ROWS // MOE_ROW_STEP
        for ns in range(spb + 1, 2 * spb + 1):
            size = ns * MOE_ROW_STEP if ns < 2 * spb - 1 else 2 * MOE_ROWS

            @pl.when(steps == ns)
            def _(size=size):
                slot_s = lax.broadcasted_iota(jnp.int32, (size, tm), 0).astype(F32)
                pb = _bf(jnp.where(rank_row == slot_s, 1.0, 0.0))
                xg_ref[0:size, :] = _bf(_dot(pb, h_ref[...]))
                wr_ref[0:size, :] = _dot_nt(pb, g_hi) + _dot_nt(pb, g_lo)
                yg_ref[0:2 * MOE_ROWS, :] = jnp.zeros((2 * MOE_ROWS, yg_ref.shape[1]), F32)

        @pl.when((steps <= spb) | (steps > 2 * spb))
        def _():
            lax.fori_loop(0, nblk, compact, 0)

    def expert(blk, size):
        rows = block_rows(blk, size)
        xb = xg_ref[rows, :]
        hg = _dot(xb, wg_ref[0, 0])
        hu = _dot(xb, wu_ref[0, 0])
        lane8 = lax.broadcasted_iota(jnp.int32, (size, sub8), 1)
        wcol = jnp.sum(jnp.where(lane8 == j, wr_ref[rows, :], 0.0), axis=1, keepdims=True)
        he = hg * jax.nn.sigmoid(hg) * hu * wcol
        yg_ref[rows, :] += _dot(_bf(he), wd_ref[0, 0])

    n_used = nblk_ref[0]
    n_steps = nblk_ref[1]

    assert xg_ref.shape[0] >= 2 * MOE_ROWS
    steps_per_block = MOE_ROWS // MOE_ROW_STEP
    trimmed = range(steps_per_block + 1, 2 * steps_per_block - 1)
    for ns in trimmed:
        @pl.when(n_steps == ns)
        def _(ns=ns):
            expert(0, ns * MOE_ROW_STEP)

    @pl.when((n_steps < trimmed.start) | (n_steps >= trimmed.stop))
    def _():
        n_pairs = lax.shift_right_logical(n_used, 1)

        def expert_pair(p, carry):
            expert(2 * p, 2 * MOE_ROWS)
            return carry

        lax.fori_loop(0, n_pairs, expert_pair, 0)

        @pl.when(n_used - 2 * n_pairs == 1)
        def _():
            expert(n_used - 1, MOE_ROWS)

    @pl.when(j == EXPERTS_PER_GROUP - 1)
    def _():
        lane_g = lax.broadcasted_iota(jnp.int32, (tm, LANE), 1)
        rank_col = jnp.sum(jnp.where(lane_g == g, rc_ref[...], 0.0), axis=1, keepdims=True)
        slot_l = lax.broadcasted_iota(jnp.int32, (tm, MOE_ROWS), 1).astype(F32)

        def spread(blk, carry):
            pt = _bf(jnp.where(rank_col - block_base(blk) == slot_l, 1.0, 0.0))
            acc_ref[...] += _dot(pt, _bf(yg_ref[block_rows(blk, MOE_ROWS), :]))
            return carry

        slot2 = lax.broadcasted_iota(jnp.int32, (tm, 2 * MOE_ROWS), 1).astype(F32)

        @pl.when(n_used == 2)
        def _():
            pt2 = _bf(jnp.where(rank_col == slot2, 1.0, 0.0))
            acc_ref[...] += _dot(pt2, _bf(yg_ref[0:2 * MOE_ROWS, :]))

        @pl.when(n_used != 2)
        def _():
            lax.fori_loop(0, n_used, spread, 0)

    @pl.when(e == n_exp - 1)
    def _():
        out = x_ref[0] + mod_ref[0][5:6] * acc_ref[...]
        if final:
            out = out * lax.rsqrt(jnp.mean(out * out, axis=-1, keepdims=True) + RMS_EPS) * vec_ref[1:2]
        o_ref[0] = out


def _moe_call(x, mod, per_batch, vec, rwt, rb, wg, wu, wd, layer, final):
    b, t, d = x.shape
    tm = min(1024, t)
    cap = max(-(-tm // MOE_ROWS), 2) * MOE_ROWS
    _, n_exp, _, de = wg.shape
    mod_map = (lambda i, j, e: (i, 0, 0)) if per_batch else (lambda i, j, e: (0, 0, 0))
    tile = pl.BlockSpec((1, tm, d), lambda i, j, e: (i, j, 0))
    cst = lambda shape: pl.BlockSpec(shape, lambda i, j, e: (0,) * len(shape))
    return pl.pallas_call(
        functools.partial(_moe_kernel, n_exp=n_exp, final=final),
        grid=(b, t // tm, n_exp),
        in_specs=[tile, pl.BlockSpec((1, 6, d), mod_map), cst(vec.shape), cst(rwt.shape), cst(rb.shape),
                  pl.BlockSpec((1, 1, d, de), lambda i, j, e: (layer, e, 0, 0)),
                  pl.BlockSpec((1, 1, d, de), lambda i, j, e: (layer, e, 0, 0)),
                  pl.BlockSpec((1, 1, de, d), lambda i, j, e: (layer, e, 0, 0))],
        out_specs=tile,
        out_shape=jax.ShapeDtypeStruct((b, t, d), F32),
        scratch_shapes=[
            pltpu.VMEM((tm, d), BF16),
            pltpu.VMEM((N_EXPERT_GROUPS, 8, tm), F32),
            pltpu.VMEM((8, tm), F32),
            pltpu.VMEM((tm, LANE), F32),
            pltpu.VMEM((tm, tm), BF16),
            pltpu.VMEM((cap, d), BF16),
            pltpu.VMEM((cap, d), F32),
            pltpu.VMEM((cap, 8), F32),
            pltpu.VMEM((tm, d), F32),
            pltpu.SMEM((2,), jnp.int32),
        ],
        compiler_params=_params(("arbitrary", "arbitrary", "arbitrary")),
        name="grouped_moe",
    )(x, mod, vec, rwt, rb, wg, wu, wd)


def _pad_rows(a, rows):
    return jnp.concatenate([a, jnp.zeros((rows - a.shape[0],) + a.shape[1:], a.dtype)], axis=0)


def _lane_stacked_state(s0):
    b, two, h, n, _ = s0.shape
    return jnp.swapaxes(s0, 2, 3).reshape(b, two, n, h * n)


def kernel(x_prompt, x_sample, state_rwkv, c, c_ctx, norm_g, ada_w, ada_b, final_g, rwkv_mu, rwkv_w_rkv, rwkv_w_o, rwkv_w0, rwkv_w1, rwkv_w2, rwkv_a0, rwkv_a1, rwkv_a2, rwkv_g1, rwkv_g2, rwkv_k_k, rwkv_k_a, rwkv_r_k, rwkv_gn_g, rwkv_gn_b, conv_pw1, conv_pw1_b, conv_dw, conv_dw_b, conv_ln_g, conv_ln_b, conv_pw2, conv_pw2_b, router_w, router_b, moe_w_gate, moe_w_up, moe_w_down):
    d = x_prompt.shape[-1]
    depth = ada_w.shape[0]
    n_heads = d // HEAD_DIM
    dec_b = c.shape[0]

    cond = _pad_rows(jnp.concatenate([c_ctx[None, :], c], axis=0), COND_ROWS)
    mod = _mod_call(cond, ada_w, ada_b).reshape(depth, COND_ROWS, 6, d)

    head_of_lane = jnp.arange(d) // HEAD_DIM
    hsel = (head_of_lane[:, None] == jnp.arange(LANE)[None, :]).astype(BF16)
    hselt = jnp.concatenate([hsel.T, hsel.T], axis=0)
    rwt = router_w.T
    rb = router_b[:, None]
    wg = _bf(moe_w_gate)
    wu = _bf(moe_w_up)
    wd = _bf(moe_w_down)

    def lora_pad(w2):
        z = jnp.zeros_like(w2[0])
        return _bf(jnp.stack([jnp.concatenate([w2[0], z], 0), jnp.concatenate([z, w2[1]], 0)]))

    groups = (
        dict(x=x_prompt, rows=slice(0, 1), per_batch=False, s0=None, want_state=True, seg=x_prompt.shape[1]),
        dict(x=x_sample, rows=slice(1, 1 + dec_b), per_batch=True, s0=state_rwkv, want_state=False, seg=GRID_W),
    )
    outs = []
    new_state = None
    for gr in groups:
        x = gr["x"]
        states = []
        for i in range(depth):
            m_i = mod[i, gr["rows"]]
            j = i // 2
            if i % 2 == 0:
                vec = jnp.stack([norm_g[i, 0], rwkv_k_k[j], rwkv_k_a[j], rwkv_r_k[j].reshape(d),
                                 rwkv_w0[j, 0], rwkv_w0[j, 1], rwkv_a0[j, 0], rwkv_a0[j, 1]])
                r, v, kkn, logd, kd, bd, gate, bonus = _rwkv_pre_call(
                    x, m_i, gr["per_batch"], vec, rwkv_mu[j], _bf(rwkv_w_rkv[j]), _bf(rwkv_g1[j]), _bf(rwkv_g2[j]),
                    _bf(jnp.concatenate([rwkv_w1[j, 0], rwkv_w1[j, 1]], axis=1)), lora_pad(rwkv_w2[j]),
                    _bf(jnp.concatenate([rwkv_a1[j, 0], rwkv_a1[j, 1]], axis=1)), lora_pad(rwkv_a2[j]),
                    hsel, hselt)
                s0_bd = None if gr["s0"] is None else _lane_stacked_state(gr["s0"][:, j])
                res = _wkv_call(r, v, kkn, logd, kd, bd, s0_bd, gr["want_state"])
                if gr["want_state"]:
                    states.append(res[2])
                vec = _pad_rows(jnp.stack([rwkv_gn_g[j], rwkv_gn_b[j]]), 8)
                x = _rwkv_post_call(x, res[0], res[1], bonus, gate, m_i, gr["per_batch"], vec,
                                    _bf(rwkv_w_o[j]), hsel, hselt)
            else:
                vec = _pad_rows(jnp.stack([norm_g[i, 0], conv_dw_b[j], conv_ln_g[j], conv_ln_b[j], conv_pw2_b[j]]), 8)
                dw = conv_dw[j].reshape(CONV_WIDTH, d // LANE, 1, LANE)
                x = _conv_call(x, m_i, gr["per_batch"], gr["seg"], vec, _bf(conv_pw1[j]),
                               conv_pw1_b[j][None, :], dw, _bf(conv_pw2[j]))
            vec = _pad_rows(jnp.stack([norm_g[i, 1], final_g]), 8)
            xm = x if gr["per_batch"] else x.reshape(1, -1, d)
            xm = _moe_call(xm, m_i, gr["per_batch"], vec, rwt, rb, wg, wu, wd, i, final=(i == depth - 1))
            x = xm.reshape(x.shape)
        outs.append(x)
        if gr["want_state"]:
            new_state = jnp.stack(states, axis=1)
    return (outs[0], outs[1], new_state)
```
